```python
import math
import jax, jax.numpy as jnp
from jax import lax
import numpy as np

D_MODEL = 1024
BATCH = 8
SEQ = 16384
DEPTH = 2

CHUNK = 64
Q_BLOCK = 128
A_HEADS = 8
A_HEAD_DIM = 64
A_WIDTH = A_HEADS * A_HEAD_DIM
B_EXPAND = 128
B_WIDTH = D_MODEL // 2
B_HEADS = B_WIDTH // B_EXPAND
B_DK = B_EXPAND
B_DV = B_WIDTH // B_HEADS
FFN_HIDDEN = int(math.ceil((8 * D_MODEL / 3) / 256) * 256)
ALPHA = (2 * DEPTH) ** 0.25
BETA = (8 * DEPTH) ** -0.25
LN_EPS = 1e-5
RMS_EPS = 1e-6

IN_WIDTHS = [A_WIDTH, A_WIDTH, A_WIDTH, A_HEADS,
             B_WIDTH, B_WIDTH, B_WIDTH, B_WIDTH,
             D_MODEL, D_MODEL]
IN_TOTAL = int(sum(IN_WIDTHS))
IN_SPLITS = [int(s) for s in np.cumsum(IN_WIDTHS)[:-1]]

kernel_name = "fox_hgrn2_gated_hybrid_deepnorm"


def layer_norm(x, g, b):
    xf = x.astype(jnp.float32)
    mu = jnp.mean(xf, axis=-1, keepdims=True)
    var = jnp.mean(jnp.square(xf - mu), axis=-1, keepdims=True)
    y = (xf - mu) * lax.rsqrt(var + LN_EPS) * g.astype(jnp.float32) + b.astype(jnp.float32)
    return y.astype(x.dtype)


def fox_attention(q, k, v, log_f):
    B, S, H, dh = q.shape
    nb = S // Q_BLOCK
    F = jnp.cumsum(log_f, axis=1).transpose(0, 2, 1)
    kh = k.transpose(0, 2, 1, 3)
    vh = v.transpose(0, 2, 1, 3)
    qb = q.reshape(B, nb, Q_BLOCK, H, dh).transpose(1, 0, 3, 2, 4)
    fq = F.reshape(B, H, nb, Q_BLOCK).transpose(2, 0, 1, 3)
    starts = jnp.arange(nb, dtype=jnp.int32) * Q_BLOCK
    key_pos = jnp.arange(S, dtype=jnp.int32)
    scale = dh ** -0.5

    def one_block(args):
        q_i, fq_i, start = args
        s = jnp.einsum('bhqd,bhkd->bhqk', q_i, kh).astype(jnp.float32) * scale
        s = s + fq_i[..., None] - F[:, :, None, :]
        q_pos = start + jnp.arange(Q_BLOCK, dtype=jnp.int32)
        causal = key_pos[None, :] <= q_pos[:, None]
        p = jax.nn.softmax(jnp.where(causal, s, -jnp.inf), axis=-1)
        return jnp.einsum('bhqk,bhkd->bhqd', p.astype(vh.dtype), vh)

    o = lax.map(one_block, (qb, fq, starts))
    return o.transpose(1, 0, 3, 2, 4).reshape(B, S, H * dh)


def hgrn2_chunkwise(q, k, v, log_f):
    B, S, H, dk = q.shape
    dv = v.shape[-1]
    nc = S // CHUNK

    def to_chunks(t):
        return t.reshape(B, nc, CHUNK, H, t.shape[-1]).transpose(1, 0, 3, 2, 4)

    qc, kc, vc, gc = to_chunks(q), to_chunks(k), to_chunks(v), to_chunks(log_f)
    incl = jnp.tril(jnp.ones((CHUNK, CHUNK), dtype=bool))

    def step(state, inp):
        q_i, k_i, v_i, g_i = inp
        b = jnp.cumsum(g_i, axis=2)
        o_inter = jnp.einsum('bhtd,bhde->bhte', q_i * jnp.exp(b), state)
        diff = b[:, :, :, None, :] - b[:, :, None, :, :]
        decay = jnp.exp(jnp.where(incl[None, None, :, :, None], diff, -jnp.inf))
        attn = jnp.einsum('bhtd,bhtsd,bhsd->bhts', q_i, decay, k_i)
        o_intra = jnp.einsum('bhts,bhse->bhte', attn, v_i)
        b_last = b[:, :, -1:, :]
        k_dec = k_i * jnp.exp(b_last - b)
        new_state = (jnp.exp(b_last[:, :, 0, :])[..., None] * state
                     + jnp.einsum('bhsd,bhse->bhde', k_dec, v_i))
        return new_state, o_inter + o_intra

    s0 = jnp.zeros((B, H, dk, dv), jnp.float32)
    _, o = lax.scan(step, s0, (qc, kc, vc, gc))
    return o.transpose(1, 0, 3, 2, 4).reshape(B, S, H, dv)


def hybrid_mixer(x, w_in, b_f, lb, norm_g, w_pa, w_pb, w_o):
    B, S, _ = x.shape
    proj = x @ w_in
    q_a, k_a, v_a, f_a, q_b, f_b, i_b, g_b, gate_a, gate_b = jnp.split(proj, IN_SPLITS, axis=-1)

    log_fa = jax.nn.log_sigmoid(f_a.astype(jnp.float32) + b_f.astype(jnp.float32))
    shp_a = (B, S, A_HEADS, A_HEAD_DIM)
    y_a = fox_attention(q_a.reshape(shp_a), k_a.reshape(shp_a), v_a.reshape(shp_a), log_fa)

    lbh = lb.astype(jnp.float32).reshape(B_HEADS, B_DK)
    f = lbh + (1.0 - lbh) * jax.nn.sigmoid(f_b.astype(jnp.float32).reshape(B, S, B_HEADS, B_DK))
    qh = jax.nn.silu(q_b.astype(jnp.float32)).reshape(B, S, B_HEADS, B_DK)
    vh = i_b.astype(jnp.float32).reshape(B, S, B_HEADS, B_DV)
    o_b = hgrn2_chunkwise(qh, 1.0 - f, vh, jnp.log(f))
    o_b = o_b * lax.rsqrt(jnp.mean(jnp.square(o_b), axis=-1, keepdims=True) + RMS_EPS)
    o_b = o_b * norm_g.astype(jnp.float32)
    y_b = (o_b.reshape(B, S, B_WIDTH) * jax.nn.sigmoid(g_b.astype(jnp.float32))).astype(x.dtype)

    p_a = y_a @ w_pa
    p_b = y_b @ w_pb
    merged = jax.nn.sigmoid(gate_a) * p_a + jax.nn.sigmoid(gate_b) * p_b
    return merged @ w_o


def swiglu_ffn(x, w_ff_in, w_ff_out):
    h = x @ w_ff_in
    u, g = jnp.split(h, 2, axis=-1)
    return (jax.nn.silu(g) * u) @ w_ff_out


def _fwd_setup_inputs(seed: int = 0) -> dict:
    key = jax.random.key(seed)
    ks = jax.random.split(key, 16)
    f32 = jnp.float32
    x = jax.random.normal(ks[0], (BATCH, SEQ, D_MODEL), f32)
    col_scale = np.ones((IN_TOTAL,), np.float32)
    col_scale[2 * A_WIDTH:3 * A_WIDTH] = BETA
    v_b_start = 3 * A_WIDTH + A_HEADS + 2 * B_WIDTH
    col_scale[v_b_start:v_b_start + B_WIDTH] = BETA
    w_in = jax.random.normal(ks[1], (DEPTH, D_MODEL, IN_TOTAL), f32) * (D_MODEL ** -0.5) * jnp.asarray(col_scale)
    b_fgate = jnp.linspace(1.0, 5.0, A_HEADS, dtype=f32)[None, :] + 0.1 * jax.random.normal(ks[2], (DEPTH, A_HEADS), f32)
    hgrn_lb_logits = 0.1 * jax.random.normal(ks[3], (DEPTH, B_WIDTH), f32)
    hgrn_norm_g = 1.0 + 0.02 * jax.random.normal(ks[4], (DEPTH, B_DV), f32)
    w_branch_a = jax.random.normal(ks[5], (DEPTH, A_WIDTH, D_MODEL), f32) * (A_WIDTH ** -0.5) * BETA
    w_branch_b = jax.random.normal(ks[6], (DEPTH, B_WIDTH, D_MODEL), f32) * (B_WIDTH ** -0.5) * BETA
    w_out = jax.random.normal(ks[7], (DEPTH, D_MODEL, D_MODEL), f32) * (D_MODEL ** -0.5) * BETA
    ln1_g = 1.0 + 0.02 * jax.random.normal(ks[8], (DEPTH, D_MODEL), f32)
    ln1_b = 0.02 * jax.random.normal(ks[9], (DEPTH, D_MODEL), f32)
    w_ff_in = jax.random.normal(ks[10], (DEPTH, D_MODEL, 2 * FFN_HIDDEN), f32) * (D_MODEL ** -0.5)
    w_ff_out = jax.random.normal(ks[11], (DEPTH, FFN_HIDDEN, D_MODEL), f32) * (FFN_HIDDEN ** -0.5) * BETA
    ln2_g = 1.0 + 0.02 * jax.random.normal(ks[12], (DEPTH, D_MODEL), f32)
    ln2_b = 0.02 * jax.random.normal(ks[13], (DEPTH, D_MODEL), f32)
    return {"x": x, "w_in": w_in, "b_fgate": b_fgate, "hgrn_lb_logits": hgrn_lb_logits,
            "hgrn_norm_g": hgrn_norm_g, "w_branch_a": w_branch_a, "w_branch_b": w_branch_b,
            "w_out": w_out, "ln1_g": ln1_g, "ln1_b": ln1_b, "w_ff_in": w_ff_in,
            "w_ff_out": w_ff_out, "ln2_g": ln2_g, "ln2_b": ln2_b}


def _fwd_reference(x, w_in, b_fgate, hgrn_lb_logits, hgrn_norm_g, w_branch_a, w_branch_b,
              w_out, ln1_g, ln1_b, w_ff_in, w_ff_out, ln2_g, ln2_b):
    sm = jax.nn.softmax(hgrn_lb_logits.astype(jnp.float32), axis=0)
    lower_bounds = jnp.cumsum(sm, axis=0) - sm[0:1]
    for l in range(DEPTH):
        mix = hybrid_mixer(x, w_in[l], b_fgate[l], lower_bounds[l], hgrn_norm_g[l],
                           w_branch_a[l], w_branch_b[l], w_out[l])
        x = layer_norm(ALPHA * x + mix, ln1_g[l], ln1_b[l])
        ffn = swiglu_ffn(x, w_ff_in[l], w_ff_out[l])
        x = layer_norm(ALPHA * x + ffn, ln2_g[l], ln2_b[l])
    return x


import jax as _jax
import jax.numpy as _jnp

TWIN_FORMAT = 'train_step'
FWD_PARAMS = ['x', 'w_in', 'b_fgate', 'hgrn_lb_logits', 'hgrn_norm_g', 'w_branch_a', 'w_branch_b', 'w_out', 'ln1_g', 'ln1_b', 'w_ff_in', 'w_ff_out', 'ln2_g', 'ln2_b']
TWIN_WEIGHTS = ['w_in', 'b_fgate', 'hgrn_lb_logits', 'hgrn_norm_g', 'w_branch_a', 'w_branch_b', 'w_out', 'ln1_g', 'ln1_b', 'w_ff_in', 'w_ff_out', 'ln2_g', 'ln2_b']
TWIN_DIFF_INPUT = 'x'
TWIN_INPUTS = ['x', 'w_in', 'b_fgate', 'hgrn_lb_logits', 'hgrn_norm_g', 'w_branch_a', 'w_branch_b', 'w_out', 'ln1_g', 'ln1_b', 'w_ff_in', 'w_ff_out', 'ln2_g', 'ln2_b', 'loss_target', 'm_w_in', 'm_b_fgate', 'm_hgrn_lb_logits', 'm_hgrn_norm_g', 'm_w_branch_a', 'm_w_branch_b', 'm_w_out', 'm_ln1_g', 'm_ln1_b', 'm_w_ff_in', 'm_w_ff_out', 'm_ln2_g', 'm_ln2_b', 'v_w_in', 'v_b_fgate', 'v_hgrn_lb_logits', 'v_hgrn_norm_g', 'v_w_branch_a', 'v_w_branch_b', 'v_w_out', 'v_ln1_g', 'v_ln1_b', 'v_w_ff_in', 'v_w_ff_out', 'v_ln2_g', 'v_ln2_b']
TWIN_OUTPUTS = ['loss', 'grad_x', 'grad_w_in', 'grad_b_fgate', 'grad_hgrn_lb_logits', 'grad_hgrn_norm_g', 'grad_w_branch_a', 'grad_w_branch_b', 'grad_w_out', 'grad_ln1_g', 'grad_ln1_b', 'grad_w_ff_in', 'grad_w_ff_out', 'grad_ln2_g', 'grad_ln2_b', 'delta_w_in', 'delta_b_fgate', 'delta_hgrn_lb_logits', 'delta_hgrn_norm_g', 'delta_w_branch_a', 'delta_w_branch_b', 'delta_w_out', 'delta_ln1_g', 'delta_ln1_b', 'delta_w_ff_in', 'delta_w_ff_out', 'delta_ln2_g', 'delta_ln2_b', 'new_m_w_in', 'new_m_b_fgate', 'new_m_hgrn_lb_logits', 'new_m_hgrn_norm_g', 'new_m_w_branch_a', 'new_m_w_branch_b', 'new_m_w_out', 'new_m_ln1_g', 'new_m_ln1_b', 'new_m_w_ff_in', 'new_m_w_ff_out', 'new_m_ln2_g', 'new_m_ln2_b', 'new_v_w_in', 'new_v_b_fgate', 'new_v_hgrn_lb_logits', 'new_v_hgrn_norm_g', 'new_v_w_branch_a', 'new_v_w_branch_b', 'new_v_w_out', 'new_v_ln1_g', 'new_v_ln1_b', 'new_v_w_ff_in', 'new_v_w_ff_out', 'new_v_ln2_g', 'new_v_ln2_b']
TWIN_LEAF_KINDS = {'loss': 'loss', 'grad_x': 'grad_x', 'grad_w_in': 'grad_w', 'grad_b_fgate': 'grad_w', 'grad_hgrn_lb_logits': 'grad_w', 'grad_hgrn_norm_g': 'grad_w', 'grad_w_branch_a': 'grad_w', 'grad_w_branch_b': 'grad_w', 'grad_w_out': 'grad_w', 'grad_ln1_g': 'grad_w', 'grad_ln1_b': 'grad_w', 'grad_w_ff_in': 'grad_w', 'grad_w_ff_out': 'grad_w', 'grad_ln2_g': 'grad_w', 'grad_ln2_b': 'grad_w', 'delta_w_in': 'delta_w', 'delta_b_fgate': 'delta_w', 'delta_hgrn_lb_logits': 'delta_w', 'delta_hgrn_norm_g': 'delta_w', 'delta_w_branch_a': 'delta_w', 'delta_w_branch_b': 'delta_w', 'delta_w_out': 'delta_w', 'delta_ln1_g': 'delta_w', 'delta_ln1_b': 'delta_w', 'delta_w_ff_in': 'delta_w', 'delta_w_ff_out': 'delta_w', 'delta_ln2_g': 'delta_w', 'delta_ln2_b': 'delta_w', 'new_m_w_in': 'new_m', 'new_m_b_fgate': 'new_m', 'new_m_hgrn_lb_logits': 'new_m', 'new_m_hgrn_norm_g': 'new_m', 'new_m_w_branch_a': 'new_m', 'new_m_w_branch_b': 'new_m', 'new_m_w_out': 'new_m', 'new_m_ln1_g': 'new_m', 'new_m_ln1_b': 'new_m', 'new_m_w_ff_in': 'new_m', 'new_m_w_ff_out': 'new_m', 'new_m_ln2_g': 'new_m', 'new_m_ln2_b': 'new_m', 'new_v_w_in': 'new_v', 'new_v_b_fgate': 'new_v', 'new_v_hgrn_lb_logits': 'new_v', 'new_v_hgrn_norm_g': 'new_v', 'new_v_w_branch_a': 'new_v', 'new_v_w_branch_b': 'new_v', 'new_v_w_out': 'new_v', 'new_v_ln1_g': 'new_v', 'new_v_ln1_b': 'new_v', 'new_v_w_ff_in': 'new_v', 'new_v_w_ff_out': 'new_v', 'new_v_ln2_g': 'new_v', 'new_v_ln2_b': 'new_v'}


def _forward(args):
    return _fwd_reference(*[args[k] for k in FWD_PARAMS])


def _output_shape():
    def fwd():
        inp = _fwd_setup_inputs(0)
        return _fwd_reference(*[inp[k] for k in FWD_PARAMS])
    out = _jax.eval_shape(fwd)
    return out.shape, out.dtype

N_MICROBATCH = 1
ADAM_LR = 0.001
ADAM_B1 = 0.9
ADAM_B2 = 0.999
ADAM_EPS = 1e-08
ADAM_WD = 0.01
ADAM_STEP = 10
PER_EXAMPLE_BATCH_AXIS = {'x': 0, 'loss_target': 0}
SHARED_INPUTS = []
_WEIGHT_DTYPES = {'w_in': _jnp.float32, 'b_fgate': _jnp.float32, 'hgrn_lb_logits': _jnp.float32, 'hgrn_norm_g': _jnp.float32, 'w_branch_a': _jnp.float32, 'w_branch_b': _jnp.float32, 'w_out': _jnp.float32, 'ln1_g': _jnp.float32, 'ln1_b': _jnp.float32, 'w_ff_in': _jnp.float32, 'w_ff_out': _jnp.float32, 'ln2_g': _jnp.float32, 'ln2_b': _jnp.float32}
MOMENT_SCALE = {'w_in': 1.805543e-02, 'b_fgate': 2.986212e-02, 'hgrn_lb_logits': 2.593239e-03, 'hgrn_norm_g': 5.166983e-02, 'w_branch_a': 1.271444e-02, 'w_branch_b': 3.837554e-02, 'w_out': 3.996833e-02, 'ln1_g': 3.872634e+00, 'ln1_b': 1.782440e+00, 'w_ff_in': 4.543746e-02, 'w_ff_out': 1.488183e-01, 'ln2_g': 9.074394e+01, 'ln2_b': 3.095493e+00}


def _to_microbatches(a, axis):
    t = _jnp.moveaxis(a, axis, 0)
    t = t.reshape((N_MICROBATCH, t.shape[0] // N_MICROBATCH) + t.shape[1:])
    return _jnp.moveaxis(t, 1, axis + 1)


def setup_inputs(seed: int = 0) -> dict:
    inp = _fwd_setup_inputs(seed)
    key = _jax.random.fold_in(_jax.random.key(seed), 7919)
    shape, _ = _output_shape()
    out = dict(inp)
    out["loss_target"] = _jax.random.normal(_jax.random.fold_in(key, 0), shape, _jnp.float32)
    for i, name in enumerate(TWIN_WEIGHTS):
        w = inp[name].astype(_jnp.float32)
        if MOMENT_SCALE is None:
            s = _jnp.sqrt(_jnp.mean(_jnp.square(w)) + 1e-30)
        else:
            s = MOMENT_SCALE[name]
        km, kv = _jax.random.split(_jax.random.fold_in(key, i + 1))
        out[name] = w
        out["m_" + name] = s * _jax.random.normal(km, w.shape, _jnp.float32)
        out["v_" + name] = (s * s) * _jax.random.uniform(kv, w.shape, _jnp.float32, 0.5, 1.5)
    if N_MICROBATCH > 1:
        for name, axis in PER_EXAMPLE_BATCH_AXIS.items():
            out[name] = _to_microbatches(out[name], axis)
    return {'x': out['x'], 'w_in': out['w_in'], 'b_fgate': out['b_fgate'], 'hgrn_lb_logits': out['hgrn_lb_logits'], 'hgrn_norm_g': out['hgrn_norm_g'], 'w_branch_a': out['w_branch_a'], 'w_branch_b': out['w_branch_b'], 'w_out': out['w_out'], 'ln1_g': out['ln1_g'], 'ln1_b': out['ln1_b'], 'w_ff_in': out['w_ff_in'], 'w_ff_out': out['w_ff_out'], 'ln2_g': out['ln2_g'], 'ln2_b': out['ln2_b'], 'loss_target': out['loss_target'], 'm_w_in': out['m_w_in'], 'm_b_fgate': out['m_b_fgate'], 'm_hgrn_lb_logits': out['m_hgrn_lb_logits'], 'm_hgrn_norm_g': out['m_hgrn_norm_g'], 'm_w_branch_a': out['m_w_branch_a'], 'm_w_branch_b': out['m_w_branch_b'], 'm_w_out': out['m_w_out'], 'm_ln1_g': out['m_ln1_g'], 'm_ln1_b': out['m_ln1_b'], 'm_w_ff_in': out['m_w_ff_in'], 'm_w_ff_out': out['m_w_ff_out'], 'm_ln2_g': out['m_ln2_g'], 'm_ln2_b': out['m_ln2_b'], 'v_w_in': out['v_w_in'], 'v_b_fgate': out['v_b_fgate'], 'v_hgrn_lb_logits': out['v_hgrn_lb_logits'], 'v_hgrn_norm_g': out['v_hgrn_norm_g'], 'v_w_branch_a': out['v_w_branch_a'], 'v_w_branch_b': out['v_w_branch_b'], 'v_w_out': out['v_w_out'], 'v_ln1_g': out['v_ln1_g'], 'v_ln1_b': out['v_ln1_b'], 'v_w_ff_in': out['v_w_ff_in'], 'v_w_ff_out': out['v_w_ff_out'], 'v_ln2_g': out['v_ln2_g'], 'v_ln2_b': out['v_ln2_b']}


def _loss(weights, diff, rest, loss_target):
    with _jax.named_scope("forward"):
        args = {**rest, TWIN_DIFF_INPUT: diff, **{k: w.astype(_WEIGHT_DTYPES[k]) for k, w in weights.items()}}
        y = _forward(args)
    with _jax.named_scope("loss_head"):
        err = _jnp.square(y.astype(_jnp.float32) - loss_target)
        return 0.5 * _jnp.sum(_jnp.mean(err, axis=-1)) if err.ndim else 0.5 * err


def _adamw(w, g, m, v):
    m = ADAM_B1 * m + (1.0 - ADAM_B1) * g
    v = ADAM_B2 * v + (1.0 - ADAM_B2) * _jnp.square(g)
    m_hat = m / (1.0 - ADAM_B1 ** ADAM_STEP)
    v_hat = v / (1.0 - ADAM_B2 ** ADAM_STEP)
    delta = -ADAM_LR * (m_hat / (_jnp.sqrt(v_hat) + ADAM_EPS) + ADAM_WD * w)
    return delta, m, v


def reference(x, w_in, b_fgate, hgrn_lb_logits, hgrn_norm_g, w_branch_a, w_branch_b, w_out, ln1_g, ln1_b, w_ff_in, w_ff_out, ln2_g, ln2_b, loss_target, m_w_in, m_b_fgate, m_hgrn_lb_logits, m_hgrn_norm_g, m_w_branch_a, m_w_branch_b, m_w_out, m_ln1_g, m_ln1_b, m_w_ff_in, m_w_ff_out, m_ln2_g, m_ln2_b, v_w_in, v_b_fgate, v_hgrn_lb_logits, v_hgrn_norm_g, v_w_branch_a, v_w_branch_b, v_w_out, v_ln1_g, v_ln1_b, v_w_ff_in, v_w_ff_out, v_ln2_g, v_ln2_b):
    given = dict(x=x, w_in=w_in, b_fgate=b_fgate, hgrn_lb_logits=hgrn_lb_logits, hgrn_norm_g=hgrn_norm_g, w_branch_a=w_branch_a, w_branch_b=w_branch_b, w_out=w_out, ln1_g=ln1_g, ln1_b=ln1_b, w_ff_in=w_ff_in, w_ff_out=w_ff_out, ln2_g=ln2_g, ln2_b=ln2_b, loss_target=loss_target, m_w_in=m_w_in, m_b_fgate=m_b_fgate, m_hgrn_lb_logits=m_hgrn_lb_logits, m_hgrn_norm_g=m_hgrn_norm_g, m_w_branch_a=m_w_branch_a, m_w_branch_b=m_w_branch_b, m_w_out=m_w_out, m_ln1_g=m_ln1_g, m_ln1_b=m_ln1_b, m_w_ff_in=m_w_ff_in, m_w_ff_out=m_w_ff_out, m_ln2_g=m_ln2_g, m_ln2_b=m_ln2_b, v_w_in=v_w_in, v_b_fgate=v_b_fgate, v_hgrn_lb_logits=v_hgrn_lb_logits, v_hgrn_norm_g=v_hgrn_norm_g, v_w_branch_a=v_w_branch_a, v_w_branch_b=v_w_branch_b, v_w_out=v_w_out, v_ln1_g=v_ln1_g, v_ln1_b=v_ln1_b, v_w_ff_in=v_w_ff_in, v_w_ff_out=v_w_ff_out, v_ln2_g=v_ln2_g, v_ln2_b=v_ln2_b)
    weights = {n: given[n] for n in TWIN_WEIGHTS}
    shared = {n: given[n] for n in SHARED_INPUTS}
    per_example = {n: given[n] for n in ['x']}
    grad_fn = _jax.value_and_grad(_loss, argnums=(0, 1))

    def one_microbatch(ex, loss_target):
        ex = dict(ex)
        diff = ex.pop(TWIN_DIFF_INPUT)
        return grad_fn(weights, diff, {**shared, **ex}, loss_target)

    if N_MICROBATCH == 1:
        loss, (grad_w, grad_x) = one_microbatch(per_example, given["loss_target"])
    else:
        def body(carry, xs):
            loss_sum, grad_sum = carry
            l_k, (gw_k, gx_k) = one_microbatch(xs[0], xs[1])
            with _jax.named_scope("update"):
                return (loss_sum + l_k, _jax.tree.map(_jnp.add, grad_sum, gw_k)), gx_k

        init = (_jnp.zeros((), _jnp.float32), _jax.tree.map(_jnp.zeros_like, weights))
        (loss, grad_w), grad_x = _jax.lax.scan(body, init, (per_example, given["loss_target"]))
    with _jax.named_scope("update"):
        delta_w, new_m, new_v = {}, {}, {}
        for n in TWIN_WEIGHTS:
            delta_w[n], new_m[n], new_v[n] = _adamw(weights[n], grad_w[n], given["m_" + n], given["v_" + n])
    return (loss, grad_x, *[grad_w[n] for n in TWIN_WEIGHTS], *[delta_w[n] for n in TWIN_WEIGHTS],
            *[new_m[n] for n in TWIN_WEIGHTS], *[new_v[n] for n in TWIN_WEIGHTS])
```

```python
import functools
import math

import jax
import jax.numpy as jnp
from jax import lax
from jax.experimental import pallas as pl
from jax.experimental.pallas import tpu as pltpu

F32 = jnp.float32
BF16 = jnp.bfloat16

D_MODEL = 1024
DEPTH = 2
CHUNK = 64
A_HEADS = 8
A_HEAD_DIM = 64
A_WIDTH = 512
B_WIDTH = 512
B_HEADS = 4
B_DK = 128
FFN_HIDDEN = 2816
IN_TOTAL = 5640
ALPHA = (2 * DEPTH) ** 0.25
LN_EPS = 1e-5
RMS_EPS = 1e-6
ADAM_LR = 0.001
ADAM_B1 = 0.9
ADAM_B2 = 0.999
ADAM_EPS = 1e-08
ADAM_WD = 0.01
ADAM_STEP = 10

N_DEV = 8
LANE = 128
VMEM_LIMIT = 48 * 1024 * 1024

QKV_W = 3 * A_WIDTH
C_GA, C_GB2 = 0, D_MODEL
C_QB = 2 * D_MODEL
C_FB, C_IB, C_GB = C_QB + B_WIDTH, C_QB + 2 * B_WIDTH, C_QB + 3 * B_WIDTH
C_FA = C_QB + 4 * B_WIDTH
REST_W = C_FA + LANE
IN_PERM = QKV_W + REST_W


def _cparams(sem):
    return pltpu.CompilerParams(dimension_semantics=sem, vmem_limit_bytes=VMEM_LIMIT)


def _pick(n, pref):
    if n <= pref:
        return n
    t = pref
    while n % t:
        t -= LANE
    return t


def _matmul(a, b, mode, out_dtype, name, *, n_off=0, n_size=None, tm=512, tn=1024, tk=1024):
    if mode == "nn":
        M, K = a.shape
        N = b.shape[1] if n_size is None else n_size
    elif mode == "nt":
        M, K = a.shape
        N = b.shape[0]
    else:
        K, M = a.shape
        N = b.shape[1]
    tm, tn, tk = _pick(M, tm), _pick(N, tn), _pick(K, tk)
    assert n_off % tn == 0
    noff = n_off // tn
    nk = K // tk

    def body(a_ref, b_ref, o_ref, acc_ref):
        k = pl.program_id(2)

        @pl.when(k == 0)
        def _():
            acc_ref[...] = jnp.zeros_like(acc_ref)

        av, bv = a_ref[...], b_ref[...]
        if mode == "nn":
            dn = (((1,), (0,)), ((), ()))
        elif mode == "nt":
            dn = (((1,), (1,)), ((), ()))
        else:
            dn = (((0,), (0,)), ((), ()))
        acc_ref[...] += lax.dot_general(av, bv, dn, preferred_element_type=F32)

        @pl.when(k == nk - 1)
        def _():
            o_ref[...] = acc_ref[...].astype(o_ref.dtype)

    if mode == "nn":
        a_spec = pl.BlockSpec((tm, tk), lambda i, j, k: (i, k))
        b_spec = pl.BlockSpec((tk, tn), lambda i, j, k: (k, j + noff))
    elif mode == "nt":
        a_spec = pl.BlockSpec((tm, tk), lambda i, j, k: (i, k))
        b_spec = pl.BlockSpec((tn, tk), lambda i, j, k: (j, k))
    else:
        a_spec = pl.BlockSpec((tk, tm), lambda i, j, k: (k, i))
        b_spec = pl.BlockSpec((tk, tn), lambda i, j, k: (k, j))
    return pl.pallas_call(
        body,
        name=name,
        grid=(M // tm, N // tn, nk),
        in_specs=[a_spec, b_spec],
        out_specs=pl.BlockSpec((tm, tn), lambda i, j, k: (i, j)),
        out_shape=jax.ShapeDtypeStruct((M, N), out_dtype),
        scratch_shapes=[pltpu.VMEM((tm, tn), F32)],
        compiler_params=_cparams(("parallel", "parallel", "arbitrary")),
    )(a, b)


def _rowwise(fn, name, tiled, params, outs, sums=(), *, tr=256):
    T = tiled[0][0].shape[0]
    tr = _pick(T, tr)
    nt, npar, no, ns = len(tiled), len(params), len(outs), len(sums)

    def body(*refs):
        ins = [r[...] for r in refs[:nt + npar]]
        res = fn(*ins)
        if not isinstance(res, (tuple, list)):
            res = (res,)
        o_refs = refs[nt + npar:nt + npar + no]
        s_refs = refs[nt + npar + no:]
        for r, v in zip(o_refs, res[:no]):
            r[...] = v.astype(r.dtype)
        if ns:
            @pl.when(pl.program_id(0) == 0)
            def _():
                for r in s_refs:
                    r[...] = jnp.zeros_like(r)
            for r, v in zip(s_refs, res[no:]):
                r[...] += v

    in_specs = []
    args = []
    for arr, off, w in tiled:
        assert off % w == 0 or w == arr.shape[1]
        cb = off // w
        in_specs.append(pl.BlockSpec((tr, w), lambda i, cb=cb: (i, cb)))
        args.append(arr)
    for p in params:
        in_specs.append(pl.BlockSpec(p.shape, lambda i, nd=p.ndim: (0,) * nd))
        args.append(p)
    out_specs = [pl.BlockSpec((tr, w), lambda i: (i, 0)) for w, _ in outs]
    out_shape = [jax.ShapeDtypeStruct((T, w), dt) for w, dt in outs]
    out_specs += [pl.BlockSpec((8, w), lambda i: (0, 0)) for w in sums]
    out_shape += [jax.ShapeDtypeStruct((8, w), F32) for w in sums]
    res = pl.pallas_call(
        body,
        name=name,
        grid=(T // tr,),
        in_specs=in_specs,
        out_specs=out_specs,
        out_shape=out_shape,
        compiler_params=_cparams(("arbitrary",)),
    )(*args)
    return res


def _colsum8(v):
    r, w = v.shape
    return jnp.sum(v.reshape(r // 8, 8, w), axis=0)


NEG_BIG = -1e30
QK_SCALE = A_HEAD_DIM ** -0.5
_NT = (((1,), (1,)), ((), ()))
_NN = (((1,), (0,)), ((), ()))
_TN = (((0,), (0,)), ((), ()))


def _causal_mask(tb, transposed=False):
    r = lax.broadcasted_iota(jnp.int32, (tb, tb), 0)
    c = lax.broadcasted_iota(jnp.int32, (tb, tb), 1)
    return (r <= c) if transposed else (r >= c)


def _fox_specs(nb, tb):
    blk = lambda w: pl.BlockSpec((1, 1, tb, w), lambda h, i: (h, i, 0, 0))
    whole = lambda w: pl.BlockSpec((1, nb, tb, w), lambda h, i: (h, 0, 0, 0))
    rows = pl.BlockSpec((1, nb, 1, tb), lambda h, i: (h, 0, 0, 0))
    return blk, whole, rows


def _fox_fwd(q, k, v, fcol, frow):
    H, nb, tb, dh = q.shape
    blk, whole, rows = _fox_specs(nb, tb)

    def body(q_ref, k_ref, v_ref, fq_ref, fk_ref, o_ref, lse_ref, m_ref, l_ref, acc_ref):
        i = pl.program_id(1)
        qv = q_ref[0, 0] * QK_SCALE
        fq = fq_ref[0, 0]
        m_ref[...] = jnp.full_like(m_ref, NEG_BIG)
        l_ref[...] = jnp.zeros_like(l_ref)
        acc_ref[...] = jnp.zeros_like(acc_ref)

        def step(j, masked):
            s = lax.dot_general(qv, k_ref[0, j], _NT, preferred_element_type=F32) + (fq - fk_ref[0, j])
            if masked:
                s = jnp.where(_causal_mask(tb), s, NEG_BIG)
            m_old = m_ref[...]
            m_new = jnp.maximum(m_old, jnp.max(s, axis=1, keepdims=True))
            p = jnp.exp(s - m_new)
            a = jnp.exp(m_old - m_new)
            l_ref[...] = a * l_ref[...] + jnp.sum(p, axis=1, keepdims=True)
            acc_ref[...] = a * acc_ref[...] + lax.dot_general(p.astype(BF16), v_ref[0, j], _NN, preferred_element_type=F32)
            m_ref[...] = m_new

        def off_diag(j, c):
            step(j, False)
            return c

        lax.fori_loop(0, i, off_diag, 0)
        step(i, True)
        o_ref[0, 0] = acc_ref[...] / l_ref[...]
        lse_ref[0, 0] = m_ref[...] + jnp.log(l_ref[...])

    return pl.pallas_call(
        body,
        name="fox_fwd",
        grid=(H, nb),
        in_specs=[blk(dh), whole(dh), whole(dh), blk(1), rows],
        out_specs=[blk(dh), blk(1)],
        out_shape=[jax.ShapeDtypeStruct((H, nb, tb, dh), F32), jax.ShapeDtypeStruct((H, nb, tb, 1), F32)],
        scratch_shapes=[pltpu.VMEM((tb, 1), F32), pltpu.VMEM((tb, 1), F32), pltpu.VMEM((tb, dh), F32)],
        compiler_params=_cparams(("parallel", "arbitrary")),
    )(q, k, v, fcol, frow)


def _fox_dq(q, k, v, do, o, lse, fcol, frow):
    H, nb, tb, dh = q.shape
    blk, whole, rows = _fox_specs(nb, tb)

    def body(q_ref, k_ref, v_ref, do_ref, o_ref, lse_ref, fq_ref, fk_ref, dq_ref, dl_ref, df_ref, acc_ref, rs_ref):
        i = pl.program_id(1)
        qv = q_ref[0, 0] * QK_SCALE
        dov = do_ref[0, 0]
        fq = fq_ref[0, 0]
        lse = lse_ref[0, 0]
        delta = jnp.sum(dov.astype(F32) * o_ref[0, 0], axis=1, keepdims=True)
        dl_ref[0, 0] = delta
        acc_ref[...] = jnp.zeros_like(acc_ref)
        rs_ref[...] = jnp.zeros_like(rs_ref)

        def step(j, masked):
            kv = k_ref[0, j]
            s = lax.dot_general(qv, kv, _NT, preferred_element_type=F32) + (fq - fk_ref[0, j])
            p = jnp.exp(s - lse)
            if masked:
                p = jnp.where(_causal_mask(tb), p, 0.0)
            dp = lax.dot_general(dov, v_ref[0, j], _NT, preferred_element_type=F32)
            ds = p * (dp - delta)
            acc_ref[...] += lax.dot_general(ds.astype(BF16), kv, _NN, preferred_element_type=F32)
            rs_ref[...] += jnp.sum(ds, axis=1, keepdims=True)

        def off_diag(j, c):
            step(j, False)
            return c

        lax.fori_loop(0, i, off_diag, 0)
        step(i, True)
        dq_ref[0, 0] = (acc_ref[...] * QK_SCALE).astype(BF16)
        df_ref[0, 0] = rs_ref[...]

    return pl.pallas_call(
        body,
        name="fox_dq",
        grid=(H, nb),
        in_specs=[blk(dh), whole(dh), whole(dh), blk(dh), blk(dh), blk(1), blk(1), rows],
        out_specs=[blk(dh), blk(1), blk(1)],
        out_shape=[jax.ShapeDtypeStruct((H, nb, tb, dh), BF16), jax.ShapeDtypeStruct((H, nb, tb, 1), F32),
                   jax.ShapeDtypeStruct((H, nb, tb, 1), F32)],
        scratch_shapes=[pltpu.VMEM((tb, dh), F32), pltpu.VMEM((tb, 1), F32)],
        compiler_params=_cparams(("parallel", "arbitrary")),
    )(q, k, v, do, o, lse, fcol, frow)


def _fox_dkv(q, k, v, do, lse_row, delta_row, fcol, frow):
    H, nb, tb, dh = q.shape
    blk, whole, rows = _fox_specs(nb, tb)

    def body(q_ref, k_ref, v_ref, do_ref, lse_ref, dl_ref, fk_ref, fq_ref, dk_ref, dv_ref, df_ref, dk_acc, dv_acc, df_acc):
        j = pl.program_id(1)
        kv = k_ref[0, 0]
        vv = v_ref[0, 0]
        fk = fk_ref[0, 0]
        dk_acc[...] = jnp.zeros_like(dk_acc)
        dv_acc[...] = jnp.zeros_like(dv_acc)
        df_acc[...] = jnp.zeros_like(df_acc)

        def step(i, masked):
            qv = q_ref[0, i] * QK_SCALE
            dov = do_ref[0, i]
            st = lax.dot_general(kv, qv, _NT, preferred_element_type=F32) + (fq_ref[0, i] - fk)
            pt = jnp.exp(st - lse_ref[0, i])
            if masked:
                pt = jnp.where(_causal_mask(tb, transposed=True), pt, 0.0)
            dpt = lax.dot_general(vv, dov, _NT, preferred_element_type=F32)
            dst = pt * (dpt - dl_ref[0, i])
            dv_acc[...] += lax.dot_general(pt.astype(BF16), dov, _NN, preferred_element_type=F32)
            dk_acc[...] += lax.dot_general(dst.astype(BF16), qv, _NN, preferred_element_type=F32)
            df_acc[...] += jnp.sum(dst, axis=1, keepdims=True)

        step(j, True)

        def off_diag(i, c):
            step(i, False)
            return c

        lax.fori_loop(j + 1, nb, off_diag, 0)
        dk_ref[0, 0] = dk_acc[...].astype(BF16)
        dv_ref[0, 0] = dv_acc[...].astype(BF16)
        df_ref[0, 0] = -df_acc[...]

    return pl.pallas_call(
        body,
        name="fox_dkv",
        grid=(H, nb),
        in_specs=[whole(dh), blk(dh), blk(dh), whole(dh), rows, rows, blk(1), rows],
        out_specs=[blk(dh), blk(dh), blk(1)],
        out_shape=[jax.ShapeDtypeStruct((H, nb, tb, dh), BF16), jax.ShapeDtypeStruct((H, nb, tb, dh), BF16),
                   jax.ShapeDtypeStruct((H, nb, tb, 1), F32)],
        scratch_shapes=[pltpu.VMEM((tb, dh), F32), pltpu.VMEM((tb, dh), F32), pltpu.VMEM((tb, 1), F32)],
        compiler_params=_cparams(("parallel", "arbitrary")),
    )(q, k, v, do, lse_row, delta_row, fcol, frow)


HG_ROWS = 512
_HI = lax.Precision.HIGHEST


def _tri(n, upper):
    r = lax.broadcasted_iota(jnp.int32, (n, n), 0)
    c = lax.broadcasted_iota(jnp.int32, (n, n), 1)
    return jnp.where((r <= c) if upper else (r >= c), 1.0, 0.0).astype(F32)


def _dot(a, b, dn):
    return lax.dot_general(a.astype(BF16), b.astype(BF16), dn, preferred_element_type=F32)


def _hgrn_gates(qb, fb, lb):
    sig = jax.nn.sigmoid(fb)
    f = lb + (1.0 - lb) * sig
    sq = jax.nn.sigmoid(qb)
    return sig, f, sq, qb * sq


def _hgrn_fwd(rest, lb, col_q, col_f, col_i):
    T = rest.shape[0]
    tb = _pick(T, HG_ROWS)
    cpb = tb // CHUNK
    nblk = T // tb

    def body(q_ref, f_ref, i_ref, lb_ref, o_ref, st_ref, s_ref):
        @pl.when(pl.program_id(1) == 0)
        def _():
            s_ref[...] = jnp.zeros_like(s_ref)

        lbv = lb_ref[...]
        tril = _tri(CHUNK, False)
        mask = tril > 0.5

        def chunk(c, carry):
            rows = pl.ds(pl.multiple_of(c * CHUNK, CHUNK), CHUNK)
            _, f, _, q = _hgrn_gates(q_ref[rows, :], f_ref[rows, :], lbv)
            v = i_ref[rows, :]
            k = 1.0 - f
            b = jnp.dot(tril, jnp.log(f), precision=_HI, preferred_element_type=F32)
            bm = b[CHUNK // 2 - 1:CHUNK // 2, :]
            bl = b[CHUNK - 1:CHUNK, :]
            st = s_ref[...]
            st_ref[0, c] = st
            p = jnp.where(mask, _dot(q * jnp.exp(b - bm), k * jnp.exp(bm - b), _NT), 0.0)
            o_ref[rows, :] = _dot(p, v, _NN) + _dot(q * jnp.exp(b), st, _NT)
            s_ref[...] = st * jnp.exp(bl) + _dot(v, k * jnp.exp(bl - b), _TN)
            return carry

        lax.fori_loop(0, cpb, chunk, 0)

    col = lambda cb: pl.BlockSpec((tb, B_DK), lambda h, i, cb=cb: (i, cb + h))
    return pl.pallas_call(
        body,
        name="hgrn_fwd",
        grid=(B_HEADS, nblk),
        in_specs=[col(col_q // B_DK), col(col_f // B_DK), col(col_i // B_DK), pl.BlockSpec((1, B_DK), lambda h, i: (0, h))],
        out_specs=[pl.BlockSpec((tb, B_DK), lambda h, i: (i, h)),
                   pl.BlockSpec((1, cpb, B_DK, B_DK), lambda h, i: (h, i, 0, 0))],
        out_shape=[jax.ShapeDtypeStruct((T, B_WIDTH), F32), jax.ShapeDtypeStruct((B_HEADS, T // CHUNK, B_DK, B_DK), F32)],
        scratch_shapes=[pltpu.VMEM((B_DK, B_DK), F32)],
        compiler_params=_cparams(("parallel", "arbitrary")),
    )(rest, rest, rest, lb)


def _hgrn_bwd(rest, lb, states, do, col_q, col_f, col_i):
    T = rest.shape[0]
    tb = _pick(T, HG_ROWS)
    cpb = tb // CHUNK
    nblk = T // tb

    def body(q_ref, f_ref, i_ref, lb_ref, st_ref, do_ref, dq_ref, df_ref, di_ref, dlb_ref, ds_ref):
        @pl.when(pl.program_id(1) == 0)
        def _():
            ds_ref[...] = jnp.zeros_like(ds_ref)
            dlb_ref[...] = jnp.zeros_like(dlb_ref)

        lbv = lb_ref[...]
        tril = _tri(CHUNK, False)
        triu = _tri(CHUNK, True)
        mask = tril > 0.5

        def chunk(cc, carry):
            c = cpb - 1 - cc
            rows = pl.ds(pl.multiple_of(c * CHUNK, CHUNK), CHUNK)
            qb = q_ref[rows, :]
            sig, f, sq, q = _hgrn_gates(qb, f_ref[rows, :], lbv)
            v = i_ref[rows, :]
            dov = do_ref[rows, :]
            k = 1.0 - f
            b = jnp.dot(tril, jnp.log(f), precision=_HI, preferred_element_type=F32)
            bm = b[CHUNK // 2 - 1:CHUNK // 2, :]
            bl = b[CHUNK - 1:CHUNK, :]
            eb, eqm, ekm, ekl, ebl = jnp.exp(b), jnp.exp(b - bm), jnp.exp(bm - b), jnp.exp(bl - b), jnp.exp(bl)
            qt, kt, qa, kd = q * eqm, k * ekm, q * eb, k * ekl
            st0 = st_ref[0, c]
            dst1 = ds_ref[...]
            p = jnp.where(mask, _dot(qt, kt, _NT), 0.0)
            dp = jnp.where(mask, _dot(dov, v, _NT), 0.0)
            dv = _dot(p, dov, _TN) + _dot(kd, dst1, _NT)
            dqt = _dot(dp, kt, _NN)
            dkt = _dot(dp, qt, _TN)
            dqa = _dot(dov, st0, _NN)
            dkd = _dot(v, dst1, _NN)
            ds_ref[...] = _dot(dov, qa, _TN) + dst1 * ebl
            dq = dqt * eqm + dqa * eb
            dk = dkt * ekm + dkd * ekl
            rnd = lambda a: a.astype(BF16).astype(F32)
            db = dqt * rnd(qt) - dkt * rnd(kt) + dqa * qa - dkd * kd
            last = jnp.sum(dkd * kd, axis=0, keepdims=True) + ebl * jnp.sum(dst1 * st0, axis=0, keepdims=True)
            dg = jnp.dot(triu, db, precision=_HI, preferred_element_type=F32) + last
            dfv = dg / f - dk
            dq_ref[rows, :] = (dq * (sq * (1.0 + qb * (1.0 - sq)))).astype(BF16)
            df_ref[rows, :] = (dfv * (1.0 - lbv) * sig * (1.0 - sig)).astype(BF16)
            di_ref[rows, :] = dv.astype(BF16)
            dlb_ref[0:1, :] += jnp.sum(dfv * (1.0 - sig), axis=0, keepdims=True)
            return carry

        lax.fori_loop(0, cpb, chunk, 0)

    rev = lambda i: nblk - 1 - i
    col = lambda cb: pl.BlockSpec((tb, B_DK), lambda h, i, cb=cb: (rev(i), cb + h))
    return pl.pallas_call(
        body,
        name="hgrn_bwd",
        grid=(B_HEADS, nblk),
        in_specs=[col(col_q // B_DK), col(col_f // B_DK), col(col_i // B_DK), pl.BlockSpec((1, B_DK), lambda h, i: (0, h)),
                  pl.BlockSpec((1, cpb, B_DK, B_DK), lambda h, i: (h, rev(i), 0, 0)), col(0)],
        out_specs=[col(0), col(0), col(0), pl.BlockSpec((8, B_DK), lambda h, i: (0, h))],
        out_shape=[jax.ShapeDtypeStruct((T, B_WIDTH), BF16)] * 3 + [jax.ShapeDtypeStruct((8, B_WIDTH), F32)],
        scratch_shapes=[pltpu.VMEM((B_DK, B_DK), F32)],
        compiler_params=_cparams(("parallel", "arbitrary")),
    )(rest, rest, rest, lb, states, do)


SCAN_ROWS = 256


def _log_sigmoid(x):
    return jnp.minimum(x, 0.0) - jnp.log(1.0 + jnp.exp(-jnp.abs(x)))


def _head_cols(shape):
    return lax.broadcasted_iota(jnp.int32, shape, 1) < A_HEADS


def _fgate_fwd(rest, bf_pad):
    T = rest.shape[0]
    tr = _pick(T, SCAN_ROWS)

    def body(fa_ref, bf_ref, f_ref, carry_ref):
        @pl.when(pl.program_id(0) == 0)
        def _():
            carry_ref[...] = jnp.zeros_like(carry_ref)

        lf = jnp.where(_head_cols((tr, LANE)), _log_sigmoid(fa_ref[...] + bf_ref[...]), 0.0)
        f = jnp.dot(_tri(tr, False), lf, precision=_HI, preferred_element_type=F32) + carry_ref[...]
        f_ref[...] = f
        carry_ref[...] = f[tr - 1:tr, :]

    return pl.pallas_call(
        body,
        name="fgate_fwd",
        grid=(T // tr,),
        in_specs=[pl.BlockSpec((tr, LANE), lambda i: (i, C_FA // LANE)), pl.BlockSpec((1, LANE), lambda i: (0, 0))],
        out_specs=pl.BlockSpec((tr, LANE), lambda i: (i, 0)),
        out_shape=jax.ShapeDtypeStruct((T, LANE), F32),
        scratch_shapes=[pltpu.VMEM((1, LANE), F32)],
        compiler_params=_cparams(("arbitrary",)),
    )(rest, bf_pad)


def _fgate_bwd(dfq_pad, dfk_pad, rest, bf_pad):
    T = rest.shape[0]
    tr = _pick(T, SCAN_ROWS)
    nblk = T // tr

    def body(dq_ref, dk_ref, fa_ref, bf_ref, dfa_ref, db_ref, carry_ref):
        @pl.when(pl.program_id(0) == 0)
        def _():
            carry_ref[...] = jnp.zeros_like(carry_ref)
            db_ref[...] = jnp.zeros_like(db_ref)

        dlf = jnp.dot(_tri(tr, True), dq_ref[...] + dk_ref[...], precision=_HI, preferred_element_type=F32) + carry_ref[...]
        carry_ref[...] = dlf[0:1, :]
        dfa = jnp.where(_head_cols((tr, LANE)), dlf * jax.nn.sigmoid(-(fa_ref[...] + bf_ref[...])), 0.0)
        dfa_ref[...] = dfa.astype(BF16)
        db_ref[...] += _colsum8(dfa)

    rev = lambda i: nblk - 1 - i
    return pl.pallas_call(
        body,
        name="fgate_bwd",
        grid=(nblk,),
        in_specs=[pl.BlockSpec((tr, LANE), lambda i: (rev(i), 0)), pl.BlockSpec((tr, LANE), lambda i: (rev(i), 0)),
                  pl.BlockSpec((tr, LANE), lambda i: (rev(i), C_FA // LANE)), pl.BlockSpec((1, LANE), lambda i: (0, 0))],
        out_specs=[pl.BlockSpec((tr, LANE), lambda i: (rev(i), 0)), pl.BlockSpec((8, LANE), lambda i: (0, 0))],
        out_shape=[jax.ShapeDtypeStruct((T, LANE), BF16), jax.ShapeDtypeStruct((8, LANE), F32)],
        scratch_shapes=[pltpu.VMEM((1, LANE), F32)],
        compiler_params=_cparams(("arbitrary",)),
    )(dfq_pad, dfk_pad, rest, bf_pad)


FOX_BLOCK = 512


def _to_heads(a, tb):
    T = a.shape[0]
    return a.reshape(T // tb, tb, A_HEADS, A_HEAD_DIM).transpose(2, 0, 1, 3)


def _from_heads(a):
    H, nb, tb, d = a.shape
    return a.transpose(1, 2, 0, 3).reshape(nb * tb, H * d)


def _head_scalars(a, nb, tb):
    return a[:, :A_HEADS].T.reshape(A_HEADS, nb, tb, 1)


def _ln_stats(z):
    mu = jnp.mean(z, axis=-1, keepdims=True)
    zc = z - mu
    rstd = lax.rsqrt(jnp.mean(zc * zc, axis=-1, keepdims=True) + LN_EPS)
    return zc * rstd, rstd


def _ln_fwd(x, y, g, b, name):
    def fn(xv, yv, gv, bv):
        xhat, _ = _ln_stats(ALPHA * xv + yv)
        out = xhat * gv + bv
        return out, out
    return _rowwise(fn, name, [(x, 0, D_MODEL), (y, 0, D_MODEL)], [g, b], [(D_MODEL, F32), (D_MODEL, BF16)])


def _ln_bwd_core(xv, yv, gv, dy):
    xhat, rstd = _ln_stats(ALPHA * xv + yv)
    dxh = dy * gv
    dz = rstd * (dxh - jnp.mean(dxh, axis=-1, keepdims=True) - xhat * jnp.mean(dxh * xhat, axis=-1, keepdims=True))
    return dz, _colsum8(dy * xhat), _colsum8(dy)


def _ln_bwd(x, y, g, ua, ub, name):
    def fn(xv, yv, uav, ubv, gv):
        dz, dg, db = _ln_bwd_core(xv, yv, gv, ALPHA * uav + ubv)
        return dz, dz, dg, db
    return _rowwise(fn, name, [(x, 0, D_MODEL), (y, 0, D_MODEL), (ua, 0, D_MODEL), (ub, 0, D_MODEL)], [g],
                    [(D_MODEL, F32), (D_MODEL, BF16)], [D_MODEL, D_MODEL])


def _ln_bwd_loss(x, y, g, b, target, name):
    def fn(xv, yv, tv, gv, bv):
        xhat, _ = _ln_stats(ALPHA * xv + yv)
        err = xhat * gv + bv - tv
        dz, dg, db = _ln_bwd_core(xv, yv, gv, err * (1.0 / D_MODEL))
        return dz, dz, dg, db, _colsum8(err * err)
    return _rowwise(fn, name, [(x, 0, D_MODEL), (y, 0, D_MODEL), (target, 0, D_MODEL)], [g, b],
                    [(D_MODEL, F32), (D_MODEL, BF16)], [D_MODEL, D_MODEL, D_MODEL])


def _silu_parts(g):
    s = jax.nn.sigmoid(g)
    return g * s, s * (1.0 + g * (1.0 - s))


def _layer_fwd(x, xb, w, bf_pad, lb, norm_g, ln1_g, ln1_b, ln2_g, ln2_b, l):
    T = x.shape[0]
    tb = _pick(T, FOX_BLOCK)
    nb = T // tb
    sv = {}
    qkv = _matmul(xb, w["in"], "nn", BF16, f"l{l}_qkv", n_off=0, n_size=QKV_W, tn=768)
    rest = _matmul(xb, w["in"], "nn", F32, f"l{l}_rest", n_off=QKV_W, n_size=REST_W, tn=384)
    fcum = _fgate_fwd(rest, bf_pad)
    fcol = _head_scalars(fcum, nb, tb)
    frow = fcol.reshape(A_HEADS, nb, 1, tb)
    qh, kh, vh = (_to_heads(qkv[:, i * A_WIDTH:(i + 1) * A_WIDTH], tb) for i in range(3))
    o_a, lse = _fox_fwd(qh, kh, vh, fcol, frow)
    ya = _from_heads(o_a).astype(BF16)
    o_b, states = _hgrn_fwd(rest, lb, C_QB, C_FB, C_IB)

    def yb_fn(ov, gbv, ngv):
        outs = []
        for h in range(B_HEADS):
            oh = ov[:, h * B_DK:(h + 1) * B_DK]
            r = lax.rsqrt(jnp.mean(oh * oh, axis=-1, keepdims=True) + RMS_EPS)
            outs.append(oh * r * ngv)
        return jnp.concatenate(outs, axis=1) * jax.nn.sigmoid(gbv)
    (yb,) = _rowwise(yb_fn, f"l{l}_yb", [(o_b, 0, B_WIDTH), (rest, C_GB, B_WIDTH)], [norm_g], [(B_WIDTH, BF16)])
    pa = _matmul(ya, w["pa"], "nn", F32, f"l{l}_pa")
    pb = _matmul(yb, w["pb"], "nn", F32, f"l{l}_pb")

    def merge_fn(gav, gbv, pav, pbv):
        return jax.nn.sigmoid(gav) * pav + jax.nn.sigmoid(gbv) * pbv
    (merged,) = _rowwise(merge_fn, f"l{l}_merge", [(rest, C_GA, D_MODEL), (rest, C_GB2, D_MODEL), (pa, 0, D_MODEL), (pb, 0, D_MODEL)],
                         [], [(D_MODEL, BF16)])
    mix = _matmul(merged, w["o"], "nn", F32, f"l{l}_mix")
    x1, x1b = _ln_fwd(x, mix, ln1_g, ln1_b, f"l{l}_ln1")
    hid = _matmul(x1b, w["ffin"], "nn", F32, f"l{l}_ffin", tn=512)

    def act_fn(uv, gv):
        return _silu_parts(gv)[0] * uv
    (act,) = _rowwise(act_fn, f"l{l}_act", [(hid, 0, FFN_HIDDEN), (hid, FFN_HIDDEN, FFN_HIDDEN)], [], [(FFN_HIDDEN, BF16)], tr=128)
    ffn = _matmul(act, w["ffout"], "nn", F32, f"l{l}_ffout", tk=1408)
    x2, x2b = _ln_fwd(x1, ffn, ln2_g, ln2_b, f"l{l}_ln2")
    sv.update(x=x, xb=xb, rest=rest, fcol=fcol, frow=frow, qh=qh, kh=kh, vh=vh, o_a=o_a, lse=lse, ya=ya, o_b=o_b, states=states,
              yb=yb, pa=pa, pb=pb, merged=merged, mix=mix, x1=x1, x1b=x1b, hid=hid, act=act, ffn=ffn)
    return x2, x2b, sv


def _layer_bwd(sv, w, dz2, dz2b, dln2, bf_pad, lb, norm_g, ln1_g, l):
    T = sv["x"].shape[0]
    tb = _pick(T, FOX_BLOCK)
    nb = T // tb
    gr = {}
    dact = _matmul(dz2b, w["ffout"], "nt", F32, f"l{l}_dact", tn=1408)
    gr["ffout"] = _matmul(sv["act"], dz2b, "tn", F32, f"l{l}_dwffout", tm=1408, tn=512, tk=512)

    def dhid_fn(dav, uv, gv):
        sg, dsg = _silu_parts(gv)
        return jnp.concatenate([dav * sg, dav * uv * dsg], axis=1)
    (dhid,) = _rowwise(dhid_fn, f"l{l}_dhid", [(dact, 0, FFN_HIDDEN), (sv["hid"], 0, FFN_HIDDEN), (sv["hid"], FFN_HIDDEN, FFN_HIDDEN)],
                       [], [(2 * FFN_HIDDEN, BF16)], tr=128)
    dx1f = _matmul(dhid, w["ffin"], "nt", F32, f"l{l}_dx1f", tk=512)
    gr["ffin"] = _matmul(sv["x1b"], dhid, "tn", F32, f"l{l}_dwffin", tn=512)
    dz1, dz1b, dg1, db1 = _ln_bwd(sv["x"], sv["mix"], ln1_g, dz2, dx1f, f"l{l}_dln1")
    dmerged = _matmul(dz1b, w["o"], "nt", F32, f"l{l}_dmerged")
    gr["o"] = _matmul(sv["merged"], dz1b, "tn", F32, f"l{l}_dwo")

    def dmerge_fn(dmv, gav, gbv, pav, pbv):
        sa, sb = jax.nn.sigmoid(gav), jax.nn.sigmoid(gbv)
        return dmv * sa, dmv * sb, dmv * pav * sa * (1.0 - sa), dmv * pbv * sb * (1.0 - sb)
    dpa, dpb, dga, dgb2 = _rowwise(
        dmerge_fn, f"l{l}_dmerge",
        [(dmerged, 0, D_MODEL), (sv["rest"], C_GA, D_MODEL), (sv["rest"], C_GB2, D_MODEL), (sv["pa"], 0, D_MODEL), (sv["pb"], 0, D_MODEL)],
        [], [(D_MODEL, BF16)] * 4)
    dya = _matmul(dpa, w["pa"], "nt", BF16, f"l{l}_dya")
    gr["pa"] = _matmul(sv["ya"], dpa, "tn", F32, f"l{l}_dwpa")
    dyb = _matmul(dpb, w["pb"], "nt", F32, f"l{l}_dyb")
    gr["pb"] = _matmul(sv["yb"], dpb, "tn", F32, f"l{l}_dwpb")

    def dyb_fn(dyv, ov, gbv, ngv):
        sg = jax.nn.sigmoid(gbv)
        dos, dgs = [], []
        dng = jnp.zeros((8, B_DK), F32)
        for h in range(B_HEADS):
            sl = slice(h * B_DK, (h + 1) * B_DK)
            oh, dyh, sgh = ov[:, sl], dyv[:, sl], sg[:, sl]
            r = lax.rsqrt(jnp.mean(oh * oh, axis=-1, keepdims=True) + RMS_EPS)
            n = oh * r
            dyn = dyh * sgh
            dgs.append(dyh * n * ngv * sgh * (1.0 - sgh))
            dng = dng + _colsum8(dyn * n)
            dn = dyn * ngv
            dos.append(r * (dn - n * jnp.mean(dn * n, axis=-1, keepdims=True)))
        return jnp.concatenate(dos, axis=1), jnp.concatenate(dgs, axis=1), dng
    do_b, dgb, dng = _rowwise(dyb_fn, f"l{l}_dyb2", [(dyb, 0, B_WIDTH), (sv["o_b"], 0, B_WIDTH), (sv["rest"], C_GB, B_WIDTH)], [norm_g],
                              [(B_WIDTH, F32), (B_WIDTH, BF16)], [B_DK])
    dqb, dfb, dib, dlb = _hgrn_bwd(sv["rest"], lb, sv["states"], do_b, C_QB, C_FB, C_IB)
    doh = _to_heads(dya, tb)
    dqh, delta, dfq = _fox_dq(sv["qh"], sv["kh"], sv["vh"], doh, sv["o_a"], sv["lse"], sv["fcol"], sv["frow"])
    row = lambda a: a.reshape(A_HEADS, nb, 1, tb)
    dkh, dvh, dfk = _fox_dkv(sv["qh"], sv["kh"], sv["vh"], doh, row(sv["lse"]), row(delta), sv["fcol"], sv["frow"])
    pad = lambda a: jnp.pad(a.reshape(A_HEADS, T).T, ((0, 0), (0, LANE - A_HEADS)))
    dfa, dbf = _fgate_bwd(pad(dfq), pad(dfk), sv["rest"], bf_pad)
    dproj = jnp.concatenate([_from_heads(dqh), _from_heads(dkh), _from_heads(dvh), dga, dgb2, dqb, dfb, dib, dgb, dfa], axis=1)
    dxm = _matmul(dproj, w["in"], "nt", F32, f"l{l}_dxm", tk=640)
    gr["in"] = _matmul(sv["xb"], dproj, "tn", F32, f"l{l}_dwin", tn=640)
    small = dict(dln2=dln2, dln1=(dg1, db1), dng=dng, dlb=dlb, dbf=dbf)
    return dz1, dxm, gr, small


MESH = pl.DeviceIdType.MESH
ANY = pl.BlockSpec(memory_space=pl.ANY)


def _me_and_peers():
    x, y, c = lax.axis_index("x"), lax.axis_index("y"), lax.axis_index("c")
    me = 4 * x + 2 * y + c
    peers = []
    for p in range(1, N_DEV):
        px, py, pc = x ^ ((p >> 2) & 1), y ^ ((p >> 1) & 1), c ^ (p & 1)
        peers.append(((px, py, pc), 4 * px + 2 * py + pc))
    return me, peers


def _all_gather(shard, name, vmem=False):
    R, W = shard.shape

    def body(src_ref, out_ref, send_sems, recv_sems, local_sem):
        me, peers = _me_and_peers()
        mine = pltpu.make_async_copy(src_ref, out_ref.at[me], local_sem)
        mine.start()
        copies = []
        for p, (dev, _) in enumerate(peers):
            cp = pltpu.make_async_remote_copy(src_ref=src_ref, dst_ref=out_ref.at[me], send_sem=send_sems.at[p],
                                              recv_sem=recv_sems.at[p], device_id=dev, device_id_type=MESH)
            cp.start()
            copies.append(cp)
        for p, (dev, idx) in enumerate(peers):
            pltpu.make_async_remote_copy(src_ref=src_ref, dst_ref=out_ref.at[idx], send_sem=send_sems.at[p],
                                         recv_sem=recv_sems.at[p], device_id=dev, device_id_type=MESH).wait_recv()
        for cp in copies:
            cp.wait_send()
        mine.wait()

    spec = pl.BlockSpec(memory_space=pltpu.VMEM) if vmem else ANY
    return pl.pallas_call(
        body,
        name=name,
        in_specs=[spec],
        out_specs=spec,
        out_shape=jax.ShapeDtypeStruct((N_DEV, R, W), shard.dtype),
        scratch_shapes=[pltpu.SemaphoreType.DMA((N_DEV - 1,)), pltpu.SemaphoreType.DMA((N_DEV - 1,)), pltpu.SemaphoreType.DMA],
    )(shard)


def _exchange(chunks, name):
    _, R, W = chunks.shape

    def body(src_ref, out_ref, send_sems, recv_sems, local_sem):
        me, peers = _me_and_peers()
        mine = pltpu.make_async_copy(src_ref.at[me], out_ref.at[me], local_sem)
        mine.start()
        copies = []
        for p, (dev, idx) in enumerate(peers):
            cp = pltpu.make_async_remote_copy(src_ref=src_ref.at[idx], dst_ref=out_ref.at[me], send_sem=send_sems.at[p],
                                              recv_sem=recv_sems.at[p], device_id=dev, device_id_type=MESH)
            cp.start()
            copies.append(cp)
        for p, (dev, idx) in enumerate(peers):
            pltpu.make_async_remote_copy(src_ref=src_ref.at[idx], dst_ref=out_ref.at[idx], send_sem=send_sems.at[p],
                                         recv_sem=recv_sems.at[p], device_id=dev, device_id_type=MESH).wait_recv()
        for cp in copies:
            cp.wait_send()
        mine.wait()

    return pl.pallas_call(
        body,
        name=name,
        in_specs=[ANY],
        out_specs=ANY,
        out_shape=jax.ShapeDtypeStruct(chunks.shape, chunks.dtype),
        scratch_shapes=[pltpu.SemaphoreType.DMA((N_DEV - 1,)), pltpu.SemaphoreType.DMA((N_DEV - 1,)), pltpu.SemaphoreType.DMA],
    )(chunks)


def _sum_adam(parts, w, m, v, name, tr=512):
    _, R, W = parts.shape
    tr = _pick(R, tr) if R % LANE == 0 else R
    c1 = 1.0 / (1.0 - ADAM_B1 ** ADAM_STEP)
    c2 = 1.0 / (1.0 - ADAM_B2 ** ADAM_STEP)

    def body(p_ref, w_ref, m_ref, v_ref, g_ref, d_ref, nm_ref, nv_ref):
        g = p_ref[0]
        for s in range(1, N_DEV):
            g = g + p_ref[s]
        nm = ADAM_B1 * m_ref[...] + (1.0 - ADAM_B1) * g
        nv = ADAM_B2 * v_ref[...] + (1.0 - ADAM_B2) * (g * g)
        g_ref[...] = g
        nm_ref[...] = nm
        nv_ref[...] = nv
        d_ref[...] = -ADAM_LR * ((nm * c1) / (jnp.sqrt(nv * c2) + ADAM_EPS) + ADAM_WD * w_ref[...])

    blk = pl.BlockSpec((tr, W), lambda i: (i, 0))
    return pl.pallas_call(
        body,
        name=name,
        grid=(R // tr,),
        in_specs=[pl.BlockSpec((N_DEV, tr, W), lambda i: (0, i, 0)), blk, blk, blk],
        out_specs=[blk] * 4,
        out_shape=[jax.ShapeDtypeStruct((R, W), F32)] * 4,
        compiler_params=_cparams(("parallel",)),
    )(parts, w, m, v)


PACK_ROWS = 32768
BIG = ("in", "pa", "pb", "o", "ffin", "ffout")
COL_SHARDED = {"in": True, "pa": True, "pb": True, "o": False, "ffin": True, "ffout": False}
SMALL_ROWS = 80


def _pack(parts, rows, lead=()):
    n = len(lead)
    flat = jnp.concatenate([p.reshape(lead + (-1,)) for p in parts], axis=n)
    pad = rows * LANE - flat.shape[n]
    flat = jnp.pad(flat, ((0, 0),) * n + ((0, pad),))
    return flat.reshape(lead + (rows, LANE))


def _unpack(packed, shapes, lead=()):
    flat = packed.reshape(lead + (-1,))
    out, off = [], 0
    for s in shapes:
        n = math.prod(s)
        out.append(flat[..., off:off + n].reshape(lead + tuple(s)))
        off += n
    return out


def _join_shards(g, col_sharded):
    if col_sharded:
        return g.transpose(1, 2, 0, 3).reshape(g.shape[1], g.shape[2], N_DEV * g.shape[3])
    return g.transpose(1, 0, 2, 3).reshape(g.shape[1], N_DEV * g.shape[2], g.shape[3])


def _split_shards(full, col_sharded):
    d, a, b = full.shape
    if col_sharded:
        return full.reshape(d, a, N_DEV, b // N_DEV).transpose(2, 0, 1, 3)
    return full.reshape(d, N_DEV, a // N_DEV, b).transpose(1, 0, 2, 3)


def _permute_in(w):
    a, b = QKV_W + A_HEADS, QKV_W + A_HEADS + 4 * B_WIDTH
    return jnp.concatenate([w[:, :QKV_W], w[:, b:], w[:, a:b], w[:, QKV_W:a],
                            jnp.zeros((w.shape[0], LANE - A_HEADS), w.dtype)], axis=1)


def _unpermute_in(d):
    g0, h0, f0 = QKV_W, QKV_W + 2 * D_MODEL, QKV_W + C_FA
    return jnp.concatenate([d[:, :QKV_W], d[:, f0:f0 + A_HEADS], d[:, h0:f0], d[:, g0:h0]], axis=1)


def _lower_bounds(logits):
    sm = jax.nn.softmax(logits.astype(F32), axis=0)
    return jnp.cumsum(sm, axis=0) - sm[0:1]


def kernel(x, w_in, b_fgate, hgrn_lb_logits, hgrn_norm_g, w_branch_a, w_branch_b, w_out, ln1_g, ln1_b, w_ff_in, w_ff_out, ln2_g, ln2_b, loss_target, m_w_in, m_b_fgate, m_hgrn_lb_logits, m_hgrn_norm_g, m_w_branch_a, m_w_branch_b, m_w_out, m_ln1_g, m_ln1_b, m_w_ff_in, m_w_ff_out, m_ln2_g, m_ln2_b, v_w_in, v_b_fgate, v_hgrn_lb_logits, v_hgrn_norm_g, v_w_branch_a, v_w_branch_b, v_w_out, v_ln1_g, v_ln1_b, v_w_ff_in, v_w_ff_out, v_ln2_g, v_ln2_b):
    big = {"in": (w_in, m_w_in, v_w_in), "pa": (w_branch_a, m_w_branch_a, v_w_branch_a), "pb": (w_branch_b, m_w_branch_b, v_w_branch_b),
           "o": (w_out, m_w_out, v_w_out), "ffin": (w_ff_in, m_w_ff_in, v_w_ff_in), "ffout": (w_ff_out, m_w_ff_out, v_w_ff_out)}
    small = [(b_fgate, m_b_fgate, v_b_fgate), (hgrn_lb_logits, m_hgrn_lb_logits, v_hgrn_lb_logits), (hgrn_norm_g, m_hgrn_norm_g, v_hgrn_norm_g),
             (ln1_g, m_ln1_g, v_ln1_g), (ln1_b, m_ln1_b, v_ln1_b), (ln2_g, m_ln2_g, v_ln2_g), (ln2_b, m_ln2_b, v_ln2_b)]
    shard_shapes = [big[n][0].shape for n in BIG]
    small_shapes = [s[0].shape for s in small]

    gathered = _all_gather(_pack([big[n][0].astype(BF16) for n in BIG], PACK_ROWS), "gather_weights")
    full = {}
    for n, g in zip(BIG, _unpack(gathered, shard_shapes, lead=(N_DEV,))):
        full[n] = _join_shards(g, COL_SHARDED[n])
    lbounds = _lower_bounds(hgrn_lb_logits)

    def layer_weights(l):
        w = {n: full[n][l] for n in BIG}
        w["in"] = _permute_in(w["in"])
        return w

    def layer_params(l):
        bf_pad = jnp.pad(b_fgate[l].astype(F32), (0, LANE - A_HEADS)).reshape(1, LANE)
        row = lambda a: a[l].astype(F32).reshape(1, -1)
        return dict(bf_pad=bf_pad, lb=lbounds[l].reshape(1, -1), norm_g=row(hgrn_norm_g), ln1_g=row(ln1_g), ln1_b=row(ln1_b),
                    ln2_g=row(ln2_g), ln2_b=row(ln2_b))

    xl = x[0]
    xlb = xl.astype(BF16)
    saved = []
    for l in range(DEPTH):
        p = layer_params(l)
        xl, xlb, sv = _layer_fwd(xl, xlb, layer_weights(l), p["bf_pad"], p["lb"], p["norm_g"], p["ln1_g"], p["ln1_b"],
                                 p["ln2_g"], p["ln2_b"], l)
        saved.append(sv)

    grads = [None] * DEPTH
    smalls = [None] * DEPTH
    loss_part = None
    ua = ub = None
    for l in reversed(range(DEPTH)):
        p, sv = layer_params(l), saved[l]
        if l == DEPTH - 1:
            dz2, dz2b, dg2, db2, loss_part = _ln_bwd_loss(sv["x1"], sv["ffn"], p["ln2_g"], p["ln2_b"], loss_target[0], f"l{l}_dln2")
        else:
            dz2, dz2b, dg2, db2 = _ln_bwd(sv["x1"], sv["ffn"], p["ln2_g"], ua, ub, f"l{l}_dln2")
        ua, ub, grads[l], smalls[l] = _layer_bwd(sv, layer_weights(l), dz2, dz2b, (dg2, db2), p["bf_pad"], p["lb"], p["norm_g"],
                                                 p["ln1_g"], l)
    (grad_x,) = _rowwise(lambda a, b: ALPHA * a + b, "grad_x", [(ua, 0, D_MODEL), (ub, 0, D_MODEL)], [], [(D_MODEL, F32)])

    contrib = []
    for n in BIG:
        per_layer = [_unpermute_in(grads[l][n]) if n == "in" else grads[l][n] for l in range(DEPTH)]
        contrib.append(_split_shards(jnp.stack(per_layer), COL_SHARDED[n]))
    received = _exchange(_pack(contrib, PACK_ROWS, lead=(N_DEV,)), "exchange_grads")
    wmv = [_pack([big[n][i].astype(F32) for n in BIG], PACK_ROWS) for i in range(3)]
    outs_big = [_unpack(o, shard_shapes) for o in _sum_adam(received, *wmv, "adam_big")]

    fold = lambda a: jnp.sum(a, axis=0)
    dlb = jnp.stack([fold(smalls[l]["dlb"]) for l in range(DEPTH)])
    _, lb_vjp = jax.vjp(_lower_bounds, hgrn_lb_logits)
    small_grads = [jnp.stack([fold(smalls[l]["dbf"])[:A_HEADS] for l in range(DEPTH)]), lb_vjp(dlb)[0],
                   jnp.stack([fold(smalls[l]["dng"]) for l in range(DEPTH)]),
                   jnp.stack([fold(smalls[l]["dln1"][0]) for l in range(DEPTH)]), jnp.stack([fold(smalls[l]["dln1"][1]) for l in range(DEPTH)]),
                   jnp.stack([fold(smalls[l]["dln2"][0]) for l in range(DEPTH)]), jnp.stack([fold(smalls[l]["dln2"][1]) for l in range(DEPTH)])]
    loss_local = (0.5 / D_MODEL) * jnp.sum(loss_part)
    small_all = _all_gather(_pack(small_grads + [loss_local.reshape(1)], SMALL_ROWS), "gather_small", vmem=True)
    one = jnp.zeros((1,), F32)
    swmv = [_pack([s[i].astype(F32) for s in small] + [one], SMALL_ROWS) for i in range(3)]
    outs_small = [_unpack(o, small_shapes + [(1,)]) for o in _sum_adam(small_all, *swmv, "adam_small")]
    loss = outs_small[0][-1].reshape(())

    order = {"in": 0, "pa": 4, "pb": 5, "o": 6, "ffin": 9, "ffout": 10}
    small_pos = [1, 2, 3, 7, 8, 11, 12]
    result = [loss, grad_x.reshape(x.shape)]
    for kind in range(4):
        slot = [None] * 13
        for n, arr in zip(BIG, outs_big[kind]):
            slot[order[n]] = arr
        for pos, arr in zip(small_pos, outs_small[kind][:-1]):
            slot[pos] = arr
        result.extend(slot)
    return tuple(result)
```

```python
import functools
import math

import jax
import jax.numpy as jnp
from jax import lax
from jax.experimental import pallas as pl
from jax.experimental.pallas import tpu as pltpu

F32 = jnp.float32
BF16 = jnp.bfloat16

D_MODEL = 1024
DEPTH = 2
CHUNK = 64
A_HEADS = 8
A_HEAD_DIM = 64
A_WIDTH = 512
B_WIDTH = 512
B_HEADS = 4
B_DK = 128
FFN_HIDDEN = 2816
IN_TOTAL = 5640
ALPHA = (2 * DEPTH) ** 0.25
LN_EPS = 1e-5
RMS_EPS = 1e-6
ADAM_LR = 0.001
ADAM_B1 = 0.9
ADAM_B2 = 0.999
ADAM_EPS = 1e-08
ADAM_WD = 0.01
ADAM_STEP = 10

N_DEV = 8
LANE = 128
VMEM_LIMIT = 48 * 1024 * 1024

QKV_W = 3 * A_WIDTH
C_GA, C_GB2 = 0, D_MODEL
C_QB = 2 * D_MODEL
C_FB, C_IB, C_GB = C_QB + B_WIDTH, C_QB + 2 * B_WIDTH, C_QB + 3 * B_WIDTH
C_FA = C_QB + 4 * B_WIDTH
REST_W = C_FA + LANE
IN_PERM = QKV_W + REST_W


def _cparams(sem):
    return pltpu.CompilerParams(dimension_semantics=sem, vmem_limit_bytes=VMEM_LIMIT)


def _pick(n, pref):
    if n <= pref:
        return n
    t = pref
    while n % t:
        t -= LANE
    return t


def _matmul(a, b, mode, out_dtype, name, *, n_off=0, n_size=None, tm=2048, tn=1024, tk=2816):
    if mode == "nn":
        M, K = a.shape
        N = b.shape[1] if n_size is None else n_size
    elif mode == "nt":
        M, K = a.shape
        N = b.shape[0]
    else:
        K, M = a.shape
        N = b.shape[1]
    tm, tn, tk = _pick(M, tm), _pick(N, tn), _pick(K, tk)
    assert n_off % tn == 0
    noff = n_off // tn
    nk = K // tk

    dn = {"nn": (((1,), (0,)), ((), ())), "nt": (((1,), (1,)), ((), ())), "tn": (((0,), (0,)), ((), ()))}[mode]

    def body(a_ref, b_ref, o_ref, *scratch):
        prod = lax.dot_general(a_ref[...], b_ref[...], dn, preferred_element_type=F32)
        if nk == 1:
            o_ref[...] = prod.astype(o_ref.dtype)
            return
        (acc_ref,) = scratch
        k = pl.program_id(2)

        @pl.when(k == 0)
        def _():
            acc_ref[...] = prod

        @pl.when(k > 0)
        def _():
            acc_ref[...] += prod

        @pl.when(k == nk - 1)
        def _():
            o_ref[...] = acc_ref[...].astype(o_ref.dtype)

    if mode == "nn":
        a_spec = pl.BlockSpec((tm, tk), lambda i, j, k: (i, k))
        b_spec = pl.BlockSpec((tk, tn), lambda i, j, k: (k, j + noff))
    elif mode == "nt":
        a_spec = pl.BlockSpec((tm, tk), lambda i, j, k: (i, k))
        b_spec = pl.BlockSpec((tn, tk), lambda i, j, k: (j, k))
    else:
        a_spec = pl.BlockSpec((tk, tm), lambda i, j, k: (k, i))
        b_spec = pl.BlockSpec((tk, tn), lambda i, j, k: (k, j))
    return pl.pallas_call(
        body,
        name=name,
        grid=(M // tm, N // tn, nk),
        in_specs=[a_spec, b_spec],
        out_specs=pl.BlockSpec((tm, tn), lambda i, j, k: (i, j)),
        out_shape=jax.ShapeDtypeStruct((M, N), out_dtype),
        scratch_shapes=[pltpu.VMEM((tm, tn), F32)] if nk > 1 else [],
        compiler_params=_cparams(("parallel", "parallel", "arbitrary")),
    )(a, b)


def _rowwise(fn, name, tiled, params, outs, sums=(), *, tr=256):
    T = tiled[0][0].shape[0]
    tr = _pick(T, tr)
    nt, npar, no, ns = len(tiled), len(params), len(outs), len(sums)

    def body(*refs):
        ins = [r[...] for r in refs[:nt + npar]]
        res = fn(*ins)
        if not isinstance(res, (tuple, list)):
            res = (res,)
        o_refs = refs[nt + npar:nt + npar + no]
        s_refs = refs[nt + npar + no:]
        for r, v in zip(o_refs, res[:no]):
            r[...] = v.astype(r.dtype)
        if ns:
            @pl.when(pl.program_id(0) == 0)
            def _():
                for r in s_refs:
                    r[...] = jnp.zeros_like(r)
            for r, v in zip(s_refs, res[no:]):
                r[...] += v

    in_specs = []
    args = []
    for arr, off, w in tiled:
        assert off % w == 0 or w == arr.shape[1]
        cb = off // w
        in_specs.append(pl.BlockSpec((tr, w), lambda i, cb=cb: (i, cb)))
        args.append(arr)
    for p in params:
        in_specs.append(pl.BlockSpec(p.shape, lambda i, nd=p.ndim: (0,) * nd))
        args.append(p)
    out_specs = [pl.BlockSpec((tr, w), lambda i: (i, 0)) for w, _ in outs]
    out_shape = [jax.ShapeDtypeStruct((T, w), dt) for w, dt in outs]
    out_specs += [pl.BlockSpec((8, w), lambda i: (0, 0)) for w in sums]
    out_shape += [jax.ShapeDtypeStruct((8, w), F32) for w in sums]
    res = pl.pallas_call(
        body,
        name=name,
        grid=(T // tr,),
        in_specs=in_specs,
        out_specs=out_specs,
        out_shape=out_shape,
        compiler_params=_cparams(("arbitrary",)),
    )(*args)
    return res


def _colsum8(v):
    r, w = v.shape
    return jnp.sum(v.reshape(r // 8, 8, w), axis=0)


NEG_BIG = -1e30
QK_SCALE = A_HEAD_DIM ** -0.5
PAIRS = A_HEADS // 2
QKV_BLOCKS = A_WIDTH // LANE
FOX_TQ = 1024
FOX_TK = 1024
_NT = (((1,), (1,)), ((), ()))
_NN = (((1,), (0,)), ((), ()))
_TN = (((0,), (0,)), ((), ()))


def _first_head(shape):
    return lax.broadcasted_iota(jnp.int32, shape, 1) < A_HEAD_DIM


def _split_heads(a, first):
    return jnp.where(first, a, 0), jnp.where(first, 0, a)


def _visible(i, j, tq, tk, keys_on_rows=False):
    shape = (tk, tq) if keys_on_rows else (tq, tk)
    r = lax.broadcasted_iota(jnp.int32, shape, 0)
    c = lax.broadcasted_iota(jnp.int32, shape, 1)
    return ((c - r) if keys_on_rows else (r - c)) >= j * tk - i * tq


def _lanes01(a0, a1, rows):
    lane = lax.broadcasted_iota(jnp.int32, (rows, LANE), 1)
    return jnp.where(lane == 0, a0, jnp.where(lane == 1, a1, 0.0))


def _fox_fwd(qkv, frow, tq, tk):
    T = qkv.shape[0]
    nbk = T // tk

    def body(q_ref, k_ref, v_ref, fk_ref, o_ref, ob_ref, lse_ref, m_ref, l_ref, acc_ref):
        i = pl.program_id(1)
        first = _first_head((tq, LANE))
        qs = _split_heads(q_ref[...] * QK_SCALE, first)
        m_ref[...] = jnp.full_like(m_ref, NEG_BIG)
        l_ref[...] = jnp.zeros_like(l_ref)
        acc_ref[...] = jnp.zeros_like(acc_ref)

        def step(j, diag):
            rows = pl.ds(pl.multiple_of(j * tk, tk), tk)
            k2, v2 = k_ref[rows, :], v_ref[rows, :]
            for hh in range(2):
                s = lax.dot_general(qs[hh], k2, _NT, preferred_element_type=F32) - fk_ref[hh, j]
                if diag:
                    s = jnp.where(_visible(i, j, tq, tk), s, NEG_BIG)
                m_old = m_ref[hh]
                m_new = jnp.maximum(m_old, jnp.max(s, axis=1, keepdims=True))
                p = jnp.exp(s - m_new[:, :1])
                a = jnp.exp(m_old - m_new)
                l_ref[hh] = a * l_ref[hh] + jnp.sum(p, axis=1, keepdims=True)
                acc_ref[hh] = a * acc_ref[hh] + lax.dot_general(p.astype(BF16), v2, _NN, preferred_element_type=F32)
                m_ref[hh] = m_new

        def off_diag(j, c):
            step(j, False)
            return c

        n_full = lax.shift_right_logical(i * tq, tk.bit_length() - 1)
        lax.fori_loop(0, n_full, off_diag, 0)
        for d in range(max(1, tq // tk)):
            step(n_full + d, True)
        o = jnp.where(first, acc_ref[0] / l_ref[0], acc_ref[1] / l_ref[1])
        o_ref[...] = o
        ob_ref[...] = o.astype(BF16)
        lse_ref[0] = _lanes01((m_ref[0] + jnp.log(l_ref[0]))[:, :1], (m_ref[1] + jnp.log(l_ref[1]))[:, :1], tq)

    col_blk = pl.BlockSpec((tq, LANE), lambda hp, i: (i, hp))
    return pl.pallas_call(
        body,
        name="fox_fwd",
        grid=(PAIRS, T // tq),
        in_specs=[col_blk,
                  pl.BlockSpec((T, LANE), lambda hp, i: (0, QKV_BLOCKS + hp)),
                  pl.BlockSpec((T, LANE), lambda hp, i: (0, 2 * QKV_BLOCKS + hp)),
                  pl.BlockSpec((2, nbk, 1, tk), lambda hp, i: (hp, 0, 0, 0))],
        out_specs=[col_blk, col_blk, pl.BlockSpec((1, tq, LANE), lambda hp, i: (hp, i, 0))],
        out_shape=[jax.ShapeDtypeStruct((T, A_WIDTH), F32), jax.ShapeDtypeStruct((T, A_WIDTH), BF16),
                   jax.ShapeDtypeStruct((PAIRS, T, LANE), F32)],
        scratch_shapes=[pltpu.VMEM((2, tq, LANE), F32)] * 3,
        compiler_params=_cparams(("parallel", "arbitrary")),
    )(qkv, qkv, qkv, frow)


def _fox_dq(qkv, do, o, lse, frow, tq, tk):
    T = qkv.shape[0]
    nbk = T // tk

    def body(q_ref, k_ref, v_ref, do_ref, o_ref, lse_ref, fk_ref, dq_ref, dl_ref, df_ref, acc_ref, rs_ref):
        i = pl.program_id(1)
        first = _first_head((tq, LANE))
        qs = _split_heads(q_ref[...] * QK_SCALE, first)
        do2 = do_ref[...]
        dos = _split_heads(do2, first)
        prod = do2.astype(F32) * o_ref[...]
        deltas = (jnp.sum(jnp.where(first, prod, 0.0), axis=1, keepdims=True),
                  jnp.sum(jnp.where(first, 0.0, prod), axis=1, keepdims=True))
        lse2 = lse_ref[0]
        lses = (lse2[:, 0:1], lse2[:, 1:2])
        dl_ref[0] = _lanes01(deltas[0], deltas[1], tq)
        acc_ref[...] = jnp.zeros_like(acc_ref)
        rs_ref[...] = jnp.zeros_like(rs_ref)

        def step(j, diag):
            rows = pl.ds(pl.multiple_of(j * tk, tk), tk)
            k2, v2 = k_ref[rows, :], v_ref[rows, :]
            for hh in range(2):
                s = lax.dot_general(qs[hh], k2, _NT, preferred_element_type=F32) - fk_ref[hh, j]
                p = jnp.exp(s - lses[hh])
                if diag:
                    p = jnp.where(_visible(i, j, tq, tk), p, 0.0)
                dp = lax.dot_general(dos[hh], v2, _NT, preferred_element_type=F32)
                ds = p * (dp - deltas[hh])
                acc_ref[hh] += lax.dot_general(ds.astype(BF16), k2, _NN, preferred_element_type=F32)
                rs_ref[hh] += jnp.sum(ds, axis=1, keepdims=True)

        def off_diag(j, c):
            step(j, False)
            return c

        n_full = lax.shift_right_logical(i * tq, tk.bit_length() - 1)
        lax.fori_loop(0, n_full, off_diag, 0)
        for d in range(max(1, tq // tk)):
            step(n_full + d, True)
        dq_ref[...] = (jnp.where(first, acc_ref[0], acc_ref[1]) * QK_SCALE).astype(BF16)
        df_ref[0] = _lanes01(rs_ref[0][:, 0:1], rs_ref[1][:, 0:1], tq)

    pair_blk = pl.BlockSpec((1, tq, LANE), lambda hp, i: (hp, i, 0))
    col_blk = pl.BlockSpec((tq, LANE), lambda hp, i: (i, hp))
    return pl.pallas_call(
        body,
        name="fox_dq",
        grid=(PAIRS, T // tq),
        in_specs=[col_blk,
                  pl.BlockSpec((T, LANE), lambda hp, i: (0, QKV_BLOCKS + hp)),
                  pl.BlockSpec((T, LANE), lambda hp, i: (0, 2 * QKV_BLOCKS + hp)),
                  col_blk, col_blk, pair_blk,
                  pl.BlockSpec((2, nbk, 1, tk), lambda hp, i: (hp, 0, 0, 0))],
        out_specs=[col_blk, pair_blk, pair_blk],
        out_shape=[jax.ShapeDtypeStruct((T, A_WIDTH), BF16), jax.ShapeDtypeStruct((PAIRS, T, LANE), F32),
                   jax.ShapeDtypeStruct((PAIRS, T, LANE), F32)],
        scratch_shapes=[pltpu.VMEM((2, tq, LANE), F32)] * 2,
        compiler_params=_cparams(("parallel", "arbitrary")),
    )(qkv, qkv, qkv, do, o, lse, frow)


def _fox_dkv(qkv, do, lse_row, delta_row, fpad, tq, tk):
    T = qkv.shape[0]
    nbq = T // tq

    def body(q_ref, k_ref, v_ref, do_ref, lse_ref, dl_ref, f_ref, dk_ref, dv_ref, df_ref, dk_acc, dv_acc, df_acc):
        hp = pl.program_id(0)
        j = pl.program_id(1)
        first = _first_head((tk, LANE))
        ks = _split_heads(k_ref[...], first)
        vs = _split_heads(v_ref[...], first)
        lane = lax.broadcasted_iota(jnp.int32, (tk, LANE), 1)
        fks = tuple(jnp.sum(jnp.where(lane == 2 * hp + hh, f_ref[...], 0.0), axis=1, keepdims=True) for hh in range(2))
        dk_acc[...] = jnp.zeros_like(dk_acc)
        dv_acc[...] = jnp.zeros_like(dv_acc)
        df_acc[...] = jnp.zeros_like(df_acc)

        def step(i, diag):
            rows = pl.ds(pl.multiple_of(i * tq, tq), tq)
            q2 = q_ref[rows, :] * QK_SCALE
            do2 = do_ref[rows, :]
            for hh in range(2):
                st = lax.dot_general(ks[hh], q2, _NT, preferred_element_type=F32) - fks[hh]
                pt = jnp.exp(st - lse_ref[hh, i])
                if diag:
                    pt = jnp.where(_visible(i, j, tq, tk, keys_on_rows=True), pt, 0.0)
                dpt = lax.dot_general(vs[hh], do2, _NT, preferred_element_type=F32)
                dst = pt * (dpt - dl_ref[hh, i])
                dv_acc[hh] += lax.dot_general(pt.astype(BF16), do2, _NN, preferred_element_type=F32)
                dk_acc[hh] += lax.dot_general(dst.astype(BF16), q2, _NN, preferred_element_type=F32)
                df_acc[hh] += jnp.sum(dst, axis=1, keepdims=True)

        i0 = lax.shift_right_logical(j * tk, tq.bit_length() - 1)
        nd = max(1, tk // tq)
        for d in range(nd):
            step(i0 + d, True)

        def off_diag(i, c):
            step(i, False)
            return c

        lax.fori_loop(i0 + nd, nbq, off_diag, 0)
        dk_ref[...] = jnp.where(first, dk_acc[0], dk_acc[1]).astype(BF16)
        dv_ref[...] = jnp.where(first, dv_acc[0], dv_acc[1]).astype(BF16)
        df_ref[0] = _lanes01(-df_acc[0][:, 0:1], -df_acc[1][:, 0:1], tk)

    whole = lambda cb: pl.BlockSpec((T, LANE), lambda hp, j, cb=cb: (0, cb + hp))
    blk = lambda cb: pl.BlockSpec((tk, LANE), lambda hp, j, cb=cb: (j, cb + hp))
    rows = pl.BlockSpec((2, nbq, 1, tq), lambda hp, j: (hp, 0, 0, 0))
    return pl.pallas_call(
        body,
        name="fox_dkv",
        grid=(PAIRS, T // tk),
        in_specs=[whole(0), blk(QKV_BLOCKS), blk(2 * QKV_BLOCKS), whole(0), rows, rows, pl.BlockSpec((tk, LANE), lambda hp, j: (j, 0))],
        out_specs=[blk(0), blk(0), pl.BlockSpec((1, tk, LANE), lambda hp, j: (hp, j, 0))],
        out_shape=[jax.ShapeDtypeStruct((T, A_WIDTH), BF16), jax.ShapeDtypeStruct((T, A_WIDTH), BF16),
                   jax.ShapeDtypeStruct((PAIRS, T, LANE), F32)],
        scratch_shapes=[pltpu.VMEM((2, tk, LANE), F32)] * 3,
        compiler_params=_cparams(("parallel", "arbitrary")),
    )(qkv, qkv, qkv, do, lse_row, delta_row, fpad)


HG_ROWS = 512
_HI = lax.Precision.HIGHEST


def _tri(n, upper):
    r = lax.broadcasted_iota(jnp.int32, (n, n), 0)
    c = lax.broadcasted_iota(jnp.int32, (n, n), 1)
    return jnp.where((r <= c) if upper else (r >= c), 1.0, 0.0).astype(F32)


def _dot(a, b, dn):
    return lax.dot_general(a.astype(BF16), b.astype(BF16), dn, preferred_element_type=F32)


def _hgrn_gates(qb, fb, lb):
    sig = jax.nn.sigmoid(fb)
    f = lb + (1.0 - lb) * sig
    sq = jax.nn.sigmoid(qb)
    return sig, f, sq, qb * sq


def _hgrn_fwd(rest, lb, col_q, col_f, col_i):
    T = rest.shape[0]
    tb = _pick(T, HG_ROWS)
    cpb = tb // CHUNK
    nblk = T // tb

    def body(q_ref, f_ref, i_ref, lb_ref, o_ref, st_ref, s_ref):
        @pl.when(pl.program_id(1) == 0)
        def _():
            s_ref[...] = jnp.zeros_like(s_ref)

        lbv = lb_ref[...]
        tril = _tri(CHUNK, False)
        mask = tril > 0.5

        def chunk(c, carry):
            rows = pl.ds(pl.multiple_of(c * CHUNK, CHUNK), CHUNK)
            _, f, _, q = _hgrn_gates(q_ref[rows, :], f_ref[rows, :], lbv)
            v = i_ref[rows, :]
            k = 1.0 - f
            b = jnp.dot(tril, jnp.log(f), precision=_HI, preferred_element_type=F32)
            bm = b[CHUNK // 2 - 1:CHUNK // 2, :]
            bl = b[CHUNK - 1:CHUNK, :]
            st = s_ref[...]
            st_ref[0, c] = st
            p = jnp.where(mask, _dot(q * jnp.exp(b - bm), k * jnp.exp(bm - b), _NT), 0.0)
            o_ref[rows, :] = _dot(p, v, _NN) + _dot(q * jnp.exp(b), st, _NT)
            s_ref[...] = st * jnp.exp(bl) + _dot(v, k * jnp.exp(bl - b), _TN)
            return carry

        lax.fori_loop(0, cpb, chunk, 0)

    col = lambda cb: pl.BlockSpec((tb, B_DK), lambda h, i, cb=cb: (i, cb + h))
    return pl.pallas_call(
        body,
        name="hgrn_fwd",
        grid=(B_HEADS, nblk),
        in_specs=[col(col_q // B_DK), col(col_f // B_DK), col(col_i // B_DK), pl.BlockSpec((1, B_DK), lambda h, i: (0, h))],
        out_specs=[pl.BlockSpec((tb, B_DK), lambda h, i: (i, h)),
                   pl.BlockSpec((1, cpb, B_DK, B_DK), lambda h, i: (h, i, 0, 0))],
        out_shape=[jax.ShapeDtypeStruct((T, B_WIDTH), F32), jax.ShapeDtypeStruct((B_HEADS, T // CHUNK, B_DK, B_DK), F32)],
        scratch_shapes=[pltpu.VMEM((B_DK, B_DK), F32)],
        compiler_params=_cparams(("parallel", "arbitrary")),
    )(rest, rest, rest, lb)


def _hgrn_bwd(rest, lb, states, do, col_q, col_f, col_i):
    T = rest.shape[0]
    tb = _pick(T, HG_ROWS)
    cpb = tb // CHUNK
    nblk = T // tb

    def body(q_ref, f_ref, i_ref, lb_ref, st_ref, do_ref, dq_ref, df_ref, di_ref, dlb_ref, ds_ref):
        @pl.when(pl.program_id(1) == 0)
        def _():
            ds_ref[...] = jnp.zeros_like(ds_ref)
            dlb_ref[...] = jnp.zeros_like(dlb_ref)

        lbv = lb_ref[...]
        tril = _tri(CHUNK, False)
        triu = _tri(CHUNK, True)
        mask = tril > 0.5

        def chunk(cc, carry):
            c = cpb - 1 - cc
            rows = pl.ds(pl.multiple_of(c * CHUNK, CHUNK), CHUNK)
            qb = q_ref[rows, :]
            sig, f, sq, q = _hgrn_gates(qb, f_ref[rows, :], lbv)
            v = i_ref[rows, :]
            dov = do_ref[rows, :]
            k = 1.0 - f
            b = jnp.dot(tril, jnp.log(f), precision=_HI, preferred_element_type=F32)
            bm = b[CHUNK // 2 - 1:CHUNK // 2, :]
            bl = b[CHUNK - 1:CHUNK, :]
            eb, eqm, ekm, ekl, ebl = jnp.exp(b), jnp.exp(b - bm), jnp.exp(bm - b), jnp.exp(bl - b), jnp.exp(bl)
            qt, kt, qa, kd = q * eqm, k * ekm, q * eb, k * ekl
            st0 = st_ref[0, c]
            dst1 = ds_ref[...]
            p = jnp.where(mask, _dot(qt, kt, _NT), 0.0)
            dp = jnp.where(mask, _dot(dov, v, _NT), 0.0)
            dv = _dot(p, dov, _TN) + _dot(kd, dst1, _NT)
            dqt = _dot(dp, kt, _NN)
            dkt = _dot(dp, qt, _TN)
            dqa = _dot(dov, st0, _NN)
            dkd = _dot(v, dst1, _NN)
            ds_ref[...] = _dot(dov, qa, _TN) + dst1 * ebl
            dq = dqt * eqm + dqa * eb
            dk = dkt * ekm + dkd * ekl
            rnd = lambda a: a.astype(BF16).astype(F32)
            db = dqt * rnd(qt) - dkt * rnd(kt) + dqa * qa - dkd * kd
            last = jnp.sum(dkd * kd, axis=0, keepdims=True) + ebl * jnp.sum(dst1 * st0, axis=0, keepdims=True)
            dg = jnp.dot(triu, db, precision=_HI, preferred_element_type=F32) + last
            dfv = dg / f - dk
            dq_ref[rows, :] = (dq * (sq * (1.0 + qb * (1.0 - sq)))).astype(BF16)
            df_ref[rows, :] = (dfv * (1.0 - lbv) * sig * (1.0 - sig)).astype(BF16)
            di_ref[rows, :] = dv.astype(BF16)
            dlb_ref[0:1, :] += jnp.sum(dfv * (1.0 - sig), axis=0, keepdims=True)
            return carry

        lax.fori_loop(0, cpb, chunk, 0)

    rev = lambda i: nblk - 1 - i
    col = lambda cb: pl.BlockSpec((tb, B_DK), lambda h, i, cb=cb: (rev(i), cb + h))
    return pl.pallas_call(
        body,
        name="hgrn_bwd",
        grid=(B_HEADS, nblk),
        in_specs=[col(col_q // B_DK), col(col_f // B_DK), col(col_i // B_DK), pl.BlockSpec((1, B_DK), lambda h, i: (0, h)),
                  pl.BlockSpec((1, cpb, B_DK, B_DK), lambda h, i: (h, rev(i), 0, 0)), col(0)],
        out_specs=[col(0), col(0), col(0), pl.BlockSpec((8, B_DK), lambda h, i: (0, h))],
        out_shape=[jax.ShapeDtypeStruct((T, B_WIDTH), BF16)] * 3 + [jax.ShapeDtypeStruct((8, B_WIDTH), F32)],
        scratch_shapes=[pltpu.VMEM((B_DK, B_DK), F32)],
        compiler_params=_cparams(("parallel", "arbitrary")),
    )(rest, rest, rest, lb, states, do)


SCAN_ROWS = 256


def _log_sigmoid(x):
    return jnp.minimum(x, 0.0) - jnp.log(1.0 + jnp.exp(-jnp.abs(x)))


def _head_cols(shape):
    return lax.broadcasted_iota(jnp.int32, shape, 1) < A_HEADS


def _fgate_fwd(rest, bf_pad):
    T = rest.shape[0]
    tr = _pick(T, SCAN_ROWS)

    def body(fa_ref, bf_ref, f_ref, carry_ref):
        @pl.when(pl.program_id(0) == 0)
        def _():
            carry_ref[...] = jnp.zeros_like(carry_ref)

        lf = jnp.where(_head_cols((tr, LANE)), _log_sigmoid(fa_ref[...] + bf_ref[...]), 0.0)
        f = jnp.dot(_tri(tr, False), lf, precision=_HI, preferred_element_type=F32) + carry_ref[...]
        f_ref[...] = f
        carry_ref[...] = f[tr - 1:tr, :]

    return pl.pallas_call(
        body,
        name="fgate_fwd",
        grid=(T // tr,),
        in_specs=[pl.BlockSpec((tr, LANE), lambda i: (i, C_FA // LANE)), pl.BlockSpec((1, LANE), lambda i: (0, 0))],
        out_specs=pl.BlockSpec((tr, LANE), lambda i: (i, 0)),
        out_shape=jax.ShapeDtypeStruct((T, LANE), F32),
        scratch_shapes=[pltpu.VMEM((1, LANE), F32)],
        compiler_params=_cparams(("arbitrary",)),
    )(rest, bf_pad)


def _fgate_bwd(dfq, dfk, rest, bf_pad):
    T = rest.shape[0]
    tr = _pick(T, SCAN_ROWS)
    nblk = T // tr

    def body(dq_ref, dk_ref, fa_ref, bf_ref, dfa_ref, db_ref, carry_ref):
        @pl.when(pl.program_id(0) == 0)
        def _():
            carry_ref[...] = jnp.zeros_like(carry_ref)
            db_ref[...] = jnp.zeros_like(db_ref)

        df = dq_ref[0] + dk_ref[0]
        for hp in range(1, PAIRS):
            df = df + pltpu.roll(dq_ref[hp] + dk_ref[hp], 2 * hp, 1)
        dlf = jnp.dot(_tri(tr, True), df, precision=_HI, preferred_element_type=F32) + carry_ref[...]
        carry_ref[...] = dlf[0:1, :]
        dfa = jnp.where(_head_cols((tr, LANE)), dlf * jax.nn.sigmoid(-(fa_ref[...] + bf_ref[...])), 0.0)
        dfa_ref[...] = dfa.astype(BF16)
        db_ref[...] += _colsum8(dfa)

    rev = lambda i: nblk - 1 - i
    return pl.pallas_call(
        body,
        name="fgate_bwd",
        grid=(nblk,),
        in_specs=[pl.BlockSpec((PAIRS, tr, LANE), lambda i: (0, rev(i), 0)), pl.BlockSpec((PAIRS, tr, LANE), lambda i: (0, rev(i), 0)),
                  pl.BlockSpec((tr, LANE), lambda i: (rev(i), C_FA // LANE)), pl.BlockSpec((1, LANE), lambda i: (0, 0))],
        out_specs=[pl.BlockSpec((tr, LANE), lambda i: (rev(i), 0)), pl.BlockSpec((8, LANE), lambda i: (0, 0))],
        out_shape=[jax.ShapeDtypeStruct((T, LANE), BF16), jax.ShapeDtypeStruct((8, LANE), F32)],
        scratch_shapes=[pltpu.VMEM((1, LANE), F32)],
        compiler_params=_cparams(("arbitrary",)),
    )(dfq, dfk, rest, bf_pad)


def _head_rows(a, t):
    return a[:, :A_HEADS].T.reshape(A_HEADS, a.shape[0] // t, 1, t)


def _pair_rows(a, t):
    return a[:, :, :2].transpose(0, 2, 1).reshape(A_HEADS, a.shape[1] // t, 1, t)


def _ln_stats(z):
    mu = jnp.mean(z, axis=-1, keepdims=True)
    zc = z - mu
    rstd = lax.rsqrt(jnp.mean(zc * zc, axis=-1, keepdims=True) + LN_EPS)
    return zc * rstd, rstd


def _ln_fwd(x, y, g, b, name):
    def fn(xv, yv, gv, bv):
        xhat, _ = _ln_stats(ALPHA * xv + yv)
        out = xhat * gv + bv
        return out, out
    return _rowwise(fn, name, [(x, 0, D_MODEL), (y, 0, D_MODEL)], [g, b], [(D_MODEL, F32), (D_MODEL, BF16)])


def _ln_bwd_core(xv, yv, gv, dy):
    xhat, rstd = _ln_stats(ALPHA * xv + yv)
    dxh = dy * gv
    dz = rstd * (dxh - jnp.mean(dxh, axis=-1, keepdims=True) - xhat * jnp.mean(dxh * xhat, axis=-1, keepdims=True))
    return dz, _colsum8(dy * xhat), _colsum8(dy)


def _ln_bwd(x, y, g, ua, ub, name):
    def fn(xv, yv, uav, ubv, gv):
        dz, dg, db = _ln_bwd_core(xv, yv, gv, ALPHA * uav + ubv)
        return dz, dz, dg, db
    return _rowwise(fn, name, [(x, 0, D_MODEL), (y, 0, D_MODEL), (ua, 0, D_MODEL), (ub, 0, D_MODEL)], [g],
                    [(D_MODEL, F32), (D_MODEL, BF16)], [D_MODEL, D_MODEL])


def _ln_bwd_loss(x, y, g, b, target, name):
    def fn(xv, yv, tv, gv, bv):
        xhat, _ = _ln_stats(ALPHA * xv + yv)
        err = xhat * gv + bv - tv
        dz, dg, db = _ln_bwd_core(xv, yv, gv, err * (1.0 / D_MODEL))
        return dz, dz, dg, db, _colsum8(err * err)
    return _rowwise(fn, name, [(x, 0, D_MODEL), (y, 0, D_MODEL), (target, 0, D_MODEL)], [g, b],
                    [(D_MODEL, F32), (D_MODEL, BF16)], [D_MODEL, D_MODEL, D_MODEL])


def _silu_parts(g):
    s = jax.nn.sigmoid(g)
    return g * s, s * (1.0 + g * (1.0 - s))


def _layer_fwd(x, xb, w, bf_pad, lb, norm_g, ln1_g, ln1_b, ln2_g, ln2_b, l):
    T = x.shape[0]
    tq, tk = min(T, FOX_TQ), min(T, FOX_TK)
    sv = {}
    qkv = _matmul(xb, w["in"], "nn", BF16, f"l{l}_qkv", n_off=0, n_size=QKV_W, tn=768)
    rest = _matmul(xb, w["in"], "nn", F32, f"l{l}_rest", n_off=QKV_W, n_size=REST_W, tn=384)
    fcum = _fgate_fwd(rest, bf_pad)
    frow = _head_rows(fcum, tk)
    o_a, ya, lse = _fox_fwd(qkv, frow, tq, tk)
    o_b, states = _hgrn_fwd(rest, lb, C_QB, C_FB, C_IB)

    def yb_fn(ov, gbv, ngv):
        outs = []
        for h in range(B_HEADS):
            oh = ov[:, h * B_DK:(h + 1) * B_DK]
            r = lax.rsqrt(jnp.mean(oh * oh, axis=-1, keepdims=True) + RMS_EPS)
            outs.append(oh * r * ngv)
        return jnp.concatenate(outs, axis=1) * jax.nn.sigmoid(gbv)
    (yb,) = _rowwise(yb_fn, f"l{l}_yb", [(o_b, 0, B_WIDTH), (rest, C_GB, B_WIDTH)], [norm_g], [(B_WIDTH, BF16)])
    pa = _matmul(ya, w["pa"], "nn", F32, f"l{l}_pa")
    pb = _matmul(yb, w["pb"], "nn", F32, f"l{l}_pb")

    def merge_fn(gav, gbv, pav, pbv):
        return jax.nn.sigmoid(gav) * pav + jax.nn.sigmoid(gbv) * pbv
    (merged,) = _rowwise(merge_fn, f"l{l}_merge", [(rest, C_GA, D_MODEL), (rest, C_GB2, D_MODEL), (pa, 0, D_MODEL), (pb, 0, D_MODEL)],
                         [], [(D_MODEL, BF16)])
    mix = _matmul(merged, w["o"], "nn", F32, f"l{l}_mix")
    x1, x1b = _ln_fwd(x, mix, ln1_g, ln1_b, f"l{l}_ln1")
    hid = _matmul(x1b, w["ffin"], "nn", F32, f"l{l}_ffin", tn=512)

    def act_fn(uv, gv):
        return _silu_parts(gv)[0] * uv
    (act,) = _rowwise(act_fn, f"l{l}_act", [(hid, 0, FFN_HIDDEN), (hid, FFN_HIDDEN, FFN_HIDDEN)], [], [(FFN_HIDDEN, BF16)], tr=128)
    ffn = _matmul(act, w["ffout"], "nn", F32, f"l{l}_ffout", tm=1024)
    x2, x2b = _ln_fwd(x1, ffn, ln2_g, ln2_b, f"l{l}_ln2")
    sv.update(x=x, xb=xb, rest=rest, qkv=qkv, fcum=fcum, frow=frow, o_a=o_a, lse=lse, ya=ya, o_b=o_b, states=states,
              yb=yb, pa=pa, pb=pb, merged=merged, mix=mix, x1=x1, x1b=x1b, hid=hid, act=act, ffn=ffn)
    return x2, x2b, sv


def _layer_bwd(sv, w, dz2, dz2b, dln2, bf_pad, lb, norm_g, ln1_g, l):
    T = sv["x"].shape[0]
    tq, tk = min(T, FOX_TQ), min(T, FOX_TK)
    gr = {}
    dact = _matmul(dz2b, w["ffout"], "nt", F32, f"l{l}_dact", tm=1024, tn=1408)
    gr["ffout"] = _matmul(sv["act"], dz2b, "tn", F32, f"l{l}_dwffout", tm=1408, tn=512, tk=2048)

    def dhid_fn(dav, uv, gv):
        sg, dsg = _silu_parts(gv)
        return jnp.concatenate([dav * sg, dav * uv * dsg], axis=1)
    (dhid,) = _rowwise(dhid_fn, f"l{l}_dhid", [(dact, 0, FFN_HIDDEN), (sv["hid"], 0, FFN_HIDDEN), (sv["hid"], FFN_HIDDEN, FFN_HIDDEN)],
                       [], [(2 * FFN_HIDDEN, BF16)], tr=128)
    dx1f = _matmul(dhid, w["ffin"], "nt", F32, f"l{l}_dx1f", tm=1024)
    gr["ffin"] = _matmul(sv["x1b"], dhid, "tn", F32, f"l{l}_dwffin", tm=1024, tn=512, tk=4096)
    dz1, dz1b, dg1, db1 = _ln_bwd(sv["x"], sv["mix"], ln1_g, dz2, dx1f, f"l{l}_dln1")
    dmerged = _matmul(dz1b, w["o"], "nt", F32, f"l{l}_dmerged")
    gr["o"] = _matmul(sv["merged"], dz1b, "tn", F32, f"l{l}_dwo", tm=1024, tn=512, tk=4096)

    def dmerge_fn(dmv, gav, gbv, pav, pbv):
        sa, sb = jax.nn.sigmoid(gav), jax.nn.sigmoid(gbv)
        return dmv * sa, dmv * sb, dmv * pav * sa * (1.0 - sa), dmv * pbv * sb * (1.0 - sb)
    dpa, dpb, dga, dgb2 = _rowwise(
        dmerge_fn, f"l{l}_dmerge",
        [(dmerged, 0, D_MODEL), (sv["rest"], C_GA, D_MODEL), (sv["rest"], C_GB2, D_MODEL), (sv["pa"], 0, D_MODEL), (sv["pb"], 0, D_MODEL)],
        [], [(D_MODEL, BF16)] * 4)
    dya = _matmul(dpa, w["pa"], "nt", BF16, f"l{l}_dya", tn=512)
    gr["pa"] = _matmul(sv["ya"], dpa, "tn", F32, f"l{l}_dwpa", tm=512, tn=512, tk=4096)
    dyb = _matmul(dpb, w["pb"], "nt", F32, f"l{l}_dyb", tn=512)
    gr["pb"] = _matmul(sv["yb"], dpb, "tn", F32, f"l{l}_dwpb", tm=512, tn=512, tk=4096)

    def dyb_fn(dyv, ov, gbv, ngv):
        sg = jax.nn.sigmoid(gbv)
        dos, dgs = [], []
        dng = jnp.zeros((8, B_DK), F32)
        for h in range(B_HEADS):
            sl = slice(h * B_DK, (h + 1) * B_DK)
            oh, dyh, sgh = ov[:, sl], dyv[:, sl], sg[:, sl]
            r = lax.rsqrt(jnp.mean(oh * oh, axis=-1, keepdims=True) + RMS_EPS)
            n = oh * r
            dyn = dyh * sgh
            dgs.append(dyh * n * ngv * sgh * (1.0 - sgh))
            dng = dng + _colsum8(dyn * n)
            dn = dyn * ngv
            dos.append(r * (dn - n * jnp.mean(dn * n, axis=-1, keepdims=True)))
        return jnp.concatenate(dos, axis=1), jnp.concatenate(dgs, axis=1), dng
    do_b, dgb, dng = _rowwise(dyb_fn, f"l{l}_dyb2", [(dyb, 0, B_WIDTH), (sv["o_b"], 0, B_WIDTH), (sv["rest"], C_GB, B_WIDTH)], [norm_g],
                              [(B_WIDTH, F32), (B_WIDTH, BF16)], [B_DK])
    dqb, dfb, dib, dlb = _hgrn_bwd(sv["rest"], lb, sv["states"], do_b, C_QB, C_FB, C_IB)
    dq, delta, dfq = _fox_dq(sv["qkv"], dya, sv["o_a"], sv["lse"], sv["frow"], tq, tk)
    dk, dv, dfk = _fox_dkv(sv["qkv"], dya, _pair_rows(sv["lse"], tq), _pair_rows(delta, tq), sv["fcum"], tq, tk)
    dfa, dbf = _fgate_bwd(dfq, dfk, sv["rest"], bf_pad)
    dproj = jnp.concatenate([dq, dk, dv, dga, dgb2, dqb, dfb, dib, dgb, dfa], axis=1)
    dxm = _matmul(dproj, w["in"], "nt", F32, f"l{l}_dxm", tm=1024, tk=1920)
    gr["in"] = _matmul(sv["xb"], dproj, "tn", F32, f"l{l}_dwin", tm=1024, tn=640, tk=4096)
    small = dict(dln2=dln2, dln1=(dg1, db1), dng=dng, dlb=dlb, dbf=dbf)
    return dz1, dxm, gr, small


MESH = pl.DeviceIdType.MESH
ANY = pl.BlockSpec(memory_space=pl.ANY)


def _me_and_peers():
    x, y, c = lax.axis_index("x"), lax.axis_index("y"), lax.axis_index("c")
    me = 4 * x + 2 * y + c
    peers = []
    for p in range(1, N_DEV):
        px, py, pc = x ^ ((p >> 2) & 1), y ^ ((p >> 1) & 1), c ^ (p & 1)
        peers.append(((px, py, pc), 4 * px + 2 * py + pc))
    return me, peers


def _all_gather(shard, name, vmem=False):
    R, W = shard.shape

    def body(src_ref, out_ref, send_sems, recv_sems, local_sem):
        me, peers = _me_and_peers()
        mine = pltpu.make_async_copy(src_ref, out_ref.at[me], local_sem)
        mine.start()
        copies = []
        for p, (dev, _) in enumerate(peers):
            cp = pltpu.make_async_remote_copy(src_ref=src_ref, dst_ref=out_ref.at[me], send_sem=send_sems.at[p],
                                              recv_sem=recv_sems.at[p], device_id=dev, device_id_type=MESH)
            cp.start()
            copies.append(cp)
        for p, (dev, idx) in enumerate(peers):
            pltpu.make_async_remote_copy(src_ref=src_ref, dst_ref=out_ref.at[idx], send_sem=send_sems.at[p],
                                         recv_sem=recv_sems.at[p], device_id=dev, device_id_type=MESH).wait_recv()
        for cp in copies:
            cp.wait_send()
        mine.wait()

    spec = pl.BlockSpec(memory_space=pltpu.VMEM) if vmem else ANY
    return pl.pallas_call(
        body,
        name=name,
        in_specs=[spec],
        out_specs=spec,
        out_shape=jax.ShapeDtypeStruct((N_DEV, R, W), shard.dtype),
        scratch_shapes=[pltpu.SemaphoreType.DMA((N_DEV - 1,)), pltpu.SemaphoreType.DMA((N_DEV - 1,)), pltpu.SemaphoreType.DMA],
    )(shard)


def _exchange(chunks, name):
    _, R, W = chunks.shape

    def body(src_ref, out_ref, send_sems, recv_sems, local_sem):
        me, peers = _me_and_peers()
        mine = pltpu.make_async_copy(src_ref.at[me], out_ref.at[me], local_sem)
        mine.start()
        copies = []
        for p, (dev, idx) in enumerate(peers):
            cp = pltpu.make_async_remote_copy(src_ref=src_ref.at[idx], dst_ref=out_ref.at[me], send_sem=send_sems.at[p],
                                              recv_sem=recv_sems.at[p], device_id=dev, device_id_type=MESH)
            cp.start()
            copies.append(cp)
        for p, (dev, idx) in enumerate(peers):
            pltpu.make_async_remote_copy(src_ref=src_ref.at[idx], dst_ref=out_ref.at[idx], send_sem=send_sems.at[p],
                                         recv_sem=recv_sems.at[p], device_id=dev, device_id_type=MESH).wait_recv()
        for cp in copies:
            cp.wait_send()
        mine.wait()

    return pl.pallas_call(
        body,
        name=name,
        in_specs=[ANY],
        out_specs=ANY,
        out_shape=jax.ShapeDtypeStruct(chunks.shape, chunks.dtype),
        scratch_shapes=[pltpu.SemaphoreType.DMA((N_DEV - 1,)), pltpu.SemaphoreType.DMA((N_DEV - 1,)), pltpu.SemaphoreType.DMA],
    )(chunks)


def _sum_adam(parts, w, m, v, name, tr=512):
    _, R, W = parts.shape
    tr = _pick(R, tr) if R % LANE == 0 else R
    c1 = 1.0 / (1.0 - ADAM_B1 ** ADAM_STEP)
    c2 = 1.0 / (1.0 - ADAM_B2 ** ADAM_STEP)

    def body(p_ref, w_ref, m_ref, v_ref, g_ref, d_ref, nm_ref, nv_ref):
        g = p_ref[0]
        for s in range(1, N_DEV):
            g = g + p_ref[s]
        nm = ADAM_B1 * m_ref[...] + (1.0 - ADAM_B1) * g
        nv = ADAM_B2 * v_ref[...] + (1.0 - ADAM_B2) * (g * g)
        g_ref[...] = g
        nm_ref[...] = nm
        nv_ref[...] = nv
        d_ref[...] = -ADAM_LR * ((nm * c1) / (jnp.sqrt(nv * c2) + ADAM_EPS) + ADAM_WD * w_ref[...])

    blk = pl.BlockSpec((tr, W), lambda i: (i, 0))
    return pl.pallas_call(
        body,
        name=name,
        grid=(R // tr,),
        in_specs=[pl.BlockSpec((N_DEV, tr, W), lambda i: (0, i, 0)), blk, blk, blk],
        out_specs=[blk] * 4,
        out_shape=[jax.ShapeDtypeStruct((R, W), F32)] * 4,
        compiler_params=_cparams(("parallel",)),
    )(parts, w, m, v)


PACK_ROWS = 32768
BIG = ("in", "pa", "pb", "o", "ffin", "ffout")
COL_SHARDED = {"in": True, "pa": True, "pb": True, "o": False, "ffin": True, "ffout": False}
SMALL_ROWS = 80


def _pack(parts, rows, lead=()):
    n = len(lead)
    flat = jnp.concatenate([p.reshape(lead + (-1,)) for p in parts], axis=n)
    pad = rows * LANE - flat.shape[n]
    flat = jnp.pad(flat, ((0, 0),) * n + ((0, pad),))
    return flat.reshape(lead + (rows, LANE))


def _unpack(packed, shapes, lead=()):
    flat = packed.reshape(lead + (-1,))
    out, off = [], 0
    for s in shapes:
        n = math.prod(s)
        out.append(flat[..., off:off + n].reshape(lead + tuple(s)))
        off += n
    return out


def _join_shards(g, col_sharded):
    if col_sharded:
        return g.transpose(1, 2, 0, 3).reshape(g.shape[1], g.shape[2], N_DEV * g.shape[3])
    return g.transpose(1, 0, 2, 3).reshape(g.shape[1], N_DEV * g.shape[2], g.shape[3])


def _split_shards(full, col_sharded):
    d, a, b = full.shape
    if col_sharded:
        return full.reshape(d, a, N_DEV, b // N_DEV).transpose(2, 0, 1, 3)
    return full.reshape(d, N_DEV, a // N_DEV, b).transpose(1, 0, 2, 3)


def _permute_in(w):
    a, b = QKV_W + A_HEADS, QKV_W + A_HEADS + 4 * B_WIDTH
    return jnp.concatenate([w[:, :QKV_W], w[:, b:], w[:, a:b], w[:, QKV_W:a],
                            jnp.zeros((w.shape[0], LANE - A_HEADS), w.dtype)], axis=1)


def _unpermute_in(d):
    g0, h0, f0 = QKV_W, QKV_W + 2 * D_MODEL, QKV_W + C_FA
    return jnp.concatenate([d[:, :QKV_W], d[:, f0:f0 + A_HEADS], d[:, h0:f0], d[:, g0:h0]], axis=1)


def _lower_bounds(logits):
    sm = jax.nn.softmax(logits.astype(F32), axis=0)
    return jnp.cumsum(sm, axis=0) - sm[0:1]


def kernel(x, w_in, b_fgate, hgrn_lb_logits, hgrn_norm_g, w_branch_a, w_branch_b, w_out, ln1_g, ln1_b, w_ff_in, w_ff_out, ln2_g, ln2_b, loss_target, m_w_in, m_b_fgate, m_hgrn_lb_logits, m_hgrn_norm_g, m_w_branch_a, m_w_branch_b, m_w_out, m_ln1_g, m_ln1_b, m_w_ff_in, m_w_ff_out, m_ln2_g, m_ln2_b, v_w_in, v_b_fgate, v_hgrn_lb_logits, v_hgrn_norm_g, v_w_branch_a, v_w_branch_b, v_w_out, v_ln1_g, v_ln1_b, v_w_ff_in, v_w_ff_out, v_ln2_g, v_ln2_b):
    big = {"in": (w_in, m_w_in, v_w_in), "pa": (w_branch_a, m_w_branch_a, v_w_branch_a), "pb": (w_branch_b, m_w_branch_b, v_w_branch_b),
           "o": (w_out, m_w_out, v_w_out), "ffin": (w_ff_in, m_w_ff_in, v_w_ff_in), "ffout": (w_ff_out, m_w_ff_out, v_w_ff_out)}
    small = [(b_fgate, m_b_fgate, v_b_fgate), (hgrn_lb_logits, m_hgrn_lb_logits, v_hgrn_lb_logits), (hgrn_norm_g, m_hgrn_norm_g, v_hgrn_norm_g),
             (ln1_g, m_ln1_g, v_ln1_g), (ln1_b, m_ln1_b, v_ln1_b), (ln2_g, m_ln2_g, v_ln2_g), (ln2_b, m_ln2_b, v_ln2_b)]
    shard_shapes = [big[n][0].shape for n in BIG]
    small_shapes = [s[0].shape for s in small]

    gathered = _all_gather(_pack([big[n][0].astype(BF16) for n in BIG], PACK_ROWS), "gather_weights")
    full = {}
    for n, g in zip(BIG, _unpack(gathered, shard_shapes, lead=(N_DEV,))):
        full[n] = _join_shards(g, COL_SHARDED[n])
    lbounds = _lower_bounds(hgrn_lb_logits)

    def layer_weights(l):
        w = {n: full[n][l] for n in BIG}
        w["in"] = _permute_in(w["in"])
        return w

    def layer_params(l):
        bf_pad = jnp.pad(b_fgate[l].astype(F32), (0, LANE - A_HEADS)).reshape(1, LANE)
        row = lambda a: a[l].astype(F32).reshape(1, -1)
        return dict(bf_pad=bf_pad, lb=lbounds[l].reshape(1, -1), norm_g=row(hgrn_norm_g), ln1_g=row(ln1_g), ln1_b=row(ln1_b),
                    ln2_g=row(ln2_g), ln2_b=row(ln2_b))

    xl = x[0]
    xlb = xl.astype(BF16)
    saved = []
    for l in range(DEPTH):
        p = layer_params(l)
        xl, xlb, sv = _layer_fwd(xl, xlb, layer_weights(l), p["bf_pad"], p["lb"], p["norm_g"], p["ln1_g"], p["ln1_b"],
                                 p["ln2_g"], p["ln2_b"], l)
        saved.append(sv)

    grads = [None] * DEPTH
    smalls = [None] * DEPTH
    loss_part = None
    ua = ub = None
    for l in reversed(range(DEPTH)):
        p, sv = layer_params(l), saved[l]
        if l == DEPTH - 1:
            dz2, dz2b, dg2, db2, loss_part = _ln_bwd_loss(sv["x1"], sv["ffn"], p["ln2_g"], p["ln2_b"], loss_target[0], f"l{l}_dln2")
        else:
            dz2, dz2b, dg2, db2 = _ln_bwd(sv["x1"], sv["ffn"], p["ln2_g"], ua, ub, f"l{l}_dln2")
        ua, ub, grads[l], smalls[l] = _layer_bwd(sv, layer_weights(l), dz2, dz2b, (dg2, db2), p["bf_pad"], p["lb"], p["norm_g"],
                                                 p["ln1_g"], l)
    (grad_x,) = _rowwise(lambda a, b: ALPHA * a + b, "grad_x", [(ua, 0, D_MODEL), (ub, 0, D_MODEL)], [], [(D_MODEL, F32)])

    contrib = []
    for n in BIG:
        per_layer = [_unpermute_in(grads[l][n]) if n == "in" else grads[l][n] for l in range(DEPTH)]
        contrib.append(_split_shards(jnp.stack(per_layer), COL_SHARDED[n]))
    received = _exchange(_pack(contrib, PACK_ROWS, lead=(N_DEV,)), "exchange_grads")
    wmv = [_pack([big[n][i].astype(F32) for n in BIG], PACK_ROWS) for i in range(3)]
    outs_big = [_unpack(o, shard_shapes) for o in _sum_adam(received, *wmv, "adam_big")]

    fold = lambda a: jnp.sum(a, axis=0)
    dlb = jnp.stack([fold(smalls[l]["dlb"]) for l in range(DEPTH)])
    _, lb_vjp = jax.vjp(_lower_bounds, hgrn_lb_logits)
    small_grads = [jnp.stack([fold(smalls[l]["dbf"])[:A_HEADS] for l in range(DEPTH)]), lb_vjp(dlb)[0],
                   jnp.stack([fold(smalls[l]["dng"]) for l in range(DEPTH)]),
                   jnp.stack([fold(smalls[l]["dln1"][0]) for l in range(DEPTH)]), jnp.stack([fold(smalls[l]["dln1"][1]) for l in range(DEPTH)]),
                   jnp.stack([fold(smalls[l]["dln2"][0]) for l in range(DEPTH)]), jnp.stack([fold(smalls[l]["dln2"][1]) for l in range(DEPTH)])]
    loss_local = (0.5 / D_MODEL) * jnp.sum(loss_part)
    small_all = _all_gather(_pack(small_grads + [loss_local.reshape(1)], SMALL_ROWS), "gather_small", vmem=True)
    one = jnp.zeros((1,), F32)
    swmv = [_pack([s[i].astype(F32) for s in small] + [one], SMALL_ROWS) for i in range(3)]
    outs_small = [_unpack(o, small_shapes + [(1,)]) for o in _sum_adam(small_all, *swmv, "adam_small")]
    loss = outs_small[0][-1].reshape(())

    order = {"in": 0, "pa": 4, "pb": 5, "o": 6, "ffin": 9, "ffout": 10}
    small_pos = [1, 2, 3, 7, 8, 11, 12]
    result = [loss, grad_x.reshape(x.shape)]
    for kind in range(4):
        slot = [None] * 13
        for n, arr in zip(BIG, outs_big[kind]):
            slot[order[n]] = arr
        for pos, arr in zip(small_pos, outs_small[kind][:-1]):
            slot[pos] = arr
        result.extend(slot)
    return tuple(result)
```

```python
import functools
import math

import jax
import jax.numpy as jnp
from jax import lax
from jax.experimental import pallas as pl
from jax.experimental.pallas import tpu as pltpu

F32 = jnp.float32
BF16 = jnp.bfloat16

D_MODEL = 1024
DEPTH = 2
CHUNK = 64
A_HEADS = 8
A_HEAD_DIM = 64
A_WIDTH = 512
B_WIDTH = 512
B_HEADS = 4
B_DK = 128
FFN_HIDDEN = 2816
IN_TOTAL = 5640
ALPHA = (2 * DEPTH) ** 0.25
LN_EPS = 1e-5
RMS_EPS = 1e-6
ADAM_LR = 0.001
ADAM_B1 = 0.9
ADAM_B2 = 0.999
ADAM_EPS = 1e-08
ADAM_WD = 0.01
ADAM_STEP = 10

N_DEV = 8
LANE = 128
VMEM_LIMIT = 48 * 1024 * 1024

QKV_W = 3 * A_WIDTH
C_GA, C_GB2 = 0, D_MODEL
C_QB = 2 * D_MODEL
C_FB, C_IB, C_GB = C_QB + B_WIDTH, C_QB + 2 * B_WIDTH, C_QB + 3 * B_WIDTH
C_FA = C_QB + 4 * B_WIDTH
REST_W = C_FA + LANE
IN_PERM = QKV_W + REST_W


def _cparams(sem):
    return pltpu.CompilerParams(dimension_semantics=sem, vmem_limit_bytes=VMEM_LIMIT)


def _pick(n, pref):
    if n <= pref:
        return n
    t = pref
    while n % t:
        t -= LANE
    return t


def _matmul(a, b, mode, out_dtype, name, *, n_off=0, n_size=None, tm=2048, tn=1024, tk=2816):
    if mode == "nn":
        M, K = a.shape
        N = b.shape[1] if n_size is None else n_size
    elif mode == "nt":
        M, K = a.shape
        N = b.shape[0]
    else:
        K, M = a.shape
        N = b.shape[1]
    tm, tn, tk = _pick(M, tm), _pick(N, tn), _pick(K, tk)
    assert n_off % tn == 0
    noff = n_off // tn
    nk = K // tk

    dn = {"nn": (((1,), (0,)), ((), ())), "nt": (((1,), (1,)), ((), ())), "tn": (((0,), (0,)), ((), ()))}[mode]

    def body(a_ref, b_ref, o_ref, *scratch):
        prod = lax.dot_general(a_ref[...], b_ref[...], dn, preferred_element_type=F32)
        if nk == 1:
            o_ref[...] = prod.astype(o_ref.dtype)
            return
        (acc_ref,) = scratch
        k = pl.program_id(2)

        @pl.when(k == 0)
        def _():
            acc_ref[...] = prod

        @pl.when(k > 0)
        def _():
            acc_ref[...] += prod

        @pl.when(k == nk - 1)
        def _():
            o_ref[...] = acc_ref[...].astype(o_ref.dtype)

    if mode == "nn":
        a_spec = pl.BlockSpec((tm, tk), lambda i, j, k: (i, k))
        b_spec = pl.BlockSpec((tk, tn), lambda i, j, k: (k, j + noff))
    elif mode == "nt":
        a_spec = pl.BlockSpec((tm, tk), lambda i, j, k: (i, k))
        b_spec = pl.BlockSpec((tn, tk), lambda i, j, k: (j, k))
    else:
        a_spec = pl.BlockSpec((tk, tm), lambda i, j, k: (k, i))
        b_spec = pl.BlockSpec((tk, tn), lambda i, j, k: (k, j))
    return pl.pallas_call(
        body,
        name=name,
        grid=(M // tm, N // tn, nk),
        in_specs=[a_spec, b_spec],
        out_specs=pl.BlockSpec((tm, tn), lambda i, j, k: (i, j)),
        out_shape=jax.ShapeDtypeStruct((M, N), out_dtype),
        scratch_shapes=[pltpu.VMEM((tm, tn), F32)] if nk > 1 else [],
        compiler_params=_cparams(("parallel", "parallel", "arbitrary")),
    )(a, b)


def _rowwise(fn, name, tiled, params, outs, sums=(), *, tr=256):
    T = tiled[0][0].shape[0]
    tr = _pick(T, tr)
    nt, npar, no, ns = len(tiled), len(params), len(outs), len(sums)

    def body(*refs):
        ins = [r[...] for r in refs[:nt + npar]]
        res = fn(*ins)
        if not isinstance(res, (tuple, list)):
            res = (res,)
        o_refs = refs[nt + npar:nt + npar + no]
        s_refs = refs[nt + npar + no:]
        for r, v in zip(o_refs, res[:no]):
            r[...] = v.astype(r.dtype)
        if ns:
            @pl.when(pl.program_id(0) == 0)
            def _():
                for r in s_refs:
                    r[...] = jnp.zeros_like(r)
            for r, v in zip(s_refs, res[no:]):
                r[...] += v

    in_specs = []
    args = []
    for arr, off, w in tiled:
        assert off % w == 0 or w == arr.shape[1]
        cb = off // w
        in_specs.append(pl.BlockSpec((tr, w), lambda i, cb=cb: (i, cb)))
        args.append(arr)
    for p in params:
        in_specs.append(pl.BlockSpec(p.shape, lambda i, nd=p.ndim: (0,) * nd))
        args.append(p)
    out_specs = [pl.BlockSpec((tr, w), lambda i: (i, 0)) for w, _ in outs]
    out_shape = [jax.ShapeDtypeStruct((T, w), dt) for w, dt in outs]
    out_specs += [pl.BlockSpec((8, w), lambda i: (0, 0)) for w in sums]
    out_shape += [jax.ShapeDtypeStruct((8, w), F32) for w in sums]
    res = pl.pallas_call(
        body,
        name=name,
        grid=(T // tr,),
        in_specs=in_specs,
        out_specs=out_specs,
        out_shape=out_shape,
        compiler_params=_cparams(("arbitrary",)),
    )(*args)
    return res


def _colsum8(v):
    r, w = v.shape
    return jnp.sum(v.reshape(r // 8, 8, w), axis=0)


NEG_BIG = -1e30
QK_SCALE = A_HEAD_DIM ** -0.5
PAIRS = A_HEADS // 2
QKV_BLOCKS = A_WIDTH // LANE
FOX_TQ = 1024
FOX_TK = 1024
_NT = (((1,), (1,)), ((), ()))
_NN = (((1,), (0,)), ((), ()))
_TN = (((0,), (0,)), ((), ()))


def _first_head(shape):
    return lax.broadcasted_iota(jnp.int32, shape, 1) < A_HEAD_DIM


def _split_heads(a, first):
    return jnp.where(first, a, 0), jnp.where(first, 0, a)


def _visible(i, j, tq, tk, keys_on_rows=False):
    shape = (tk, tq) if keys_on_rows else (tq, tk)
    r = lax.broadcasted_iota(jnp.int32, shape, 0)
    c = lax.broadcasted_iota(jnp.int32, shape, 1)
    return ((c - r) if keys_on_rows else (r - c)) >= j * tk - i * tq


def _lanes01(a0, a1, rows):
    lane = lax.broadcasted_iota(jnp.int32, (rows, LANE), 1)
    return jnp.where(lane == 0, a0, jnp.where(lane == 1, a1, 0.0))


def _fox_fwd(qkv, frow, tq, tk):
    T = qkv.shape[0]
    nbk = T // tk

    def body(q_ref, k_ref, v_ref, fk_ref, o_ref, ob_ref, lse_ref, m_ref, l_ref, acc_ref):
        i = pl.program_id(1)
        first = _first_head((tq, LANE))
        qs = _split_heads(q_ref[...] * QK_SCALE, first)
        m_ref[...] = jnp.full_like(m_ref, NEG_BIG)
        l_ref[...] = jnp.zeros_like(l_ref)
        acc_ref[...] = jnp.zeros_like(acc_ref)

        def step(j, diag):
            rows = pl.ds(pl.multiple_of(j * tk, tk), tk)
            k2, v2 = k_ref[rows, :], v_ref[rows, :]
            for hh in range(2):
                s = lax.dot_general(qs[hh], k2, _NT, preferred_element_type=F32) - fk_ref[hh, j]
                if diag:
                    s = jnp.where(_visible(i, j, tq, tk), s, NEG_BIG)
                m_old = m_ref[hh]
                m_new = jnp.maximum(m_old, jnp.max(s, axis=1, keepdims=True))
                p = jnp.exp(s - m_new[:, :1])
                a = jnp.exp(m_old - m_new)
                l_ref[hh] = a * l_ref[hh] + jnp.sum(p, axis=1, keepdims=True)
                acc_ref[hh] = a * acc_ref[hh] + lax.dot_general(p.astype(BF16), v2, _NN, preferred_element_type=F32)
                m_ref[hh] = m_new

        def off_diag(j, c):
            step(j, False)
            return c

        n_full = lax.shift_right_logical(i * tq, tk.bit_length() - 1)
        lax.fori_loop(0, n_full, off_diag, 0)
        for d in range(max(1, tq // tk)):
            step(n_full + d, True)
        o = jnp.where(first, acc_ref[0] / l_ref[0], acc_ref[1] / l_ref[1])
        o_ref[...] = o
        ob_ref[...] = o.astype(BF16)
        lse_ref[0] = _lanes01((m_ref[0] + jnp.log(l_ref[0]))[:, :1], (m_ref[1] + jnp.log(l_ref[1]))[:, :1], tq)

    col_blk = pl.BlockSpec((tq, LANE), lambda hp, i: (i, hp))
    return pl.pallas_call(
        body,
        name="fox_fwd",
        grid=(PAIRS, T // tq),
        in_specs=[col_blk,
                  pl.BlockSpec((T, LANE), lambda hp, i: (0, QKV_BLOCKS + hp)),
                  pl.BlockSpec((T, LANE), lambda hp, i: (0, 2 * QKV_BLOCKS + hp)),
                  pl.BlockSpec((2, nbk, 1, tk), lambda hp, i: (hp, 0, 0, 0))],
        out_specs=[col_blk, col_blk, pl.BlockSpec((1, tq, LANE), lambda hp, i: (hp, i, 0))],
        out_shape=[jax.ShapeDtypeStruct((T, A_WIDTH), F32), jax.ShapeDtypeStruct((T, A_WIDTH), BF16),
                   jax.ShapeDtypeStruct((PAIRS, T, LANE), F32)],
        scratch_shapes=[pltpu.VMEM((2, tq, LANE), F32)] * 3,
        compiler_params=_cparams(("parallel", "arbitrary")),
    )(qkv, qkv, qkv, frow)


def _fox_dq(qkv, do, o, lse, frow, tq, tk):
    T = qkv.shape[0]
    nbk = T // tk

    def body(q_ref, k_ref, v_ref, do_ref, o_ref, lse_ref, fk_ref, dq_ref, dl_ref, df_ref, acc_ref, rs_ref):
        i = pl.program_id(1)
        first = _first_head((tq, LANE))
        qs = _split_heads(q_ref[...] * QK_SCALE, first)
        do2 = do_ref[...]
        dos = _split_heads(do2, first)
        prod = do2.astype(F32) * o_ref[...]
        deltas = (jnp.sum(jnp.where(first, prod, 0.0), axis=1, keepdims=True),
                  jnp.sum(jnp.where(first, 0.0, prod), axis=1, keepdims=True))
        lse2 = lse_ref[0]
        lses = (lse2[:, 0:1], lse2[:, 1:2])
        dl_ref[0] = _lanes01(deltas[0], deltas[1], tq)
        acc_ref[...] = jnp.zeros_like(acc_ref)
        rs_ref[...] = jnp.zeros_like(rs_ref)

        def step(j, diag):
            rows = pl.ds(pl.multiple_of(j * tk, tk), tk)
            k2, v2 = k_ref[rows, :], v_ref[rows, :]
            for hh in range(2):
                s = lax.dot_general(qs[hh], k2, _NT, preferred_element_type=F32) - fk_ref[hh, j]
                p = jnp.exp(s - lses[hh])
                if diag:
                    p = jnp.where(_visible(i, j, tq, tk), p, 0.0)
                dp = lax.dot_general(dos[hh], v2, _NT, preferred_element_type=F32)
                ds = p * (dp - deltas[hh])
                acc_ref[hh] += lax.dot_general(ds.astype(BF16), k2, _NN, preferred_element_type=F32)
                rs_ref[hh] += jnp.sum(ds, axis=1, keepdims=True)

        def off_diag(j, c):
            step(j, False)
            return c

        n_full = lax.shift_right_logical(i * tq, tk.bit_length() - 1)
        lax.fori_loop(0, n_full, off_diag, 0)
        for d in range(max(1, tq // tk)):
            step(n_full + d, True)
        dq_ref[...] = (jnp.where(first, acc_ref[0], acc_ref[1]) * QK_SCALE).astype(BF16)
        df_ref[0] = _lanes01(rs_ref[0][:, 0:1], rs_ref[1][:, 0:1], tq)

    pair_blk = pl.BlockSpec((1, tq, LANE), lambda hp, i: (hp, i, 0))
    col_blk = pl.BlockSpec((tq, LANE), lambda hp, i: (i, hp))
    return pl.pallas_call(
        body,
        name="fox_dq",
        grid=(PAIRS, T // tq),
        in_specs=[col_blk,
                  pl.BlockSpec((T, LANE), lambda hp, i: (0, QKV_BLOCKS + hp)),
                  pl.BlockSpec((T, LANE), lambda hp, i: (0, 2 * QKV_BLOCKS + hp)),
                  col_blk, col_blk, pair_blk,
                  pl.BlockSpec((2, nbk, 1, tk), lambda hp, i: (hp, 0, 0, 0))],
        out_specs=[col_blk, pair_blk, pair_blk],
        out_shape=[jax.ShapeDtypeStruct((T, A_WIDTH), BF16), jax.ShapeDtypeStruct((PAIRS, T, LANE), F32),
                   jax.ShapeDtypeStruct((PAIRS, T, LANE), F32)],
        scratch_shapes=[pltpu.VMEM((2, tq, LANE), F32)] * 2,
        compiler_params=_cparams(("parallel", "arbitrary")),
    )(qkv, qkv, qkv, do, o, lse, frow)


def _fox_dkv(qkv, do, lse_row, delta_row, fpad, tq, tk):
    T = qkv.shape[0]
    nbq = T // tq

    def body(q_ref, k_ref, v_ref, do_ref, lse_ref, dl_ref, f_ref, dk_ref, dv_ref, df_ref, dk_acc, dv_acc, df_acc):
        hp = pl.program_id(0)
        j = pl.program_id(1)
        first = _first_head((tk, LANE))
        ks = _split_heads(k_ref[...], first)
        vs = _split_heads(v_ref[...], first)
        lane = lax.broadcasted_iota(jnp.int32, (tk, LANE), 1)
        fks = tuple(jnp.sum(jnp.where(lane == 2 * hp + hh, f_ref[...], 0.0), axis=1, keepdims=True) for hh in range(2))
        dk_acc[...] = jnp.zeros_like(dk_acc)
        dv_acc[...] = jnp.zeros_like(dv_acc)
        df_acc[...] = jnp.zeros_like(df_acc)

        def step(i, diag):
            rows = pl.ds(pl.multiple_of(i * tq, tq), tq)
            q2 = q_ref[rows, :] * QK_SCALE
            do2 = do_ref[rows, :]
            for hh in range(2):
                st = lax.dot_general(ks[hh], q2, _NT, preferred_element_type=F32) - fks[hh]
                pt = jnp.exp(st - lse_ref[hh, i])
                if diag:
                    pt = jnp.where(_visible(i, j, tq, tk, keys_on_rows=True), pt, 0.0)
                dpt = lax.dot_general(vs[hh], do2, _NT, preferred_element_type=F32)
                dst = pt * (dpt - dl_ref[hh, i])
                dv_acc[hh] += lax.dot_general(pt.astype(BF16), do2, _NN, preferred_element_type=F32)
                dk_acc[hh] += lax.dot_general(dst.astype(BF16), q2, _NN, preferred_element_type=F32)
                df_acc[hh] += jnp.sum(dst, axis=1, keepdims=True)

        i0 = lax.shift_right_logical(j * tk, tq.bit_length() - 1)
        nd = max(1, tk // tq)
        for d in range(nd):
            step(i0 + d, True)

        def off_diag(i, c):
            step(i, False)
            return c

        lax.fori_loop(i0 + nd, nbq, off_diag, 0)
        dk_ref[...] = jnp.where(first, dk_acc[0], dk_acc[1]).astype(BF16)
        dv_ref[...] = jnp.where(first, dv_acc[0], dv_acc[1]).astype(BF16)
        df_ref[0] = _lanes01(-df_acc[0][:, 0:1], -df_acc[1][:, 0:1], tk)

    whole = lambda cb: pl.BlockSpec((T, LANE), lambda hp, j, cb=cb: (0, cb + hp))
    blk = lambda cb: pl.BlockSpec((tk, LANE), lambda hp, j, cb=cb: (j, cb + hp))
    rows = pl.BlockSpec((2, nbq, 1, tq), lambda hp, j: (hp, 0, 0, 0))
    return pl.pallas_call(
        body,
        name="fox_dkv",
        grid=(PAIRS, T // tk),
        in_specs=[whole(0), blk(QKV_BLOCKS), blk(2 * QKV_BLOCKS), whole(0), rows, rows, pl.BlockSpec((tk, LANE), lambda hp, j: (j, 0))],
        out_specs=[blk(0), blk(0), pl.BlockSpec((1, tk, LANE), lambda hp, j: (hp, j, 0))],
        out_shape=[jax.ShapeDtypeStruct((T, A_WIDTH), BF16), jax.ShapeDtypeStruct((T, A_WIDTH), BF16),
                   jax.ShapeDtypeStruct((PAIRS, T, LANE), F32)],
        scratch_shapes=[pltpu.VMEM((2, tk, LANE), F32)] * 3,
        compiler_params=_cparams(("parallel", "arbitrary")),
    )(qkv, qkv, qkv, do, lse_row, delta_row, fpad)


HG_ROWS = 512
_HI = lax.Precision.HIGHEST


def _tri(n, upper):
    r = lax.broadcasted_iota(jnp.int32, (n, n), 0)
    c = lax.broadcasted_iota(jnp.int32, (n, n), 1)
    return jnp.where((r <= c) if upper else (r >= c), 1.0, 0.0).astype(F32)


def _dot(a, b, dn):
    return lax.dot_general(a.astype(BF16), b.astype(BF16), dn, preferred_element_type=F32)


def _hgrn_gates(qb, fb, lb):
    sig = jax.nn.sigmoid(fb)
    f = lb + (1.0 - lb) * sig
    sq = jax.nn.sigmoid(qb)
    return sig, f, sq, qb * sq


def _hgrn_fwd(rest, lb, col_q, col_f, col_i):
    T = rest.shape[0]
    tb = _pick(T, HG_ROWS)
    cpb = tb // CHUNK
    nblk = T // tb

    def body(q_ref, f_ref, i_ref, lb_ref, o_ref, st_ref, s_ref):
        @pl.when(pl.program_id(1) == 0)
        def _():
            s_ref[...] = jnp.zeros_like(s_ref)

        lbv = lb_ref[...]
        tril = _tri(CHUNK, False)
        mask = tril > 0.5

        def chunk(c, carry):
            rows = pl.ds(pl.multiple_of(c * CHUNK, CHUNK), CHUNK)
            _, f, _, q = _hgrn_gates(q_ref[rows, :], f_ref[rows, :], lbv)
            v = i_ref[rows, :]
            k = 1.0 - f
            b = jnp.dot(tril, jnp.log(f), precision=_HI, preferred_element_type=F32)
            bm = b[CHUNK // 2 - 1:CHUNK // 2, :]
            bl = b[CHUNK - 1:CHUNK, :]
            st = s_ref[...]
            st_ref[0, c] = st
            p = jnp.where(mask, _dot(q * jnp.exp(b - bm), k * jnp.exp(bm - b), _NT), 0.0)
            o_ref[rows, :] = _dot(p, v, _NN) + _dot(q * jnp.exp(b), st, _NT)
            s_ref[...] = st * jnp.exp(bl) + _dot(v, k * jnp.exp(bl - b), _TN)
            return carry

        lax.fori_loop(0, cpb, chunk, 0)

    col = lambda cb: pl.BlockSpec((tb, B_DK), lambda h, i, cb=cb: (i, cb + h))
    return pl.pallas_call(
        body,
        name="hgrn_fwd",
        grid=(B_HEADS, nblk),
        in_specs=[col(col_q // B_DK), col(col_f // B_DK), col(col_i // B_DK), pl.BlockSpec((1, B_DK), lambda h, i: (0, h))],
        out_specs=[pl.BlockSpec((tb, B_DK), lambda h, i: (i, h)),
                   pl.BlockSpec((1, cpb, B_DK, B_DK), lambda h, i: (h, i, 0, 0))],
        out_shape=[jax.ShapeDtypeStruct((T, B_WIDTH), F32), jax.ShapeDtypeStruct((B_HEADS, T // CHUNK, B_DK, B_DK), F32)],
        scratch_shapes=[pltpu.VMEM((B_DK, B_DK), F32)],
        compiler_params=_cparams(("parallel", "arbitrary")),
    )(rest, rest, rest, lb)


def _hgrn_bwd(rest, lb, states, do, col_q, col_f, col_i):
    T = rest.shape[0]
    tb = _pick(T, HG_ROWS)
    cpb = tb // CHUNK
    nblk = T // tb

    def body(q_ref, f_ref, i_ref, lb_ref, st_ref, do_ref, dq_ref, df_ref, di_ref, dlb_ref, ds_ref):
        @pl.when(pl.program_id(1) == 0)
        def _():
            ds_ref[...] = jnp.zeros_like(ds_ref)
            dlb_ref[...] = jnp.zeros_like(dlb_ref)

        lbv = lb_ref[...]
        tril = _tri(CHUNK, False)
        triu = _tri(CHUNK, True)
        mask = tril > 0.5

        def chunk(cc, carry):
            c = cpb - 1 - cc
            rows = pl.ds(pl.multiple_of(c * CHUNK, CHUNK), CHUNK)
            qb = q_ref[rows, :]
            sig, f, sq, q = _hgrn_gates(qb, f_ref[rows, :], lbv)
            v = i_ref[rows, :]
            dov = do_ref[rows, :]
            k = 1.0 - f
            b = jnp.dot(tril, jnp.log(f), precision=_HI, preferred_element_type=F32)
            bm = b[CHUNK // 2 - 1:CHUNK // 2, :]
            bl = b[CHUNK - 1:CHUNK, :]
            eb, eqm, ekm, ekl, ebl = jnp.exp(b), jnp.exp(b - bm), jnp.exp(bm - b), jnp.exp(bl - b), jnp.exp(bl)
            qt, kt, qa, kd = q * eqm, k * ekm, q * eb, k * ekl
            st0 = st_ref[0, c]
            dst1 = ds_ref[...]
            p = jnp.where(mask, _dot(qt, kt, _NT), 0.0)
            dp = jnp.where(mask, _dot(dov, v, _NT), 0.0)
            dv = _dot(p, dov, _TN) + _dot(kd, dst1, _NT)
            dqt = _dot(dp, kt, _NN)
            dkt = _dot(dp, qt, _TN)
            dqa = _dot(dov, st0, _NN)
            dkd = _dot(v, dst1, _NN)
            ds_ref[...] = _dot(dov, qa, _TN) + dst1 * ebl
            dq = dqt * eqm + dqa * eb
            dk = dkt * ekm + dkd * ekl
            rnd = lambda a: a.astype(BF16).astype(F32)
            db = dqt * rnd(qt) - dkt * rnd(kt) + dqa * qa - dkd * kd
            last = jnp.sum(dkd * kd, axis=0, keepdims=True) + ebl * jnp.sum(dst1 * st0, axis=0, keepdims=True)
            dg = jnp.dot(triu, db, precision=_HI, preferred_element_type=F32) + last
            dfv = dg / f - dk
            dq_ref[rows, :] = (dq * (sq * (1.0 + qb * (1.0 - sq)))).astype(BF16)
            df_ref[rows, :] = (dfv * (1.0 - lbv) * sig * (1.0 - sig)).astype(BF16)
            di_ref[rows, :] = dv.astype(BF16)
            dlb_ref[0:1, :] += jnp.sum(dfv * (1.0 - sig), axis=0, keepdims=True)
            return carry

        lax.fori_loop(0, cpb, chunk, 0)

    rev = lambda i: nblk - 1 - i
    col = lambda cb: pl.BlockSpec((tb, B_DK), lambda h, i, cb=cb: (rev(i), cb + h))
    return pl.pallas_call(
        body,
        name="hgrn_bwd",
        grid=(B_HEADS, nblk),
        in_specs=[col(col_q // B_DK), col(col_f // B_DK), col(col_i // B_DK), pl.BlockSpec((1, B_DK), lambda h, i: (0, h)),
                  pl.BlockSpec((1, cpb, B_DK, B_DK), lambda h, i: (h, rev(i), 0, 0)), col(0)],
        out_specs=[col(0), col(0), col(0), pl.BlockSpec((8, B_DK), lambda h, i: (0, h))],
        out_shape=[jax.ShapeDtypeStruct((T, B_WIDTH), BF16)] * 3 + [jax.ShapeDtypeStruct((8, B_WIDTH), F32)],
        scratch_shapes=[pltpu.VMEM((B_DK, B_DK), F32)],
        compiler_params=_cparams(("parallel", "arbitrary")),
    )(rest, rest, rest, lb, states, do)


SCAN_ROWS = 256


def _log_sigmoid(x):
    return jnp.minimum(x, 0.0) - jnp.log(1.0 + jnp.exp(-jnp.abs(x)))


def _head_cols(shape):
    return lax.broadcasted_iota(jnp.int32, shape, 1) < A_HEADS


def _fgate_fwd(rest, bf_pad):
    T = rest.shape[0]
    tr = _pick(T, SCAN_ROWS)

    def body(fa_ref, bf_ref, f_ref, carry_ref):
        @pl.when(pl.program_id(0) == 0)
        def _():
            carry_ref[...] = jnp.zeros_like(carry_ref)

        lf = jnp.where(_head_cols((tr, LANE)), _log_sigmoid(fa_ref[...] + bf_ref[...]), 0.0)
        f = jnp.dot(_tri(tr, False), lf, precision=_HI, preferred_element_type=F32) + carry_ref[...]
        f_ref[...] = f
        carry_ref[...] = f[tr - 1:tr, :]

    return pl.pallas_call(
        body,
        name="fgate_fwd",
        grid=(T // tr,),
        in_specs=[pl.BlockSpec((tr, LANE), lambda i: (i, C_FA // LANE)), pl.BlockSpec((1, LANE), lambda i: (0, 0))],
        out_specs=pl.BlockSpec((tr, LANE), lambda i: (i, 0)),
        out_shape=jax.ShapeDtypeStruct((T, LANE), F32),
        scratch_shapes=[pltpu.VMEM((1, LANE), F32)],
        compiler_params=_cparams(("arbitrary",)),
    )(rest, bf_pad)


def _fgate_bwd(dfq, dfk, rest, bf_pad):
    T = rest.shape[0]
    tr = _pick(T, SCAN_ROWS)
    nblk = T // tr

    def body(dq_ref, dk_ref, fa_ref, bf_ref, dfa_ref, db_ref, carry_ref):
        @pl.when(pl.program_id(0) == 0)
        def _():
            carry_ref[...] = jnp.zeros_like(carry_ref)
            db_ref[...] = jnp.zeros_like(db_ref)

        df = dq_ref[0] + dk_ref[0]
        for hp in range(1, PAIRS):
            df = df + pltpu.roll(dq_ref[hp] + dk_ref[hp], 2 * hp, 1)
        dlf = jnp.dot(_tri(tr, True), df, precision=_HI, preferred_element_type=F32) + carry_ref[...]
        carry_ref[...] = dlf[0:1, :]
        dfa = jnp.where(_head_cols((tr, LANE)), dlf * jax.nn.sigmoid(-(fa_ref[...] + bf_ref[...])), 0.0)
        dfa_ref[...] = dfa.astype(BF16)
        db_ref[...] += _colsum8(dfa)

    rev = lambda i: nblk - 1 - i
    return pl.pallas_call(
        body,
        name="fgate_bwd",
        grid=(nblk,),
        in_specs=[pl.BlockSpec((PAIRS, tr, LANE), lambda i: (0, rev(i), 0)), pl.BlockSpec((PAIRS, tr, LANE), lambda i: (0, rev(i), 0)),
                  pl.BlockSpec((tr, LANE), lambda i: (rev(i), C_FA // LANE)), pl.BlockSpec((1, LANE), lambda i: (0, 0))],
        out_specs=[pl.BlockSpec((tr, LANE), lambda i: (rev(i), 0)), pl.BlockSpec((8, LANE), lambda i: (0, 0))],
        out_shape=[jax.ShapeDtypeStruct((T, LANE), BF16), jax.ShapeDtypeStruct((8, LANE), F32)],
        scratch_shapes=[pltpu.VMEM((1, LANE), F32)],
        compiler_params=_cparams(("arbitrary",)),
    )(dfq, dfk, rest, bf_pad)


def _head_rows(a, t):
    return a[:, :A_HEADS].T.reshape(A_HEADS, a.shape[0] // t, 1, t)


def _pair_rows(a, t):
    return a[:, :, :2].transpose(0, 2, 1).reshape(A_HEADS, a.shape[1] // t, 1, t)


def _ln_stats(z):
    mu = jnp.mean(z, axis=-1, keepdims=True)
    zc = z - mu
    rstd = lax.rsqrt(jnp.mean(zc * zc, axis=-1, keepdims=True) + LN_EPS)
    return zc * rstd, rstd


def _ln_fwd(x, y, g, b, name):
    def fn(xv, yv, gv, bv):
        xhat, _ = _ln_stats(ALPHA * xv + yv)
        out = xhat * gv + bv
        return out, out
    return _rowwise(fn, name, [(x, 0, D_MODEL), (y, 0, D_MODEL)], [g, b], [(D_MODEL, F32), (D_MODEL, BF16)])


def _ln_bwd_core(xv, yv, gv, dy):
    xhat, rstd = _ln_stats(ALPHA * xv + yv)
    dxh = dy * gv
    dz = rstd * (dxh - jnp.mean(dxh, axis=-1, keepdims=True) - xhat * jnp.mean(dxh * xhat, axis=-1, keepdims=True))
    return dz, _colsum8(dy * xhat), _colsum8(dy)


def _ln_bwd(x, y, g, ua, ub, name):
    def fn(xv, yv, uav, ubv, gv):
        dz, dg, db = _ln_bwd_core(xv, yv, gv, ALPHA * uav + ubv)
        return dz, dz, dg, db
    return _rowwise(fn, name, [(x, 0, D_MODEL), (y, 0, D_MODEL), (ua, 0, D_MODEL), (ub, 0, D_MODEL)], [g],
                    [(D_MODEL, F32), (D_MODEL, BF16)], [D_MODEL, D_MODEL])


def _ln_bwd_loss(x, y, g, b, target, name):
    def fn(xv, yv, tv, gv, bv):
        xhat, _ = _ln_stats(ALPHA * xv + yv)
        err = xhat * gv + bv - tv
        dz, dg, db = _ln_bwd_core(xv, yv, gv, err * (1.0 / D_MODEL))
        return dz, dz, dg, db, _colsum8(err * err)
    return _rowwise(fn, name, [(x, 0, D_MODEL), (y, 0, D_MODEL), (target, 0, D_MODEL)], [g, b],
                    [(D_MODEL, F32), (D_MODEL, BF16)], [D_MODEL, D_MODEL, D_MODEL])


def _silu_parts(g):
    s = jax.nn.sigmoid(g)
    return g * s, s * (1.0 + g * (1.0 - s))


def _layer_fwd(x, xb, w, bf_pad, lb, norm_g, ln1_g, ln1_b, ln2_g, ln2_b, l):
    T = x.shape[0]
    tq, tk = min(T, FOX_TQ), min(T, FOX_TK)
    sv = {}
    qkv = _matmul(xb, w["in"], "nn", BF16, f"l{l}_qkv", n_off=0, n_size=QKV_W, tn=768)
    rest = _matmul(xb, w["in"], "nn", F32, f"l{l}_rest", n_off=QKV_W, n_size=REST_W, tn=384)
    fcum = _fgate_fwd(rest, bf_pad)
    frow = _head_rows(fcum, tk)
    o_a, ya, lse = _fox_fwd(qkv, frow, tq, tk)
    o_b, states = _hgrn_fwd(rest, lb, C_QB, C_FB, C_IB)

    def yb_fn(ov, gbv, ngv):
        outs = []
        for h in range(B_HEADS):
            oh = ov[:, h * B_DK:(h + 1) * B_DK]
            r = lax.rsqrt(jnp.mean(oh * oh, axis=-1, keepdims=True) + RMS_EPS)
            outs.append(oh * r * ngv)
        return jnp.concatenate(outs, axis=1) * jax.nn.sigmoid(gbv)
    (yb,) = _rowwise(yb_fn, f"l{l}_yb", [(o_b, 0, B_WIDTH), (rest, C_GB, B_WIDTH)], [norm_g], [(B_WIDTH, BF16)])
    pa = _matmul(ya, w["pa"], "nn", F32, f"l{l}_pa")
    pb = _matmul(yb, w["pb"], "nn", F32, f"l{l}_pb")

    def merge_fn(gav, gbv, pav, pbv):
        return jax.nn.sigmoid(gav) * pav + jax.nn.sigmoid(gbv) * pbv
    (merged,) = _rowwise(merge_fn, f"l{l}_merge", [(rest, C_GA, D_MODEL), (rest, C_GB2, D_MODEL), (pa, 0, D_MODEL), (pb, 0, D_MODEL)],
                         [], [(D_MODEL, BF16)])
    mix = _matmul(merged, w["o"], "nn", F32, f"l{l}_mix")
    x1, x1b = _ln_fwd(x, mix, ln1_g, ln1_b, f"l{l}_ln1")
    hid = _matmul(x1b, w["ffin"], "nn", F32, f"l{l}_ffin", tn=512)

    def act_fn(uv, gv):
        return _silu_parts(gv)[0] * uv
    (act,) = _rowwise(act_fn, f"l{l}_act", [(hid, 0, FFN_HIDDEN), (hid, FFN_HIDDEN, FFN_HIDDEN)], [], [(FFN_HIDDEN, BF16)], tr=128)
    ffn = _matmul(act, w["ffout"], "nn", F32, f"l{l}_ffout", tm=1024)
    x2, x2b = _ln_fwd(x1, ffn, ln2_g, ln2_b, f"l{l}_ln2")
    sv.update(x=x, xb=xb, rest=rest, qkv=qkv, fcum=fcum, frow=frow, o_a=o_a, lse=lse, ya=ya, o_b=o_b, states=states,
              yb=yb, pa=pa, pb=pb, merged=merged, mix=mix, x1=x1, x1b=x1b, hid=hid, act=act, ffn=ffn)
    return x2, x2b, sv


def _layer_bwd(sv, w, dz2, dz2b, dln2, bf_pad, lb, norm_g, ln1_g, l):
    T = sv["x"].shape[0]
    tq, tk = min(T, FOX_TQ), min(T, FOX_TK)
    gr = {}
    dact = _matmul(dz2b, w["ffout"], "nt", F32, f"l{l}_dact", tm=1024, tn=1408)
    gr["ffout"] = _matmul(sv["act"], dz2b, "tn", F32, f"l{l}_dwffout", tm=1408, tn=512, tk=2048)

    def dhid_fn(dav, uv, gv):
        sg, dsg = _silu_parts(gv)
        return jnp.concatenate([dav * sg, dav * uv * dsg], axis=1)
    (dhid,) = _rowwise(dhid_fn, f"l{l}_dhid", [(dact, 0, FFN_HIDDEN), (sv["hid"], 0, FFN_HIDDEN), (sv["hid"], FFN_HIDDEN, FFN_HIDDEN)],
                       [], [(2 * FFN_HIDDEN, BF16)], tr=128)
    dx1f = _matmul(dhid, w["ffin"], "nt", F32, f"l{l}_dx1f", tm=1024)
    gr["ffin"] = _matmul(sv["x1b"], dhid, "tn", F32, f"l{l}_dwffin", tm=1024, tn=512, tk=4096)
    dz1, dz1b, dg1, db1 = _ln_bwd(sv["x"], sv["mix"], ln1_g, dz2, dx1f, f"l{l}_dln1")
    dmerged = _matmul(dz1b, w["o"], "nt", F32, f"l{l}_dmerged")
    gr["o"] = _matmul(sv["merged"], dz1b, "tn", F32, f"l{l}_dwo", tm=1024, tn=512, tk=4096)

    def dmerge_fn(dmv, gav, gbv, pav, pbv):
        sa, sb = jax.nn.sigmoid(gav), jax.nn.sigmoid(gbv)
        return dmv * sa, dmv * sb, dmv * pav * sa * (1.0 - sa), dmv * pbv * sb * (1.0 - sb)
    dpa, dpb, dga, dgb2 = _rowwise(
        dmerge_fn, f"l{l}_dmerge",
        [(dmerged, 0, D_MODEL), (sv["rest"], C_GA, D_MODEL), (sv["rest"], C_GB2, D_MODEL), (sv["pa"], 0, D_MODEL), (sv["pb"], 0, D_MODEL)],
        [], [(D_MODEL, BF16)] * 4)
    dya = _matmul(dpa, w["pa"], "nt", BF16, f"l{l}_dya", tn=512)
    gr["pa"] = _matmul(sv["ya"], dpa, "tn", F32, f"l{l}_dwpa", tm=512, tn=512, tk=4096)
    dyb = _matmul(dpb, w["pb"], "nt", F32, f"l{l}_dyb", tn=512)
    gr["pb"] = _matmul(sv["yb"], dpb, "tn", F32, f"l{l}_dwpb", tm=512, tn=512, tk=4096)

    def dyb_fn(dyv, ov, gbv, ngv):
        sg = jax.nn.sigmoid(gbv)
        dos, dgs = [], []
        dng = jnp.zeros((8, B_DK), F32)
        for h in range(B_HEADS):
            sl = slice(h * B_DK, (h + 1) * B_DK)
            oh, dyh, sgh = ov[:, sl], dyv[:, sl], sg[:, sl]
            r = lax.rsqrt(jnp.mean(oh * oh, axis=-1, keepdims=True) + RMS_EPS)
            n = oh * r
            dyn = dyh * sgh
            dgs.append(dyh * n * ngv * sgh * (1.0 - sgh))
            dng = dng + _colsum8(dyn * n)
            dn = dyn * ngv
            dos.append(r * (dn - n * jnp.mean(dn * n, axis=-1, keepdims=True)))
        return jnp.concatenate(dos, axis=1), jnp.concatenate(dgs, axis=1), dng
    do_b, dgb, dng = _rowwise(dyb_fn, f"l{l}_dyb2", [(dyb, 0, B_WIDTH), (sv["o_b"], 0, B_WIDTH), (sv["rest"], C_GB, B_WIDTH)], [norm_g],
                              [(B_WIDTH, F32), (B_WIDTH, BF16)], [B_DK])
    dqb, dfb, dib, dlb = _hgrn_bwd(sv["rest"], lb, sv["states"], do_b, C_QB, C_FB, C_IB)
    dq, delta, dfq = _fox_dq(sv["qkv"], dya, sv["o_a"], sv["lse"], sv["frow"], tq, tk)
    dk, dv, dfk = _fox_dkv(sv["qkv"], dya, _pair_rows(sv["lse"], tq), _pair_rows(delta, tq), sv["fcum"], tq, tk)
    dfa, dbf = _fgate_bwd(dfq, dfk, sv["rest"], bf_pad)
    dproj = jnp.concatenate([dq, dk, dv, dga, dgb2, dqb, dfb, dib, dgb, dfa], axis=1)
    dxm = _matmul(dproj, w["in"], "nt", F32, f"l{l}_dxm", tm=1024, tk=1920)
    gr["in"] = _matmul(sv["xb"], dproj, "tn", F32, f"l{l}_dwin", tm=1024, tn=640, tk=4096)
    small = dict(dln2=dln2, dln1=(dg1, db1), dng=dng, dlb=dlb, dbf=dbf)
    return dz1, dxm, gr, small


MESH = pl.DeviceIdType.MESH
ANY = pl.BlockSpec(memory_space=pl.ANY)


def _me_and_peers():
    x, y, c = lax.axis_index("x"), lax.axis_index("y"), lax.axis_index("c")
    me = 4 * x + 2 * y + c
    peers = []
    for p in range(1, N_DEV):
        px, py, pc = x ^ ((p >> 2) & 1), y ^ ((p >> 1) & 1), c ^ (p & 1)
        peers.append(((px, py, pc), 4 * px + 2 * py + pc))
    return me, peers


def _all_gather(shard, name, vmem=False):
    R, W = shard.shape

    def body(src_ref, out_ref, send_sems, recv_sems, local_sem):
        me, peers = _me_and_peers()
        mine = pltpu.make_async_copy(src_ref, out_ref.at[me], local_sem)
        mine.start()
        copies = []
        for p, (dev, _) in enumerate(peers):
            cp = pltpu.make_async_remote_copy(src_ref=src_ref, dst_ref=out_ref.at[me], send_sem=send_sems.at[p],
                                              recv_sem=recv_sems.at[p], device_id=dev, device_id_type=MESH)
            cp.start()
            copies.append(cp)
        for p, (dev, idx) in enumerate(peers):
            pltpu.make_async_remote_copy(src_ref=src_ref, dst_ref=out_ref.at[idx], send_sem=send_sems.at[p],
                                         recv_sem=recv_sems.at[p], device_id=dev, device_id_type=MESH).wait_recv()
        for cp in copies:
            cp.wait_send()
        mine.wait()

    spec = pl.BlockSpec(memory_space=pltpu.VMEM) if vmem else ANY
    return pl.pallas_call(
        body,
        name=name,
        in_specs=[spec],
        out_specs=spec,
        out_shape=jax.ShapeDtypeStruct((N_DEV, R, W), shard.dtype),
        scratch_shapes=[pltpu.SemaphoreType.DMA((N_DEV - 1,)), pltpu.SemaphoreType.DMA((N_DEV - 1,)), pltpu.SemaphoreType.DMA],
    )(shard)


def _exchange(chunks, name):
    _, R, W = chunks.shape

    def body(src_ref, out_ref, send_sems, recv_sems, local_sem):
        me, peers = _me_and_peers()
        mine = pltpu.make_async_copy(src_ref.at[me], out_ref.at[me], local_sem)
        mine.start()
        copies = []
        for p, (dev, idx) in enumerate(peers):
            cp = pltpu.make_async_remote_copy(src_ref=src_ref.at[idx], dst_ref=out_ref.at[me], send_sem=send_sems.at[p],
                                              recv_sem=recv_sems.at[p], device_id=dev, device_id_type=MESH)
            cp.start()
            copies.append(cp)
        for p, (dev, idx) in enumerate(peers):
            pltpu.make_async_remote_copy(src_ref=src_ref.at[idx], dst_ref=out_ref.at[idx], send_sem=send_sems.at[p],
                                         recv_sem=recv_sems.at[p], device_id=dev, device_id_type=MESH).wait_recv()
        for cp in copies:
            cp.wait_send()
        mine.wait()

    return pl.pallas_call(
        body,
        name=name,
        in_specs=[ANY],
        out_specs=ANY,
        out_shape=jax.ShapeDtypeStruct(chunks.shape, chunks.dtype),
        scratch_shapes=[pltpu.SemaphoreType.DMA((N_DEV - 1,)), pltpu.SemaphoreType.DMA((N_DEV - 1,)), pltpu.SemaphoreType.DMA],
    )(chunks)


def _sum_adam(parts, w, m, v, name, tr=512):
    _, R, W = parts.shape
    tr = _pick(R, tr) if R % LANE == 0 else R
    c1 = 1.0 / (1.0 - ADAM_B1 ** ADAM_STEP)
    c2 = 1.0 / (1.0 - ADAM_B2 ** ADAM_STEP)

    def body(p_ref, w_ref, m_ref, v_ref, g_ref, d_ref, nm_ref, nv_ref):
        g = p_ref[0]
        for s in range(1, N_DEV):
            g = g + p_ref[s]
        nm = ADAM_B1 * m_ref[...] + (1.0 - ADAM_B1) * g
        nv = ADAM_B2 * v_ref[...] + (1.0 - ADAM_B2) * (g * g)
        g_ref[...] = g
        nm_ref[...] = nm
        nv_ref[...] = nv
        d_ref[...] = -ADAM_LR * ((nm * c1) / (jnp.sqrt(nv * c2) + ADAM_EPS) + ADAM_WD * w_ref[...])

    blk = pl.BlockSpec((tr, W), lambda i: (i, 0))
    return pl.pallas_call(
        body,
        name=name,
        grid=(R // tr,),
        in_specs=[pl.BlockSpec((N_DEV, tr, W), lambda i: (0, i, 0)), blk, blk, blk],
        out_specs=[blk] * 4,
        out_shape=[jax.ShapeDtypeStruct((R, W), F32)] * 4,
        compiler_params=_cparams(("parallel",)),
    )(parts, w, m, v)


PACK_ROWS = 32768
BIG = ("in", "pa", "pb", "o", "ffin", "ffout")
COL_SHARDED = {"in": True, "pa": True, "pb": True, "o": False, "ffin": True, "ffout": False}
SMALL_ROWS = 80


def _pack(parts, rows, lead=()):
    n = len(lead)
    cat = jnp.concatenate([p.reshape(lead + (-1, LANE)) for p in parts], axis=n)
    return jnp.pad(cat, ((0, 0),) * n + ((0, rows - cat.shape[n]), (0, 0)))


def _unpack(packed, shapes, lead=()):
    out, off = [], 0
    for s in shapes:
        r = math.prod(s) // LANE
        out.append(packed[..., off:off + r, :].reshape(lead + tuple(s)))
        off += r
    return out


def _pack_flat(parts, rows):
    flat = jnp.concatenate([p.reshape(-1) for p in parts])
    return jnp.pad(flat, (0, rows * LANE - flat.shape[0])).reshape(rows, LANE)


def _unpack_flat(packed, shapes):
    flat = packed.reshape(-1)
    out, off = [], 0
    for s in shapes:
        n = math.prod(s)
        out.append(flat[off:off + n].reshape(s))
        off += n
    return out


def _join_shards(g, col_sharded):
    if col_sharded:
        return g.transpose(1, 2, 0, 3).reshape(g.shape[1], g.shape[2], N_DEV * g.shape[3])
    return g.transpose(1, 0, 2, 3).reshape(g.shape[1], N_DEV * g.shape[2], g.shape[3])


def _split_shards(full, col_sharded):
    d, a, b = full.shape
    if col_sharded:
        return full.reshape(d, a, N_DEV, b // N_DEV).transpose(2, 0, 1, 3)
    return full.reshape(d, N_DEV, a // N_DEV, b).transpose(1, 0, 2, 3)


def _permute_in(w):
    a, b = QKV_W + A_HEADS, QKV_W + A_HEADS + 4 * B_WIDTH
    return jnp.concatenate([w[:, :QKV_W], w[:, b:], w[:, a:b], w[:, QKV_W:a],
                            jnp.zeros((w.shape[0], LANE - A_HEADS), w.dtype)], axis=1)


def _unpermute_in(d):
    g0, h0, f0 = QKV_W, QKV_W + 2 * D_MODEL, QKV_W + C_FA
    return jnp.concatenate([d[:, :QKV_W], d[:, f0:f0 + A_HEADS], d[:, h0:f0], d[:, g0:h0]], axis=1)


def _lower_bounds(logits):
    sm = jax.nn.softmax(logits.astype(F32), axis=0)
    return jnp.cumsum(sm, axis=0) - sm[0:1]


def kernel(x, w_in, b_fgate, hgrn_lb_logits, hgrn_norm_g, w_branch_a, w_branch_b, w_out, ln1_g, ln1_b, w_ff_in, w_ff_out, ln2_g, ln2_b, loss_target, m_w_in, m_b_fgate, m_hgrn_lb_logits, m_hgrn_norm_g, m_w_branch_a, m_w_branch_b, m_w_out, m_ln1_g, m_ln1_b, m_w_ff_in, m_w_ff_out, m_ln2_g, m_ln2_b, v_w_in, v_b_fgate, v_hgrn_lb_logits, v_hgrn_norm_g, v_w_branch_a, v_w_branch_b, v_w_out, v_ln1_g, v_ln1_b, v_w_ff_in, v_w_ff_out, v_ln2_g, v_ln2_b):
    big = {"in": (w_in, m_w_in, v_w_in), "pa": (w_branch_a, m_w_branch_a, v_w_branch_a), "pb": (w_branch_b, m_w_branch_b, v_w_branch_b),
           "o": (w_out, m_w_out, v_w_out), "ffin": (w_ff_in, m_w_ff_in, v_w_ff_in), "ffout": (w_ff_out, m_w_ff_out, v_w_ff_out)}
    small = [(b_fgate, m_b_fgate, v_b_fgate), (hgrn_lb_logits, m_hgrn_lb_logits, v_hgrn_lb_logits), (hgrn_norm_g, m_hgrn_norm_g, v_hgrn_norm_g),
             (ln1_g, m_ln1_g, v_ln1_g), (ln1_b, m_ln1_b, v_ln1_b), (ln2_g, m_ln2_g, v_ln2_g), (ln2_b, m_ln2_b, v_ln2_b)]
    shard_shapes = [big[n][0].shape for n in BIG]
    small_shapes = [s[0].shape for s in small]

    gathered = _all_gather(_pack([big[n][0].astype(BF16) for n in BIG], PACK_ROWS), "gather_weights")
    full = {}
    for n, g in zip(BIG, _unpack(gathered, shard_shapes, lead=(N_DEV,))):
        full[n] = _join_shards(g, COL_SHARDED[n])
    lbounds = _lower_bounds(hgrn_lb_logits)

    def layer_weights(l):
        w = {n: full[n][l] for n in BIG}
        w["in"] = _permute_in(w["in"])
        return w

    def layer_params(l):
        bf_pad = jnp.pad(b_fgate[l].astype(F32), (0, LANE - A_HEADS)).reshape(1, LANE)
        row = lambda a: a[l].astype(F32).reshape(1, -1)
        return dict(bf_pad=bf_pad, lb=lbounds[l].reshape(1, -1), norm_g=row(hgrn_norm_g), ln1_g=row(ln1_g), ln1_b=row(ln1_b),
                    ln2_g=row(ln2_g), ln2_b=row(ln2_b))

    xl = x[0]
    xlb = xl.astype(BF16)
    saved = []
    for l in range(DEPTH):
        p = layer_params(l)
        xl, xlb, sv = _layer_fwd(xl, xlb, layer_weights(l), p["bf_pad"], p["lb"], p["norm_g"], p["ln1_g"], p["ln1_b"],
                                 p["ln2_g"], p["ln2_b"], l)
        saved.append(sv)

    grads = [None] * DEPTH
    smalls = [None] * DEPTH
    loss_part = None
    ua = ub = None
    for l in reversed(range(DEPTH)):
        p, sv = layer_params(l), saved[l]
        if l == DEPTH - 1:
            dz2, dz2b, dg2, db2, loss_part = _ln_bwd_loss(sv["x1"], sv["ffn"], p["ln2_g"], p["ln2_b"], loss_target[0], f"l{l}_dln2")
        else:
            dz2, dz2b, dg2, db2 = _ln_bwd(sv["x1"], sv["ffn"], p["ln2_g"], ua, ub, f"l{l}_dln2")
        ua, ub, grads[l], smalls[l] = _layer_bwd(sv, layer_weights(l), dz2, dz2b, (dg2, db2), p["bf_pad"], p["lb"], p["norm_g"],
                                                 p["ln1_g"], l)
    (grad_x,) = _rowwise(lambda a, b: ALPHA * a + b, "grad_x", [(ua, 0, D_MODEL), (ub, 0, D_MODEL)], [], [(D_MODEL, F32)])

    contrib = []
    for n in BIG:
        per_layer = [_unpermute_in(grads[l][n]) if n == "in" else grads[l][n] for l in range(DEPTH)]
        contrib.append(_split_shards(jnp.stack(per_layer), COL_SHARDED[n]))
    received = _exchange(_pack(contrib, PACK_ROWS, lead=(N_DEV,)), "exchange_grads")
    wmv = [_pack([big[n][i].astype(F32) for n in BIG], PACK_ROWS) for i in range(3)]
    outs_big = [_unpack(o, shard_shapes) for o in _sum_adam(received, *wmv, "adam_big")]

    fold = lambda a: jnp.sum(a, axis=0)
    dlb = jnp.stack([fold(smalls[l]["dlb"]) for l in range(DEPTH)])
    _, lb_vjp = jax.vjp(_lower_bounds, hgrn_lb_logits)
    small_grads = [jnp.stack([fold(smalls[l]["dbf"])[:A_HEADS] for l in range(DEPTH)]), lb_vjp(dlb)[0],
                   jnp.stack([fold(smalls[l]["dng"]) for l in range(DEPTH)]),
                   jnp.stack([fold(smalls[l]["dln1"][0]) for l in range(DEPTH)]), jnp.stack([fold(smalls[l]["dln1"][1]) for l in range(DEPTH)]),
                   jnp.stack([fold(smalls[l]["dln2"][0]) for l in range(DEPTH)]), jnp.stack([fold(smalls[l]["dln2"][1]) for l in range(DEPTH)])]
    loss_local = (0.5 / D_MODEL) * jnp.sum(loss_part)
    small_all = _all_gather(_pack_flat(small_grads + [loss_local.reshape(1)], SMALL_ROWS), "gather_small", vmem=True)
    one = jnp.zeros((1,), F32)
    swmv = [_pack_flat([s[i].astype(F32) for s in small] + [one], SMALL_ROWS) for i in range(3)]
    outs_small = [_unpack_flat(o, small_shapes + [(1,)]) for o in _sum_adam(small_all, *swmv, "adam_small")]
    loss = outs_small[0][-1].reshape(())

    order = {"in": 0, "pa": 4, "pb": 5, "o": 6, "ffin": 9, "ffout": 10}
    small_pos = [1, 2, 3, 7, 8, 11, 12]
    result = [loss, grad_x.reshape(x.shape)]
    for kind in range(4):
        slot = [None] * 13
        for n, arr in zip(BIG, outs_big[kind]):
            slot[order[n]] = arr
        for pos, arr in zip(small_pos, outs_small[kind][:-1]):
            slot[pos] = arr
        result.extend(slot)
    return tuple(result)
```

```python
import functools
import math

import jax
import jax.numpy as jnp
from jax import lax
from jax.experimental import pallas as pl
from jax.experimental.pallas import tpu as pltpu

F32 = jnp.float32
BF16 = jnp.bfloat16

D_MODEL = 1024
DEPTH = 2
CHUNK = 64
A_HEADS = 8
A_HEAD_DIM = 64
A_WIDTH = 512
B_WIDTH = 512
B_HEADS = 4
B_DK = 128
FFN_HIDDEN = 2816
IN_TOTAL = 5640
ALPHA = (2 * DEPTH) ** 0.25
LN_EPS = 1e-5
RMS_EPS = 1e-6
ADAM_LR = 0.001
ADAM_B1 = 0.9
ADAM_B2 = 0.999
ADAM_EPS = 1e-08
ADAM_WD = 0.01
ADAM_STEP = 10

N_DEV = 8
LANE = 128
VMEM_LIMIT = 48 * 1024 * 1024

QKV_W = 3 * A_WIDTH
C_GA, C_GB2 = 0, D_MODEL
C_QB = 2 * D_MODEL
C_FB, C_IB, C_GB = C_QB + B_WIDTH, C_QB + 2 * B_WIDTH, C_QB + 3 * B_WIDTH
C_FA = C_QB + 4 * B_WIDTH
REST_W = C_FA + LANE
IN_PERM = QKV_W + REST_W


def _cparams(sem):
    return pltpu.CompilerParams(dimension_semantics=sem, vmem_limit_bytes=VMEM_LIMIT)


def _pick(n, pref):
    if n <= pref:
        return n
    t = pref
    while n % t:
        t -= LANE
    return t


def _matmul(a, b, mode, out_dtype, name, *, n_off=0, n_size=None, tm=2048, tn=1024, tk=2816):
    if mode == "nn":
        M, K = a.shape
        N = b.shape[1] if n_size is None else n_size
    elif mode == "nt":
        M, K = a.shape
        N = b.shape[0]
    else:
        K, M = a.shape
        N = b.shape[1]
    tm, tn, tk = _pick(M, tm), _pick(N, tn), _pick(K, tk)
    assert n_off % tn == 0
    noff = n_off // tn
    nk = K // tk

    dn = {"nn": (((1,), (0,)), ((), ())), "nt": (((1,), (1,)), ((), ())), "tn": (((0,), (0,)), ((), ()))}[mode]

    def body(a_ref, b_ref, o_ref, *scratch):
        prod = lax.dot_general(a_ref[...], b_ref[...], dn, preferred_element_type=F32)
        if nk == 1:
            o_ref[...] = prod.astype(o_ref.dtype)
            return
        (acc_ref,) = scratch
        k = pl.program_id(2)

        @pl.when(k == 0)
        def _():
            acc_ref[...] = prod

        @pl.when(k > 0)
        def _():
            acc_ref[...] += prod

        @pl.when(k == nk - 1)
        def _():
            o_ref[...] = acc_ref[...].astype(o_ref.dtype)

    if mode == "nn":
        a_spec = pl.BlockSpec((tm, tk), lambda i, j, k: (i, k))
        b_spec = pl.BlockSpec((tk, tn), lambda i, j, k: (k, j + noff))
    elif mode == "nt":
        a_spec = pl.BlockSpec((tm, tk), lambda i, j, k: (i, k))
        b_spec = pl.BlockSpec((tn, tk), lambda i, j, k: (j, k))
    else:
        a_spec = pl.BlockSpec((tk, tm), lambda i, j, k: (k, i))
        b_spec = pl.BlockSpec((tk, tn), lambda i, j, k: (k, j))
    return pl.pallas_call(
        body,
        name=name,
        grid=(M // tm, N // tn, nk),
        in_specs=[a_spec, b_spec],
        out_specs=pl.BlockSpec((tm, tn), lambda i, j, k: (i, j)),
        out_shape=jax.ShapeDtypeStruct((M, N), out_dtype),
        scratch_shapes=[pltpu.VMEM((tm, tn), F32)] if nk > 1 else [],
        compiler_params=_cparams(("parallel", "parallel", "arbitrary")),
    )(a, b)


def _rowwise(fn, name, tiled, params, outs, sums=(), *, tr=256):
    T = tiled[0][0].shape[0]
    tr = _pick(T, tr)
    nt, npar, no, ns = len(tiled), len(params), len(outs), len(sums)

    def body(*refs):
        ins = [r[...] for r in refs[:nt + npar]]
        res = fn(*ins)
        if not isinstance(res, (tuple, list)):
            res = (res,)
        o_refs = refs[nt + npar:nt + npar + no]
        s_refs = refs[nt + npar + no:]
        for r, v in zip(o_refs, res[:no]):
            r[...] = v.astype(r.dtype)
        if ns:
            @pl.when(pl.program_id(0) == 0)
            def _():
                for r in s_refs:
                    r[...] = jnp.zeros_like(r)
            for r, v in zip(s_refs, res[no:]):
                r[...] += v

    in_specs = []
    args = []
    for arr, off, w in tiled:
        assert off % w == 0 or w == arr.shape[1]
        cb = off // w
        in_specs.append(pl.BlockSpec((tr, w), lambda i, cb=cb: (i, cb)))
        args.append(arr)
    for p in params:
        in_specs.append(pl.BlockSpec(p.shape, lambda i, nd=p.ndim: (0,) * nd))
        args.append(p)
    out_specs = [pl.BlockSpec((tr, w), lambda i: (i, 0)) for w, _ in outs]
    out_shape = [jax.ShapeDtypeStruct((T, w), dt) for w, dt in outs]
    out_specs += [pl.BlockSpec((8, w), lambda i: (0, 0)) for w in sums]
    out_shape += [jax.ShapeDtypeStruct((8, w), F32) for w in sums]
    res = pl.pallas_call(
        body,
        name=name,
        grid=(T // tr,),
        in_specs=in_specs,
        out_specs=out_specs,
        out_shape=out_shape,
        compiler_params=_cparams(("arbitrary",)),
    )(*args)
    return res


def _colsum8(v):
    r, w = v.shape
    return jnp.sum(v.reshape(r // 8, 8, w), axis=0)


NEG_BIG = -1e30
QK_SCALE = A_HEAD_DIM ** -0.5
PAIRS = A_HEADS // 2
QKV_BLOCKS = A_WIDTH // LANE
FOX_TQ = 1024
FOX_TK = 1024
FOX_BWD_TQ = 512
_NT = (((1,), (1,)), ((), ()))
_NN = (((1,), (0,)), ((), ()))
_TN = (((0,), (0,)), ((), ()))


def _first_head(shape):
    return lax.broadcasted_iota(jnp.int32, shape, 1) < A_HEAD_DIM


def _split_heads(a, first):
    return jnp.where(first, a, 0), jnp.where(first, 0, a)


def _visible(i, j, tq, tk, keys_on_rows=False):
    shape = (tk, tq) if keys_on_rows else (tq, tk)
    r = lax.broadcasted_iota(jnp.int32, shape, 0)
    c = lax.broadcasted_iota(jnp.int32, shape, 1)
    return ((c - r) if keys_on_rows else (r - c)) >= j * tk - i * tq


def _lanes01(a0, a1, rows):
    lane = lax.broadcasted_iota(jnp.int32, (rows, LANE), 1)
    return jnp.where(lane == 0, a0, jnp.where(lane == 1, a1, 0.0))


def _fox_fwd(qkv, frow, tq, tk):
    T = qkv.shape[0]
    nbk = T // tk

    def body(q_ref, k_ref, v_ref, fk_ref, o_ref, ob_ref, lse_ref, m_ref, l_ref, acc_ref):
        i = pl.program_id(1)
        first = _first_head((tq, LANE))
        qs = _split_heads(q_ref[...] * QK_SCALE, first)
        m_ref[...] = jnp.full_like(m_ref, NEG_BIG)
        l_ref[...] = jnp.zeros_like(l_ref)
        acc_ref[...] = jnp.zeros_like(acc_ref)

        def step(j, diag):
            rows = pl.ds(pl.multiple_of(j * tk, tk), tk)
            k2, v2 = k_ref[rows, :], v_ref[rows, :]
            for hh in range(2):
                s = lax.dot_general(qs[hh], k2, _NT, preferred_element_type=F32) - fk_ref[hh, j]
                if diag:
                    s = jnp.where(_visible(i, j, tq, tk), s, NEG_BIG)
                m_old = m_ref[hh]
                m_new = jnp.maximum(m_old, jnp.max(s, axis=1, keepdims=True))
                p = jnp.exp(s - m_new[:, :1])
                a = jnp.exp(m_old - m_new)
                l_ref[hh] = a * l_ref[hh] + jnp.sum(p, axis=1, keepdims=True)
                acc_ref[hh] = a * acc_ref[hh] + lax.dot_general(p.astype(BF16), v2, _NN, preferred_element_type=F32)
                m_ref[hh] = m_new

        def off_diag(j, c):
            step(j, False)
            return c

        n_full = lax.shift_right_logical(i * tq, tk.bit_length() - 1)
        lax.fori_loop(0, n_full, off_diag, 0)
        for d in range(max(1, tq // tk)):
            step(n_full + d, True)
        o = jnp.where(first, acc_ref[0] / l_ref[0], acc_ref[1] / l_ref[1])
        o_ref[...] = o
        ob_ref[...] = o.astype(BF16)
        lse_ref[0] = _lanes01((m_ref[0] + jnp.log(l_ref[0]))[:, :1], (m_ref[1] + jnp.log(l_ref[1]))[:, :1], tq)

    col_blk = pl.BlockSpec((tq, LANE), lambda hp, i: (i, hp))
    return pl.pallas_call(
        body,
        name="fox_fwd",
        grid=(PAIRS, T // tq),
        in_specs=[col_blk,
                  pl.BlockSpec((T, LANE), lambda hp, i: (0, QKV_BLOCKS + hp)),
                  pl.BlockSpec((T, LANE), lambda hp, i: (0, 2 * QKV_BLOCKS + hp)),
                  pl.BlockSpec((2, nbk, 1, tk), lambda hp, i: (hp, 0, 0, 0))],
        out_specs=[col_blk, col_blk, pl.BlockSpec((1, tq, LANE), lambda hp, i: (hp, i, 0))],
        out_shape=[jax.ShapeDtypeStruct((T, A_WIDTH), F32), jax.ShapeDtypeStruct((T, A_WIDTH), BF16),
                   jax.ShapeDtypeStruct((PAIRS, T, LANE), F32)],
        scratch_shapes=[pltpu.VMEM((2, tq, LANE), F32)] * 3,
        compiler_params=_cparams(("parallel", "arbitrary")),
    )(qkv, qkv, qkv, frow)


def _fox_delta(do, o):
    T = do.shape[0]
    tr = min(T, FOX_TQ)

    def body(do_ref, o_ref, dl_ref):
        first = _first_head((tr, LANE))
        prod = do_ref[...].astype(F32) * o_ref[...]
        dl_ref[0] = _lanes01(jnp.sum(jnp.where(first, prod, 0.0), axis=1, keepdims=True),
                             jnp.sum(jnp.where(first, 0.0, prod), axis=1, keepdims=True), tr)

    blk = pl.BlockSpec((tr, LANE), lambda hp, i: (i, hp))
    return pl.pallas_call(
        body,
        name="fox_delta",
        grid=(PAIRS, T // tr),
        in_specs=[blk, blk],
        out_specs=pl.BlockSpec((1, tr, LANE), lambda hp, i: (hp, i, 0)),
        out_shape=jax.ShapeDtypeStruct((PAIRS, T, LANE), F32),
        compiler_params=_cparams(("parallel", "parallel")),
    )(do, o)


def _fox_bwd(qkv, do, lse_row, delta_row, fpad, tq, tk):
    T = qkv.shape[0]
    nbq = T // tq

    def body(q_ref, k_ref, v_ref, do_ref, lse_ref, dl_ref, f_ref, dq_ref, dk_ref, dv_ref, df_ref, dfq_ref, dk_acc, dv_acc, df_acc):
        hp = pl.program_id(0)
        j = pl.program_id(1)
        first = _first_head((tk, LANE))
        k2 = k_ref[...]
        ks = _split_heads(k2, first)
        ksq = _split_heads(k2 * QK_SCALE, first)
        vs = _split_heads(v_ref[...], first)
        lane = lax.broadcasted_iota(jnp.int32, (tk, LANE), 1)
        fks = tuple(jnp.sum(jnp.where(lane == 2 * hp + hh, f_ref[...], 0.0), axis=1, keepdims=True) for hh in range(2))
        dk_acc[...] = jnp.zeros_like(dk_acc)
        dv_acc[...] = jnp.zeros_like(dv_acc)
        df_acc[...] = jnp.zeros_like(df_acc)

        @pl.when(j == 0)
        def _():
            dq_ref[...] = jnp.zeros_like(dq_ref)
            dfq_ref[...] = jnp.zeros_like(dfq_ref)

        def step(i, diag):
            rows = pl.ds(pl.multiple_of(i * tq, tq), tq)
            q2 = q_ref[rows, :] * QK_SCALE
            do2 = do_ref[rows, :]
            dq_add = None
            for hh in range(2):
                st = lax.dot_general(ks[hh], q2, _NT, preferred_element_type=F32) - fks[hh]
                pt = jnp.exp(st - lse_ref[hh, i])
                if diag:
                    pt = jnp.where(_visible(i, j, tq, tk, keys_on_rows=True), pt, 0.0)
                dpt = lax.dot_general(vs[hh], do2, _NT, preferred_element_type=F32)
                dst = pt * (dpt - dl_ref[hh, i])
                dsb = dst.astype(BF16)
                dv_acc[hh] += lax.dot_general(pt.astype(BF16), do2, _NN, preferred_element_type=F32)
                dk_acc[hh] += lax.dot_general(dsb, q2, _NN, preferred_element_type=F32)
                df_acc[hh] += jnp.sum(dst, axis=1, keepdims=True)
                dfq_ref[hh, i] += jnp.sum(dst, axis=0, keepdims=True)
                part = lax.dot_general(dsb, ksq[hh], _TN, preferred_element_type=F32)
                dq_add = part if dq_add is None else dq_add + part
            dq_ref[rows, :] += dq_add

        i0 = lax.shift_right_logical(j * tk, tq.bit_length() - 1)
        nd = max(1, tk // tq)
        for d in range(nd):
            step(i0 + d, True)

        def off_diag(i, c):
            step(i, False)
            return c

        lax.fori_loop(i0 + nd, nbq, off_diag, 0)
        dk_ref[...] = jnp.where(first, dk_acc[0], dk_acc[1]).astype(BF16)
        dv_ref[...] = jnp.where(first, dv_acc[0], dv_acc[1]).astype(BF16)
        df_ref[0] = _lanes01(-df_acc[0][:, 0:1], -df_acc[1][:, 0:1], tk)

    whole = lambda cb: pl.BlockSpec((T, LANE), lambda hp, j, cb=cb: (0, cb + hp), pipeline_mode=pl.Buffered(1))
    blk = lambda cb: pl.BlockSpec((tk, LANE), lambda hp, j, cb=cb: (j, cb + hp))
    rows = pl.BlockSpec((2, nbq, 1, tq), lambda hp, j: (hp, 0, 0, 0))
    return pl.pallas_call(
        body,
        name="fox_bwd",
        grid=(PAIRS, T // tk),
        in_specs=[whole(0), blk(QKV_BLOCKS), blk(2 * QKV_BLOCKS), whole(0), rows, rows, pl.BlockSpec((tk, LANE), lambda hp, j: (j, 0))],
        out_specs=[whole(0), blk(0), blk(0), pl.BlockSpec((1, tk, LANE), lambda hp, j: (hp, j, 0)), rows],
        out_shape=[jax.ShapeDtypeStruct((T, A_WIDTH), F32), jax.ShapeDtypeStruct((T, A_WIDTH), BF16), jax.ShapeDtypeStruct((T, A_WIDTH), BF16),
                   jax.ShapeDtypeStruct((PAIRS, T, LANE), F32), jax.ShapeDtypeStruct((A_HEADS, nbq, 1, tq), F32)],
        scratch_shapes=[pltpu.VMEM((2, tk, LANE), F32)] * 3,
        compiler_params=_cparams(("parallel", "arbitrary")),
    )(qkv, qkv, qkv, do, lse_row, delta_row, fpad)


HG_ROWS = 512
_HI = lax.Precision.HIGHEST


def _tri(n, upper):
    r = lax.broadcasted_iota(jnp.int32, (n, n), 0)
    c = lax.broadcasted_iota(jnp.int32, (n, n), 1)
    return jnp.where((r <= c) if upper else (r >= c), 1.0, 0.0).astype(F32)


def _dot(a, b, dn):
    return lax.dot_general(a.astype(BF16), b.astype(BF16), dn, preferred_element_type=F32)


def _hgrn_gates(qb, fb, lb):
    sig = jax.nn.sigmoid(fb)
    f = lb + (1.0 - lb) * sig
    sq = jax.nn.sigmoid(qb)
    return sig, f, sq, qb * sq


def _hgrn_fwd(rest, lb, col_q, col_f, col_i):
    T = rest.shape[0]
    tb = _pick(T, HG_ROWS)
    cpb = tb // CHUNK
    nblk = T // tb

    def body(q_ref, f_ref, i_ref, lb_ref, o_ref, st_ref, s_ref):
        @pl.when(pl.program_id(0) == 0)
        def _():
            s_ref[...] = jnp.zeros_like(s_ref)

        tril = _tri(CHUNK, False)
        mask = tril > 0.5

        def chunk(c, carry):
            rows = pl.ds(pl.multiple_of(c * CHUNK, CHUNK), CHUNK)
            for h in range(B_HEADS):
                cols = slice(h * B_DK, (h + 1) * B_DK)
                _, f, _, q = _hgrn_gates(q_ref[rows, cols], f_ref[rows, cols], lb_ref[:, cols])
                v = i_ref[rows, cols]
                k = 1.0 - f
                b = jnp.dot(tril, jnp.log(f), precision=_HI, preferred_element_type=F32)
                bm = b[CHUNK // 2 - 1:CHUNK // 2, :]
                bl = b[CHUNK - 1:CHUNK, :]
                st = s_ref[h]
                st_ref[h, c] = st
                p = jnp.where(mask, _dot(q * jnp.exp(b - bm), k * jnp.exp(bm - b), _NT), 0.0)
                o_ref[rows, cols] = _dot(p, v, _NN) + _dot(q * jnp.exp(b), st, _NT)
                s_ref[h] = st * jnp.exp(bl) + _dot(v, k * jnp.exp(bl - b), _TN)
            return carry

        lax.fori_loop(0, cpb, chunk, 0)

    col = lambda cb: pl.BlockSpec((tb, B_WIDTH), lambda i, cb=cb: (i, cb))
    return pl.pallas_call(
        body,
        name="hgrn_fwd",
        grid=(nblk,),
        in_specs=[col(col_q // B_WIDTH), col(col_f // B_WIDTH), col(col_i // B_WIDTH), pl.BlockSpec((1, B_WIDTH), lambda i: (0, 0))],
        out_specs=[pl.BlockSpec((tb, B_WIDTH), lambda i: (i, 0)),
                   pl.BlockSpec((B_HEADS, cpb, B_DK, B_DK), lambda i: (0, i, 0, 0))],
        out_shape=[jax.ShapeDtypeStruct((T, B_WIDTH), F32), jax.ShapeDtypeStruct((B_HEADS, T // CHUNK, B_DK, B_DK), F32)],
        scratch_shapes=[pltpu.VMEM((B_HEADS, B_DK, B_DK), F32)],
        compiler_params=_cparams(("arbitrary",)),
    )(rest, rest, rest, lb)


def _hgrn_bwd(rest, lb, states, do, col_q, col_f, col_i):
    T = rest.shape[0]
    tb = _pick(T, HG_ROWS)
    cpb = tb // CHUNK
    nblk = T // tb

    def body(q_ref, f_ref, i_ref, lb_ref, st_ref, do_ref, dq_ref, df_ref, di_ref, dlb_ref, ds_ref):
        @pl.when(pl.program_id(0) == 0)
        def _():
            ds_ref[...] = jnp.zeros_like(ds_ref)
            dlb_ref[...] = jnp.zeros_like(dlb_ref)

        tril = _tri(CHUNK, False)
        triu = _tri(CHUNK, True)
        mask = tril > 0.5

        def chunk(cc, carry):
            c = cpb - 1 - cc
            rows = pl.ds(pl.multiple_of(c * CHUNK, CHUNK), CHUNK)
            for h in range(B_HEADS):
                cols = slice(h * B_DK, (h + 1) * B_DK)
                lbv = lb_ref[:, cols]
                qb = q_ref[rows, cols]
                sig, f, sq, q = _hgrn_gates(qb, f_ref[rows, cols], lbv)
                v = i_ref[rows, cols]
                dov = do_ref[rows, cols]
                k = 1.0 - f
                b = jnp.dot(tril, jnp.log(f), precision=_HI, preferred_element_type=F32)
                bm = b[CHUNK // 2 - 1:CHUNK // 2, :]
                bl = b[CHUNK - 1:CHUNK, :]
                eb, eqm, ekm, ekl, ebl = jnp.exp(b), jnp.exp(b - bm), jnp.exp(bm - b), jnp.exp(bl - b), jnp.exp(bl)
                qt, kt, qa, kd = q * eqm, k * ekm, q * eb, k * ekl
                st0 = st_ref[h, c]
                dst1 = ds_ref[h]
                p = jnp.where(mask, _dot(qt, kt, _NT), 0.0)
                dp = jnp.where(mask, _dot(dov, v, _NT), 0.0)
                dv = _dot(p, dov, _TN) + _dot(kd, dst1, _NT)
                dqt = _dot(dp, kt, _NN)
                dkt = _dot(dp, qt, _TN)
                dqa = _dot(dov, st0, _NN)
                dkd = _dot(v, dst1, _NN)
                ds_ref[h] = _dot(dov, qa, _TN) + dst1 * ebl
                dq = dqt * eqm + dqa * eb
                dk = dkt * ekm + dkd * ekl
                rnd = lambda a: a.astype(BF16).astype(F32)
                db = dqt * rnd(qt) - dkt * rnd(kt) + dqa * qa - dkd * kd
                last = jnp.sum(dkd * kd, axis=0, keepdims=True) + ebl * jnp.sum(dst1 * st0, axis=0, keepdims=True)
                dg = jnp.dot(triu, db, precision=_HI, preferred_element_type=F32) + last
                dfv = dg / f - dk
                dq_ref[rows, cols] = (dq * (sq * (1.0 + qb * (1.0 - sq)))).astype(BF16)
                df_ref[rows, cols] = (dfv * (1.0 - lbv) * sig * (1.0 - sig)).astype(BF16)
                di_ref[rows, cols] = dv.astype(BF16)
                dlb_ref[0:1, cols] += jnp.sum(dfv * (1.0 - sig), axis=0, keepdims=True)
            return carry

        lax.fori_loop(0, cpb, chunk, 0)

    rev = lambda i: nblk - 1 - i
    col = lambda cb: pl.BlockSpec((tb, B_WIDTH), lambda i, cb=cb: (rev(i), cb))
    return pl.pallas_call(
        body,
        name="hgrn_bwd",
        grid=(nblk,),
        in_specs=[col(col_q // B_WIDTH), col(col_f // B_WIDTH), col(col_i // B_WIDTH), pl.BlockSpec((1, B_WIDTH), lambda i: (0, 0)),
                  pl.BlockSpec((B_HEADS, cpb, B_DK, B_DK), lambda i: (0, rev(i), 0, 0)), col(0)],
        out_specs=[col(0), col(0), col(0), pl.BlockSpec((8, B_WIDTH), lambda i: (0, 0))],
        out_shape=[jax.ShapeDtypeStruct((T, B_WIDTH), BF16)] * 3 + [jax.ShapeDtypeStruct((8, B_WIDTH), F32)],
        scratch_shapes=[pltpu.VMEM((B_HEADS, B_DK, B_DK), F32)],
        compiler_params=_cparams(("arbitrary",)),
    )(rest, rest, rest, lb, states, do)


SCAN_ROWS = 256


def _log_sigmoid(x):
    return jnp.minimum(x, 0.0) - jnp.log(1.0 + jnp.exp(-jnp.abs(x)))


def _head_cols(shape):
    return lax.broadcasted_iota(jnp.int32, shape, 1) < A_HEADS


def _fgate_fwd(rest, bf_pad):
    T = rest.shape[0]
    tr = _pick(T, SCAN_ROWS)

    def body(fa_ref, bf_ref, f_ref, carry_ref):
        @pl.when(pl.program_id(0) == 0)
        def _():
            carry_ref[...] = jnp.zeros_like(carry_ref)

        lf = jnp.where(_head_cols((tr, LANE)), _log_sigmoid(fa_ref[...] + bf_ref[...]), 0.0)
        f = jnp.dot(_tri(tr, False), lf, precision=_HI, preferred_element_type=F32) + carry_ref[...]
        f_ref[...] = f
        carry_ref[...] = f[tr - 1:tr, :]

    return pl.pallas_call(
        body,
        name="fgate_fwd",
        grid=(T // tr,),
        in_specs=[pl.BlockSpec((tr, LANE), lambda i: (i, C_FA // LANE)), pl.BlockSpec((1, LANE), lambda i: (0, 0))],
        out_specs=pl.BlockSpec((tr, LANE), lambda i: (i, 0)),
        out_shape=jax.ShapeDtypeStruct((T, LANE), F32),
        scratch_shapes=[pltpu.VMEM((1, LANE), F32)],
        compiler_params=_cparams(("arbitrary",)),
    )(rest, bf_pad)


def _fgate_bwd(dfq, dfk, rest, bf_pad):
    T = rest.shape[0]
    tr = _pick(T, SCAN_ROWS)
    nblk = T // tr

    def body(dq_ref, dk_ref, fa_ref, bf_ref, dfa_ref, db_ref, carry_ref):
        @pl.when(pl.program_id(0) == 0)
        def _():
            carry_ref[...] = jnp.zeros_like(carry_ref)
            db_ref[...] = jnp.zeros_like(db_ref)

        df = dq_ref[...] + dk_ref[0]
        for hp in range(1, PAIRS):
            df = df + pltpu.roll(dk_ref[hp], 2 * hp, 1)
        dlf = jnp.dot(_tri(tr, True), df, precision=_HI, preferred_element_type=F32) + carry_ref[...]
        carry_ref[...] = dlf[0:1, :]
        dfa = jnp.where(_head_cols((tr, LANE)), dlf * jax.nn.sigmoid(-(fa_ref[...] + bf_ref[...])), 0.0)
        dfa_ref[...] = dfa.astype(BF16)
        db_ref[...] += _colsum8(dfa)

    rev = lambda i: nblk - 1 - i
    return pl.pallas_call(
        body,
        name="fgate_bwd",
        grid=(nblk,),
        in_specs=[pl.BlockSpec((tr, LANE), lambda i: (rev(i), 0)), pl.BlockSpec((PAIRS, tr, LANE), lambda i: (0, rev(i), 0)),
                  pl.BlockSpec((tr, LANE), lambda i: (rev(i), C_FA // LANE)), pl.BlockSpec((1, LANE), lambda i: (0, 0))],
        out_specs=[pl.BlockSpec((tr, LANE), lambda i: (rev(i), 0)), pl.BlockSpec((8, LANE), lambda i: (0, 0))],
        out_shape=[jax.ShapeDtypeStruct((T, LANE), BF16), jax.ShapeDtypeStruct((8, LANE), F32)],
        scratch_shapes=[pltpu.VMEM((1, LANE), F32)],
        compiler_params=_cparams(("arbitrary",)),
    )(dfq, dfk, rest, bf_pad)


def _head_rows(a, t):
    return a[:, :A_HEADS].T.reshape(A_HEADS, a.shape[0] // t, 1, t)


def _pair_rows(a, t):
    return a[:, :, :2].transpose(0, 2, 1).reshape(A_HEADS, a.shape[1] // t, 1, t)


def _ln_stats(z):
    mu = jnp.mean(z, axis=-1, keepdims=True)
    zc = z - mu
    rstd = lax.rsqrt(jnp.mean(zc * zc, axis=-1, keepdims=True) + LN_EPS)
    return zc * rstd, rstd


def _ln_fwd(x, y, g, b, name):
    def fn(xv, yv, gv, bv):
        xhat, _ = _ln_stats(ALPHA * xv + yv)
        out = xhat * gv + bv
        return out, out
    return _rowwise(fn, name, [(x, 0, D_MODEL), (y, 0, D_MODEL)], [g, b], [(D_MODEL, F32), (D_MODEL, BF16)])


def _ln_bwd_core(xv, yv, gv, dy):
    xhat, rstd = _ln_stats(ALPHA * xv + yv)
    dxh = dy * gv
    dz = rstd * (dxh - jnp.mean(dxh, axis=-1, keepdims=True) - xhat * jnp.mean(dxh * xhat, axis=-1, keepdims=True))
    return dz, _colsum8(dy * xhat), _colsum8(dy)


def _ln_bwd(x, y, g, ua, ub, name):
    def fn(xv, yv, uav, ubv, gv):
        dz, dg, db = _ln_bwd_core(xv, yv, gv, ALPHA * uav + ubv)
        return dz, dz, dg, db
    return _rowwise(fn, name, [(x, 0, D_MODEL), (y, 0, D_MODEL), (ua, 0, D_MODEL), (ub, 0, D_MODEL)], [g],
                    [(D_MODEL, F32), (D_MODEL, BF16)], [D_MODEL, D_MODEL])


def _ln_bwd_loss(x, y, g, b, target, name):
    def fn(xv, yv, tv, gv, bv):
        xhat, _ = _ln_stats(ALPHA * xv + yv)
        err = xhat * gv + bv - tv
        dz, dg, db = _ln_bwd_core(xv, yv, gv, err * (1.0 / D_MODEL))
        return dz, dz, dg, db, _colsum8(err * err)
    return _rowwise(fn, name, [(x, 0, D_MODEL), (y, 0, D_MODEL), (target, 0, D_MODEL)], [g, b],
                    [(D_MODEL, F32), (D_MODEL, BF16)], [D_MODEL, D_MODEL, D_MODEL])


def _silu_parts(g):
    s = jax.nn.sigmoid(g)
    return g * s, s * (1.0 + g * (1.0 - s))


def _layer_fwd(x, xb, w, bf_pad, lb, norm_g, ln1_g, ln1_b, ln2_g, ln2_b, l):
    T = x.shape[0]
    tq, tk = min(T, FOX_TQ), min(T, FOX_TK)
    sv = {}
    qkv = _matmul(xb, w["in"], "nn", BF16, f"l{l}_qkv", n_off=0, n_size=QKV_W, tn=768)
    rest = _matmul(xb, w["in"], "nn", F32, f"l{l}_rest", n_off=QKV_W, n_size=REST_W, tn=384)
    fcum = _fgate_fwd(rest, bf_pad)
    frow = _head_rows(fcum, tk)
    o_a, ya, lse = _fox_fwd(qkv, frow, tq, tk)
    o_b, states = _hgrn_fwd(rest, lb, C_QB, C_FB, C_IB)

    def yb_fn(ov, gbv, ngv):
        outs = []
        for h in range(B_HEADS):
            oh = ov[:, h * B_DK:(h + 1) * B_DK]
            r = lax.rsqrt(jnp.mean(oh * oh, axis=-1, keepdims=True) + RMS_EPS)
            outs.append(oh * r * ngv)
        return jnp.concatenate(outs, axis=1) * jax.nn.sigmoid(gbv)
    (yb,) = _rowwise(yb_fn, f"l{l}_yb", [(o_b, 0, B_WIDTH), (rest, C_GB, B_WIDTH)], [norm_g], [(B_WIDTH, BF16)])
    pa = _matmul(ya, w["pa"], "nn", F32, f"l{l}_pa")
    pb = _matmul(yb, w["pb"], "nn", F32, f"l{l}_pb")

    def merge_fn(gav, gbv, pav, pbv):
        return jax.nn.sigmoid(gav) * pav + jax.nn.sigmoid(gbv) * pbv
    (merged,) = _rowwise(merge_fn, f"l{l}_merge", [(rest, C_GA, D_MODEL), (rest, C_GB2, D_MODEL), (pa, 0, D_MODEL), (pb, 0, D_MODEL)],
                         [], [(D_MODEL, BF16)])
    mix = _matmul(merged, w["o"], "nn", F32, f"l{l}_mix")
    x1, x1b = _ln_fwd(x, mix, ln1_g, ln1_b, f"l{l}_ln1")
    hid = _matmul(x1b, w["ffin"], "nn", F32, f"l{l}_ffin", tn=512)

    def act_fn(uv, gv):
        return _silu_parts(gv)[0] * uv
    (act,) = _rowwise(act_fn, f"l{l}_act", [(hid, 0, FFN_HIDDEN), (hid, FFN_HIDDEN, FFN_HIDDEN)], [], [(FFN_HIDDEN, BF16)], tr=128)
    ffn = _matmul(act, w["ffout"], "nn", F32, f"l{l}_ffout", tm=1024)
    x2, x2b = _ln_fwd(x1, ffn, ln2_g, ln2_b, f"l{l}_ln2")
    sv.update(x=x, xb=xb, rest=rest, qkv=qkv, fcum=fcum, frow=frow, o_a=o_a, lse=lse, ya=ya, o_b=o_b, states=states,
              yb=yb, pa=pa, pb=pb, merged=merged, mix=mix, x1=x1, x1b=x1b, hid=hid, act=act, ffn=ffn)
    return x2, x2b, sv


def _layer_bwd(sv, w, dz2, dz2b, dln2, bf_pad, lb, norm_g, ln1_g, l):
    T = sv["x"].shape[0]
    tq, tk = min(T, FOX_BWD_TQ), min(T, FOX_TK)
    gr = {}
    dact = _matmul(dz2b, w["ffout"], "nt", F32, f"l{l}_dact", tm=1024, tn=1408)
    gr["ffout"] = _matmul(sv["act"], dz2b, "tn", F32, f"l{l}_dwffout", tm=1408, tn=512, tk=2048)

    def dhid_fn(dav, uv, gv):
        sg, dsg = _silu_parts(gv)
        return jnp.concatenate([dav * sg, dav * uv * dsg], axis=1)
    (dhid,) = _rowwise(dhid_fn, f"l{l}_dhid", [(dact, 0, FFN_HIDDEN), (sv["hid"], 0, FFN_HIDDEN), (sv["hid"], FFN_HIDDEN, FFN_HIDDEN)],
                       [], [(2 * FFN_HIDDEN, BF16)], tr=128)
    dx1f = _matmul(dhid, w["ffin"], "nt", F32, f"l{l}_dx1f", tm=1024)
    gr["ffin"] = _matmul(sv["x1b"], dhid, "tn", F32, f"l{l}_dwffin", tm=1024, tn=512, tk=4096)
    dz1, dz1b, dg1, db1 = _ln_bwd(sv["x"], sv["mix"], ln1_g, dz2, dx1f, f"l{l}_dln1")
    dmerged = _matmul(dz1b, w["o"], "nt", F32, f"l{l}_dmerged")
    gr["o"] = _matmul(sv["merged"], dz1b, "tn", F32, f"l{l}_dwo", tm=1024, tn=512, tk=4096)

    def dmerge_fn(dmv, gav, gbv, pav, pbv):
        sa, sb = jax.nn.sigmoid(gav), jax.nn.sigmoid(gbv)
        return dmv * sa, dmv * sb, dmv * pav * sa * (1.0 - sa), dmv * pbv * sb * (1.0 - sb)
    dpa, dpb, dga, dgb2 = _rowwise(
        dmerge_fn, f"l{l}_dmerge",
        [(dmerged, 0, D_MODEL), (sv["rest"], C_GA, D_MODEL), (sv["rest"], C_GB2, D_MODEL), (sv["pa"], 0, D_MODEL), (sv["pb"], 0, D_MODEL)],
        [], [(D_MODEL, BF16)] * 4)
    dya = _matmul(dpa, w["pa"], "nt", BF16, f"l{l}_dya", tn=512)
    gr["pa"] = _matmul(sv["ya"], dpa, "tn", F32, f"l{l}_dwpa", tm=512, tn=512, tk=4096)
    dyb = _matmul(dpb, w["pb"], "nt", F32, f"l{l}_dyb", tn=512)
    gr["pb"] = _matmul(sv["yb"], dpb, "tn", F32, f"l{l}_dwpb", tm=512, tn=512, tk=4096)

    def dyb_fn(dyv, ov, gbv, ngv):
        sg = jax.nn.sigmoid(gbv)
        dos, dgs = [], []
        dng = jnp.zeros((8, B_DK), F32)
        for h in range(B_HEADS):
            sl = slice(h * B_DK, (h + 1) * B_DK)
            oh, dyh, sgh = ov[:, sl], dyv[:, sl], sg[:, sl]
            r = lax.rsqrt(jnp.mean(oh * oh, axis=-1, keepdims=True) + RMS_EPS)
            n = oh * r
            dyn = dyh * sgh
            dgs.append(dyh * n * ngv * sgh * (1.0 - sgh))
            dng = dng + _colsum8(dyn * n)
            dn = dyn * ngv
            dos.append(r * (dn - n * jnp.mean(dn * n, axis=-1, keepdims=True)))
        return jnp.concatenate(dos, axis=1), jnp.concatenate(dgs, axis=1), dng
    do_b, dgb, dng = _rowwise(dyb_fn, f"l{l}_dyb2", [(dyb, 0, B_WIDTH), (sv["o_b"], 0, B_WIDTH), (sv["rest"], C_GB, B_WIDTH)], [norm_g],
                              [(B_WIDTH, F32), (B_WIDTH, BF16)], [B_DK])
    dqb, dfb, dib, dlb = _hgrn_bwd(sv["rest"], lb, sv["states"], do_b, C_QB, C_FB, C_IB)
    delta = _fox_delta(dya, sv["o_a"])
    dq, dk, dv, dfk, dfq = _fox_bwd(sv["qkv"], dya, _pair_rows(sv["lse"], tq), _pair_rows(delta, tq), sv["fcum"], tq, tk)
    dfq_pad = jnp.pad(dfq.reshape(A_HEADS, T).T, ((0, 0), (0, LANE - A_HEADS)))
    dfa, dbf = _fgate_bwd(dfq_pad, dfk, sv["rest"], bf_pad)
    dproj = jnp.concatenate([dq.astype(BF16), dk, dv, dga, dgb2, dqb, dfb, dib, dgb, dfa], axis=1)
    dxm = _matmul(dproj, w["in"], "nt", F32, f"l{l}_dxm", tm=1024, tk=1920)
    gr["in"] = _matmul(sv["xb"], dproj, "tn", F32, f"l{l}_dwin", tm=1024, tn=640, tk=4096)
    small = dict(dln2=dln2, dln1=(dg1, db1), dng=dng, dlb=dlb, dbf=dbf)
    return dz1, dxm, gr, small


MESH = pl.DeviceIdType.MESH
ANY = pl.BlockSpec(memory_space=pl.ANY)


def _me_and_peers():
    x, y, c = lax.axis_index("x"), lax.axis_index("y"), lax.axis_index("c")
    me = 4 * x + 2 * y + c
    peers = []
    for p in range(1, N_DEV):
        px, py, pc = x ^ ((p >> 2) & 1), y ^ ((p >> 1) & 1), c ^ (p & 1)
        peers.append(((px, py, pc), 4 * px + 2 * py + pc))
    return me, peers


def _all_gather(shard, name, vmem=False):
    R, W = shard.shape

    def body(src_ref, out_ref, send_sems, recv_sems, local_sem):
        me, peers = _me_and_peers()
        mine = pltpu.make_async_copy(src_ref, out_ref.at[me], local_sem)
        mine.start()
        copies = []
        for p, (dev, _) in enumerate(peers):
            cp = pltpu.make_async_remote_copy(src_ref=src_ref, dst_ref=out_ref.at[me], send_sem=send_sems.at[p],
                                              recv_sem=recv_sems.at[p], device_id=dev, device_id_type=MESH)
            cp.start()
            copies.append(cp)
        for p, (dev, idx) in enumerate(peers):
            pltpu.make_async_remote_copy(src_ref=src_ref, dst_ref=out_ref.at[idx], send_sem=send_sems.at[p],
                                         recv_sem=recv_sems.at[p], device_id=dev, device_id_type=MESH).wait_recv()
        for cp in copies:
            cp.wait_send()
        mine.wait()

    spec = pl.BlockSpec(memory_space=pltpu.VMEM) if vmem else ANY
    return pl.pallas_call(
        body,
        name=name,
        in_specs=[spec],
        out_specs=spec,
        out_shape=jax.ShapeDtypeStruct((N_DEV, R, W), shard.dtype),
        scratch_shapes=[pltpu.SemaphoreType.DMA((N_DEV - 1,)), pltpu.SemaphoreType.DMA((N_DEV - 1,)), pltpu.SemaphoreType.DMA],
    )(shard)


def _exchange(chunks, name):
    _, R, W = chunks.shape

    def body(src_ref, out_ref, send_sems, recv_sems, local_sem):
        me, peers = _me_and_peers()
        mine = pltpu.make_async_copy(src_ref.at[me], out_ref.at[me], local_sem)
        mine.start()
        copies = []
        for p, (dev, idx) in enumerate(peers):
            cp = pltpu.make_async_remote_copy(src_ref=src_ref.at[idx], dst_ref=out_ref.at[me], send_sem=send_sems.at[p],
                                              recv_sem=recv_sems.at[p], device_id=dev, device_id_type=MESH)
            cp.start()
            copies.append(cp)
        for p, (dev, idx) in enumerate(peers):
            pltpu.make_async_remote_copy(src_ref=src_ref.at[idx], dst_ref=out_ref.at[idx], send_sem=send_sems.at[p],
                                         recv_sem=recv_sems.at[p], device_id=dev, device_id_type=MESH).wait_recv()
        for cp in copies:
            cp.wait_send()
        mine.wait()

    return pl.pallas_call(
        body,
        name=name,
        in_specs=[ANY],
        out_specs=ANY,
        out_shape=jax.ShapeDtypeStruct(chunks.shape, chunks.dtype),
        scratch_shapes=[pltpu.SemaphoreType.DMA((N_DEV - 1,)), pltpu.SemaphoreType.DMA((N_DEV - 1,)), pltpu.SemaphoreType.DMA],
    )(chunks)


def _sum_adam(parts, w, m, v, name, tr=512):
    _, R, W = parts.shape
    tr = _pick(R, tr) if R % LANE == 0 else R
    c1 = 1.0 / (1.0 - ADAM_B1 ** ADAM_STEP)
    c2 = 1.0 / (1.0 - ADAM_B2 ** ADAM_STEP)

    def body(p_ref, w_ref, m_ref, v_ref, g_ref, d_ref, nm_ref, nv_ref):
        g = p_ref[0]
        for s in range(1, N_DEV):
            g = g + p_ref[s]
        nm = ADAM_B1 * m_ref[...] + (1.0 - ADAM_B1) * g
        nv = ADAM_B2 * v_ref[...] + (1.0 - ADAM_B2) * (g * g)
        g_ref[...] = g
        nm_ref[...] = nm
        nv_ref[...] = nv
        d_ref[...] = -ADAM_LR * ((nm * c1) / (jnp.sqrt(nv * c2) + ADAM_EPS) + ADAM_WD * w_ref[...])

    blk = pl.BlockSpec((tr, W), lambda i: (i, 0))
    return pl.pallas_call(
        body,
        name=name,
        grid=(R // tr,),
        in_specs=[pl.BlockSpec((N_DEV, tr, W), lambda i: (0, i, 0)), blk, blk, blk],
        out_specs=[blk] * 4,
        out_shape=[jax.ShapeDtypeStruct((R, W), F32)] * 4,
        compiler_params=_cparams(("parallel",)),
    )(parts, w, m, v)


PACK_ROWS = 32768
BIG = ("in", "pa", "pb", "o", "ffin", "ffout")
COL_SHARDED = {"in": True, "pa": True, "pb": True, "o": False, "ffin": True, "ffout": False}
SMALL_ROWS = 80


def _pack(parts, rows, lead=()):
    n = len(lead)
    cat = jnp.concatenate([p.reshape(lead + (-1, LANE)) for p in parts], axis=n)
    return jnp.pad(cat, ((0, 0),) * n + ((0, rows - cat.shape[n]), (0, 0)))


def _unpack(packed, shapes, lead=()):
    out, off = [], 0
    for s in shapes:
        r = math.prod(s) // LANE
        out.append(packed[..., off:off + r, :].reshape(lead + tuple(s)))
        off += r
    return out


def _pack_flat(parts, rows):
    flat = jnp.concatenate([p.reshape(-1) for p in parts])
    return jnp.pad(flat, (0, rows * LANE - flat.shape[0])).reshape(rows, LANE)


def _unpack_flat(packed, shapes):
    flat = packed.reshape(-1)
    out, off = [], 0
    for s in shapes:
        n = math.prod(s)
        out.append(flat[off:off + n].reshape(s))
        off += n
    return out


def _join_shards(g, col_sharded):
    if col_sharded:
        return g.transpose(1, 2, 0, 3).reshape(g.shape[1], g.shape[2], N_DEV * g.shape[3])
    return g.transpose(1, 0, 2, 3).reshape(g.shape[1], N_DEV * g.shape[2], g.shape[3])


def _split_shards(full, col_sharded):
    d, a, b = full.shape
    if col_sharded:
        return full.reshape(d, a, N_DEV, b // N_DEV).transpose(2, 0, 1, 3)
    return full.reshape(d, N_DEV, a // N_DEV, b).transpose(1, 0, 2, 3)


def _permute_in(w):
    a, b = QKV_W + A_HEADS, QKV_W + A_HEADS + 4 * B_WIDTH
    return jnp.concatenate([w[:, :QKV_W], w[:, b:], w[:, a:b], w[:, QKV_W:a],
                            jnp.zeros((w.shape[0], LANE - A_HEADS), w.dtype)], axis=1)


def _unpermute_in(d):
    g0, h0, f0 = QKV_W, QKV_W + 2 * D_MODEL, QKV_W + C_FA
    return jnp.concatenate([d[:, :QKV_W], d[:, f0:f0 + A_HEADS], d[:, h0:f0], d[:, g0:h0]], axis=1)


def _lower_bounds(logits):
    sm = jax.nn.softmax(logits.astype(F32), axis=0)
    return jnp.cumsum(sm, axis=0) - sm[0:1]


def kernel(x, w_in, b_fgate, hgrn_lb_logits, hgrn_norm_g, w_branch_a, w_branch_b, w_out, ln1_g, ln1_b, w_ff_in, w_ff_out, ln2_g, ln2_b, loss_target, m_w_in, m_b_fgate, m_hgrn_lb_logits, m_hgrn_norm_g, m_w_branch_a, m_w_branch_b, m_w_out, m_ln1_g, m_ln1_b, m_w_ff_in, m_w_ff_out, m_ln2_g, m_ln2_b, v_w_in, v_b_fgate, v_hgrn_lb_logits, v_hgrn_norm_g, v_w_branch_a, v_w_branch_b, v_w_out, v_ln1_g, v_ln1_b, v_w_ff_in, v_w_ff_out, v_ln2_g, v_ln2_b):
    big = {"in": (w_in, m_w_in, v_w_in), "pa": (w_branch_a, m_w_branch_a, v_w_branch_a), "pb": (w_branch_b, m_w_branch_b, v_w_branch_b),
           "o": (w_out, m_w_out, v_w_out), "ffin": (w_ff_in, m_w_ff_in, v_w_ff_in), "ffout": (w_ff_out, m_w_ff_out, v_w_ff_out)}
    small = [(b_fgate, m_b_fgate, v_b_fgate), (hgrn_lb_logits, m_hgrn_lb_logits, v_hgrn_lb_logits), (hgrn_norm_g, m_hgrn_norm_g, v_hgrn_norm_g),
             (ln1_g, m_ln1_g, v_ln1_g), (ln1_b, m_ln1_b, v_ln1_b), (ln2_g, m_ln2_g, v_ln2_g), (ln2_b, m_ln2_b, v_ln2_b)]
    shard_shapes = [big[n][0].shape for n in BIG]
    small_shapes = [s[0].shape for s in small]

    gathered = _all_gather(_pack([big[n][0].astype(BF16) for n in BIG], PACK_ROWS), "gather_weights")
    full = {}
    for n, g in zip(BIG, _unpack(gathered, shard_shapes, lead=(N_DEV,))):
        full[n] = _join_shards(g, COL_SHARDED[n])
    lbounds = _lower_bounds(hgrn_lb_logits)

    def layer_weights(l):
        w = {n: full[n][l] for n in BIG}
        w["in"] = _permute_in(w["in"])
        return w

    def layer_params(l):
        bf_pad = jnp.pad(b_fgate[l].astype(F32), (0, LANE - A_HEADS)).reshape(1, LANE)
        row = lambda a: a[l].astype(F32).reshape(1, -1)
        return dict(bf_pad=bf_pad, lb=lbounds[l].reshape(1, -1), norm_g=row(hgrn_norm_g), ln1_g=row(ln1_g), ln1_b=row(ln1_b),
                    ln2_g=row(ln2_g), ln2_b=row(ln2_b))

    xl = x[0]
    xlb = xl.astype(BF16)
    saved = []
    for l in range(DEPTH):
        p = layer_params(l)
        xl, xlb, sv = _layer_fwd(xl, xlb, layer_weights(l), p["bf_pad"], p["lb"], p["norm_g"], p["ln1_g"], p["ln1_b"],
                                 p["ln2_g"], p["ln2_b"], l)
        saved.append(sv)

    grads = [None] * DEPTH
    smalls = [None] * DEPTH
    loss_part = None
    ua = ub = None
    for l in reversed(range(DEPTH)):
        p, sv = layer_params(l), saved[l]
        if l == DEPTH - 1:
            dz2, dz2b, dg2, db2, loss_part = _ln_bwd_loss(sv["x1"], sv["ffn"], p["ln2_g"], p["ln2_b"], loss_target[0], f"l{l}_dln2")
        else:
            dz2, dz2b, dg2, db2 = _ln_bwd(sv["x1"], sv["ffn"], p["ln2_g"], ua, ub, f"l{l}_dln2")
        ua, ub, grads[l], smalls[l] = _layer_bwd(sv, layer_weights(l), dz2, dz2b, (dg2, db2), p["bf_pad"], p["lb"], p["norm_g"],
                                                 p["ln1_g"], l)
    (grad_x,) = _rowwise(lambda a, b: ALPHA * a + b, "grad_x", [(ua, 0, D_MODEL), (ub, 0, D_MODEL)], [], [(D_MODEL, F32)])

    contrib = []
    for n in BIG:
        per_layer = [_unpermute_in(grads[l][n]) if n == "in" else grads[l][n] for l in range(DEPTH)]
        contrib.append(_split_shards(jnp.stack(per_layer), COL_SHARDED[n]))
    received = _exchange(_pack(contrib, PACK_ROWS, lead=(N_DEV,)), "exchange_grads")
    wmv = [_pack([big[n][i].astype(F32) for n in BIG], PACK_ROWS) for i in range(3)]
    outs_big = [_unpack(o, shard_shapes) for o in _sum_adam(received, *wmv, "adam_big")]

    fold = lambda a: jnp.sum(a, axis=0)
    dlb = jnp.stack([fold(smalls[l]["dlb"]) for l in range(DEPTH)])
    _, lb_vjp = jax.vjp(_lower_bounds, hgrn_lb_logits)
    small_grads = [jnp.stack([fold(smalls[l]["dbf"])[:A_HEADS] for l in range(DEPTH)]), lb_vjp(dlb)[0],
                   jnp.stack([fold(smalls[l]["dng"]) for l in range(DEPTH)]),
                   jnp.stack([fold(smalls[l]["dln1"][0]) for l in range(DEPTH)]), jnp.stack([fold(smalls[l]["dln1"][1]) for l in range(DEPTH)]),
                   jnp.stack([fold(smalls[l]["dln2"][0]) for l in range(DEPTH)]), jnp.stack([fold(smalls[l]["dln2"][1]) for l in range(DEPTH)])]
    loss_local = (0.5 / D_MODEL) * jnp.sum(loss_part)
    small_all = _all_gather(_pack_flat(small_grads + [loss_local.reshape(1)], SMALL_ROWS), "gather_small", vmem=True)
    one = jnp.zeros((1,), F32)
    swmv = [_pack_flat([s[i].astype(F32) for s in small] + [one], SMALL_ROWS) for i in range(3)]
    outs_small = [_unpack_flat(o, small_shapes + [(1,)]) for o in _sum_adam(small_all, *swmv, "adam_small")]
    loss = outs_small[0][-1].reshape(())

    order = {"in": 0, "pa": 4, "pb": 5, "o": 6, "ffin": 9, "ffout": 10}
    small_pos = [1, 2, 3, 7, 8, 11, 12]
    result = [loss, grad_x.reshape(x.shape)]
    for kind in range(4):
        slot = [None] * 13
        for n, arr in zip(BIG, outs_big[kind]):
            slot[order[n]] = arr
        for pos, arr in zip(small_pos, outs_small[kind][:-1]):
            slot[pos] = arr
        result.extend(slot)
    return tuple(result)
```

```python
import functools
import math

import jax
import jax.numpy as jnp
from jax import lax
from jax.experimental import pallas as pl
from jax.experimental.pallas import tpu as pltpu

F32 = jnp.float32
BF16 = jnp.bfloat16

D_MODEL = 1024
DEPTH = 2
CHUNK = 64
A_HEADS = 8
A_HEAD_DIM = 64
A_WIDTH = 512
B_WIDTH = 512
B_HEADS = 4
B_DK = 128
FFN_HIDDEN = 2816
IN_TOTAL = 5640
ALPHA = (2 * DEPTH) ** 0.25
LN_EPS = 1e-5
RMS_EPS = 1e-6
ADAM_LR = 0.001
ADAM_B1 = 0.9
ADAM_B2 = 0.999
ADAM_EPS = 1e-08
ADAM_WD = 0.01
ADAM_STEP = 10

N_DEV = 8
LANE = 128
VMEM_LIMIT = 48 * 1024 * 1024
ADAM_ROWS = 512

QKV_W = 3 * A_WIDTH
C_GA, C_GB2 = 0, D_MODEL
C_QB = 2 * D_MODEL
C_FB, C_IB, C_GB = C_QB + B_WIDTH, C_QB + 2 * B_WIDTH, C_QB + 3 * B_WIDTH
C_FA = C_QB + 4 * B_WIDTH
REST_W = C_FA + LANE
IN_PERM = QKV_W + REST_W


def _cparams(sem):
    return pltpu.CompilerParams(dimension_semantics=sem, vmem_limit_bytes=VMEM_LIMIT)


def _pick(n, pref):
    if n <= pref:
        return n
    t = pref
    while n % t:
        t -= LANE
    return t


def _matmul(a, b, mode, out_dtype, name, *, n_off=0, n_size=None, tm=2048, tn=1024, tk=2816):
    if mode == "nn":
        M, K = a.shape
        N = b.shape[1] if n_size is None else n_size
    elif mode == "nt":
        M, K = a.shape
        N = b.shape[0]
    else:
        K, M = a.shape
        N = b.shape[1]
    tm, tn, tk = _pick(M, tm), _pick(N, tn), _pick(K, tk)
    assert n_off % tn == 0
    noff = n_off // tn
    nk = K // tk

    dn = {"nn": (((1,), (0,)), ((), ())), "nt": (((1,), (1,)), ((), ())), "tn": (((0,), (0,)), ((), ()))}[mode]

    def body(a_ref, b_ref, o_ref, *scratch):
        prod = lax.dot_general(a_ref[...], b_ref[...], dn, preferred_element_type=F32)
        if nk == 1:
            o_ref[...] = prod.astype(o_ref.dtype)
            return
        (acc_ref,) = scratch
        k = pl.program_id(2)

        @pl.when(k == 0)
        def _():
            acc_ref[...] = prod

        @pl.when(k > 0)
        def _():
            acc_ref[...] += prod

        @pl.when(k == nk - 1)
        def _():
            o_ref[...] = acc_ref[...].astype(o_ref.dtype)

    if mode == "nn":
        a_spec = pl.BlockSpec((tm, tk), lambda i, j, k: (i, k))
        b_spec = pl.BlockSpec((tk, tn), lambda i, j, k: (k, j + noff))
    elif mode == "nt":
        a_spec = pl.BlockSpec((tm, tk), lambda i, j, k: (i, k))
        b_spec = pl.BlockSpec((tn, tk), lambda i, j, k: (j, k))
    else:
        a_spec = pl.BlockSpec((tk, tm), lambda i, j, k: (k, i))
        b_spec = pl.BlockSpec((tk, tn), lambda i, j, k: (k, j))
    return pl.pallas_call(
        body,
        name=name,
        grid=(M // tm, N // tn, nk),
        in_specs=[a_spec, b_spec],
        out_specs=pl.BlockSpec((tm, tn), lambda i, j, k: (i, j)),
        out_shape=jax.ShapeDtypeStruct((M, N), out_dtype),
        scratch_shapes=[pltpu.VMEM((tm, tn), F32)] if nk > 1 else [],
        compiler_params=_cparams(("parallel", "parallel", "arbitrary")),
    )(a, b)


def _rowwise(fn, name, tiled, params, outs, sums=(), *, tr=256):
    T = tiled[0][0].shape[0]
    tr = _pick(T, tr)
    nt, npar, no, ns = len(tiled), len(params), len(outs), len(sums)

    def body(*refs):
        ins = [r[...] for r in refs[:nt + npar]]
        res = fn(*ins)
        if not isinstance(res, (tuple, list)):
            res = (res,)
        o_refs = refs[nt + npar:nt + npar + no]
        s_refs = refs[nt + npar + no:]
        for r, v in zip(o_refs, res[:no]):
            r[...] = v.astype(r.dtype)
        if ns:
            @pl.when(pl.program_id(0) == 0)
            def _():
                for r in s_refs:
                    r[...] = jnp.zeros_like(r)
            for r, v in zip(s_refs, res[no:]):
                r[...] += v

    in_specs = []
    args = []
    for arr, off, w in tiled:
        assert off % w == 0 or w == arr.shape[1]
        cb = off // w
        in_specs.append(pl.BlockSpec((tr, w), lambda i, cb=cb: (i, cb)))
        args.append(arr)
    for p in params:
        in_specs.append(pl.BlockSpec(p.shape, lambda i, nd=p.ndim: (0,) * nd))
        args.append(p)
    out_specs = [pl.BlockSpec((tr, w), lambda i: (i, 0)) for w, _ in outs]
    out_shape = [jax.ShapeDtypeStruct((T, w), dt) for w, dt in outs]
    out_specs += [pl.BlockSpec((8, w), lambda i: (0, 0)) for w in sums]
    out_shape += [jax.ShapeDtypeStruct((8, w), F32) for w in sums]
    res = pl.pallas_call(
        body,
        name=name,
        grid=(T // tr,),
        in_specs=in_specs,
        out_specs=out_specs,
        out_shape=out_shape,
        compiler_params=_cparams(("arbitrary",)),
    )(*args)
    return res


def _colsum8(v):
    r, w = v.shape
    return jnp.sum(v.reshape(r // 8, 8, w), axis=0)


NEG_BIG = -1e30
QK_SCALE = A_HEAD_DIM ** -0.5
PAIRS = A_HEADS // 2
QKV_BLOCKS = A_WIDTH // LANE
FOX_TQ = 1024
FOX_TK = 1024
FOX_BWD_TQ = 512
_NT = (((1,), (1,)), ((), ()))
_NN = (((1,), (0,)), ((), ()))
_TN = (((0,), (0,)), ((), ()))


def _first_head(shape):
    return lax.broadcasted_iota(jnp.int32, shape, 1) < A_HEAD_DIM


def _split_heads(a, first):
    return jnp.where(first, a, 0), jnp.where(first, 0, a)


def _visible(i, j, tq, tk, keys_on_rows=False):
    shape = (tk, tq) if keys_on_rows else (tq, tk)
    r = lax.broadcasted_iota(jnp.int32, shape, 0)
    c = lax.broadcasted_iota(jnp.int32, shape, 1)
    return ((c - r) if keys_on_rows else (r - c)) >= j * tk - i * tq


def _lanes01(a0, a1, rows):
    lane = lax.broadcasted_iota(jnp.int32, (rows, LANE), 1)
    return jnp.where(lane == 0, a0, jnp.where(lane == 1, a1, 0.0))


def _ride(ride):
    if ride is None:
        return [], [], [], [], []
    src, gather = ride
    return [ANY], [ANY], [_push_shape(src, gather)], PUSH_SCRATCH, [src]


def _ride_steps(ride, refs, n_in, n_out, first, last):
    if ride is None:
        return refs
    src_ref, dst_ref, sems = refs[n_in], refs[n_in + 1 + n_out], refs[-3:]
    pl.when(first)(lambda: _push_start(ride[1], src_ref, dst_ref, *sems))
    pl.when(last)(lambda: _push_wait(ride[1], src_ref, dst_ref, *sems))
    return refs[:n_in] + refs[n_in + 1:n_in + 1 + n_out] + refs[n_in + 2 + n_out:-3]


def _fox_fwd(qkv, frow, tq, tk, ride=None):
    T = qkv.shape[0]
    nbk = T // tk
    nq = T // tq
    r_in, r_out, r_shape, r_scratch, r_args = _ride(ride)

    def body(*refs):
        hp, i = pl.program_id(0), pl.program_id(1)
        refs = _ride_steps(ride, refs, 4, 3, (hp == 0) & (i == 0), (hp == PAIRS - 1) & (i == nq - 1))
        q_ref, k_ref, v_ref, fk_ref, o_ref, ob_ref, lse_ref, m_ref, l_ref, acc_ref = refs
        first = _first_head((tq, LANE))
        qs = _split_heads(q_ref[...] * QK_SCALE, first)
        m_ref[...] = jnp.full_like(m_ref, NEG_BIG)
        l_ref[...] = jnp.zeros_like(l_ref)
        acc_ref[...] = jnp.zeros_like(acc_ref)

        def step(j, diag):
            rows = pl.ds(pl.multiple_of(j * tk, tk), tk)
            k2, v2 = k_ref[rows, :], v_ref[rows, :]
            for hh in range(2):
                s = lax.dot_general(qs[hh], k2, _NT, preferred_element_type=F32) - fk_ref[hh, j]
                if diag:
                    s = jnp.where(_visible(i, j, tq, tk), s, NEG_BIG)
                m_old = m_ref[hh]
                m_new = jnp.maximum(m_old, jnp.max(s, axis=1, keepdims=True))
                p = jnp.exp(s - m_new[:, :1])
                a = jnp.exp(m_old - m_new)
                l_ref[hh] = a * l_ref[hh] + jnp.sum(p, axis=1, keepdims=True)
                acc_ref[hh] = a * acc_ref[hh] + lax.dot_general(p.astype(BF16), v2, _NN, preferred_element_type=F32)
                m_ref[hh] = m_new

        def off_diag(j, c):
            step(j, False)
            return c

        n_full = lax.shift_right_logical(i * tq, tk.bit_length() - 1)
        lax.fori_loop(0, n_full, off_diag, 0)
        for d in range(max(1, tq // tk)):
            step(n_full + d, True)
        o = jnp.where(first, acc_ref[0] / l_ref[0], acc_ref[1] / l_ref[1])
        o_ref[...] = o
        ob_ref[...] = o.astype(BF16)
        lse_ref[0] = _lanes01((m_ref[0] + jnp.log(l_ref[0]))[:, :1], (m_ref[1] + jnp.log(l_ref[1]))[:, :1], tq)

    col_blk = pl.BlockSpec((tq, LANE), lambda hp, i: (i, hp))
    return pl.pallas_call(
        body,
        name="fox_fwd",
        grid=(PAIRS, nq),
        in_specs=[col_blk,
                  pl.BlockSpec((T, LANE), lambda hp, i: (0, QKV_BLOCKS + hp)),
                  pl.BlockSpec((T, LANE), lambda hp, i: (0, 2 * QKV_BLOCKS + hp)),
                  pl.BlockSpec((2, nbk, 1, tk), lambda hp, i: (hp, 0, 0, 0))] + r_in,
        out_specs=[col_blk, col_blk, pl.BlockSpec((1, tq, LANE), lambda hp, i: (hp, i, 0))] + r_out,
        out_shape=[jax.ShapeDtypeStruct((T, A_WIDTH), F32), jax.ShapeDtypeStruct((T, A_WIDTH), BF16),
                   jax.ShapeDtypeStruct((PAIRS, T, LANE), F32)] + r_shape,
        scratch_shapes=[pltpu.VMEM((2, tq, LANE), F32)] * 3 + r_scratch,
        compiler_params=_cparams(("arbitrary", "arbitrary")),
    )(qkv, qkv, qkv, frow, *r_args)


def _fox_delta(do, o):
    T = do.shape[0]
    tr = min(T, FOX_TQ)

    def body(do_ref, o_ref, dl_ref):
        first = _first_head((tr, LANE))
        prod = do_ref[...].astype(F32) * o_ref[...]
        dl_ref[0] = _lanes01(jnp.sum(jnp.where(first, prod, 0.0), axis=1, keepdims=True),
                             jnp.sum(jnp.where(first, 0.0, prod), axis=1, keepdims=True), tr)

    blk = pl.BlockSpec((tr, LANE), lambda hp, i: (i, hp))
    return pl.pallas_call(
        body,
        name="fox_delta",
        grid=(PAIRS, T // tr),
        in_specs=[blk, blk],
        out_specs=pl.BlockSpec((1, tr, LANE), lambda hp, i: (hp, i, 0)),
        out_shape=jax.ShapeDtypeStruct((PAIRS, T, LANE), F32),
        compiler_params=_cparams(("parallel", "parallel")),
    )(do, o)


def _fox_bwd(qkv, do, lse_row, delta_row, fpad, tq, tk, ride=None):
    T = qkv.shape[0]
    nbq = T // tq
    nbk = T // tk
    r_in, r_out, r_shape, r_scratch, r_args = _ride(ride)

    def body(*refs):
        hp = pl.program_id(0)
        j = pl.program_id(1)
        refs = _ride_steps(ride, refs, 7, 5, (hp == 0) & (j == 0), (hp == PAIRS - 1) & (j == nbk - 1))
        q_ref, k_ref, v_ref, do_ref, lse_ref, dl_ref, f_ref, dq_ref, dk_ref, dv_ref, df_ref, dfq_ref, dk_acc, dv_acc, df_acc = refs
        first = _first_head((tk, LANE))
        k2 = k_ref[...]
        ks = _split_heads(k2, first)
        ksq = _split_heads(k2 * QK_SCALE, first)
        vs = _split_heads(v_ref[...], first)
        lane = lax.broadcasted_iota(jnp.int32, (tk, LANE), 1)
        fks = tuple(jnp.sum(jnp.where(lane == 2 * hp + hh, f_ref[...], 0.0), axis=1, keepdims=True) for hh in range(2))
        dk_acc[...] = jnp.zeros_like(dk_acc)
        dv_acc[...] = jnp.zeros_like(dv_acc)
        df_acc[...] = jnp.zeros_like(df_acc)

        @pl.when(j == 0)
        def _():
            dq_ref[...] = jnp.zeros_like(dq_ref)
            dfq_ref[...] = jnp.zeros_like(dfq_ref)

        def step(i, diag):
            rows = pl.ds(pl.multiple_of(i * tq, tq), tq)
            q2 = q_ref[rows, :] * QK_SCALE
            do2 = do_ref[rows, :]
            dq_add = None
            for hh in range(2):
                st = lax.dot_general(ks[hh], q2, _NT, preferred_element_type=F32) - fks[hh]
                pt = jnp.exp(st - lse_ref[hh, i])
                if diag:
                    pt = jnp.where(_visible(i, j, tq, tk, keys_on_rows=True), pt, 0.0)
                dpt = lax.dot_general(vs[hh], do2, _NT, preferred_element_type=F32)
                dst = pt * (dpt - dl_ref[hh, i])
                dsb = dst.astype(BF16)
                dv_acc[hh] += lax.dot_general(pt.astype(BF16), do2, _NN, preferred_element_type=F32)
                dk_acc[hh] += lax.dot_general(dsb, q2, _NN, preferred_element_type=F32)
                df_acc[hh] += jnp.sum(dst, axis=1, keepdims=True)
                dfq_ref[hh, i] += jnp.sum(dst, axis=0, keepdims=True)
                part = lax.dot_general(dsb, ksq[hh], _TN, preferred_element_type=F32)
                dq_add = part if dq_add is None else dq_add + part
            dq_ref[rows, :] += dq_add

        i0 = lax.shift_right_logical(j * tk, tq.bit_length() - 1)
        nd = max(1, tk // tq)
        for d in range(nd):
            step(i0 + d, True)

        def off_diag(i, c):
            step(i, False)
            return c

        lax.fori_loop(i0 + nd, nbq, off_diag, 0)
        dk_ref[...] = jnp.where(first, dk_acc[0], dk_acc[1]).astype(BF16)
        dv_ref[...] = jnp.where(first, dv_acc[0], dv_acc[1]).astype(BF16)
        df_ref[0] = _lanes01(-df_acc[0][:, 0:1], -df_acc[1][:, 0:1], tk)

    whole = lambda cb: pl.BlockSpec((T, LANE), lambda hp, j, cb=cb: (0, cb + hp), pipeline_mode=pl.Buffered(1))
    blk = lambda cb: pl.BlockSpec((tk, LANE), lambda hp, j, cb=cb: (j, cb + hp))
    rows = pl.BlockSpec((2, nbq, 1, tq), lambda hp, j: (hp, 0, 0, 0))
    return pl.pallas_call(
        body,
        name="fox_bwd",
        grid=(PAIRS, nbk),
        in_specs=[whole(0), blk(QKV_BLOCKS), blk(2 * QKV_BLOCKS), whole(0), rows, rows, pl.BlockSpec((tk, LANE), lambda hp, j: (j, 0))] + r_in,
        out_specs=[whole(0), blk(0), blk(0), pl.BlockSpec((1, tk, LANE), lambda hp, j: (hp, j, 0)), rows] + r_out,
        out_shape=[jax.ShapeDtypeStruct((T, A_WIDTH), F32), jax.ShapeDtypeStruct((T, A_WIDTH), BF16), jax.ShapeDtypeStruct((T, A_WIDTH), BF16),
                   jax.ShapeDtypeStruct((PAIRS, T, LANE), F32), jax.ShapeDtypeStruct((A_HEADS, nbq, 1, tq), F32)] + r_shape,
        scratch_shapes=[pltpu.VMEM((2, tk, LANE), F32)] * 3 + r_scratch,
        compiler_params=_cparams(("arbitrary", "arbitrary")),
    )(qkv, qkv, qkv, do, lse_row, delta_row, fpad, *r_args)


HG_ROWS = 512
_HI = lax.Precision.HIGHEST


def _tri(n, upper):
    r = lax.broadcasted_iota(jnp.int32, (n, n), 0)
    c = lax.broadcasted_iota(jnp.int32, (n, n), 1)
    return jnp.where((r <= c) if upper else (r >= c), 1.0, 0.0).astype(F32)


def _dot(a, b, dn):
    return lax.dot_general(a.astype(BF16), b.astype(BF16), dn, preferred_element_type=F32)


def _hgrn_gates(qb, fb, lb):
    sig = jax.nn.sigmoid(fb)
    f = lb + (1.0 - lb) * sig
    sq = jax.nn.sigmoid(qb)
    return sig, f, sq, qb * sq


def _hgrn_fwd(rest, lb, col_q, col_f, col_i):
    T = rest.shape[0]
    tb = _pick(T, HG_ROWS)
    cpb = tb // CHUNK
    nblk = T // tb

    def body(q_ref, f_ref, i_ref, lb_ref, o_ref, st_ref, s_ref):
        @pl.when(pl.program_id(0) == 0)
        def _():
            s_ref[...] = jnp.zeros_like(s_ref)

        tril = _tri(CHUNK, False)
        mask = tril > 0.5

        def chunk(c, carry):
            rows = pl.ds(pl.multiple_of(c * CHUNK, CHUNK), CHUNK)
            for h in range(B_HEADS):
                cols = slice(h * B_DK, (h + 1) * B_DK)
                _, f, _, q = _hgrn_gates(q_ref[rows, cols], f_ref[rows, cols], lb_ref[:, cols])
                v = i_ref[rows, cols]
                k = 1.0 - f
                b = jnp.dot(tril, jnp.log(f), precision=_HI, preferred_element_type=F32)
                bm = b[CHUNK // 2 - 1:CHUNK // 2, :]
                bl = b[CHUNK - 1:CHUNK, :]
                st = s_ref[h]
                st_ref[h, c] = st
                p = jnp.where(mask, _dot(q * jnp.exp(b - bm), k * jnp.exp(bm - b), _NT), 0.0)
                o_ref[rows, cols] = _dot(p, v, _NN) + _dot(q * jnp.exp(b), st, _NT)
                s_ref[h] = st * jnp.exp(bl) + _dot(v, k * jnp.exp(bl - b), _TN)
            return carry

        lax.fori_loop(0, cpb, chunk, 0)

    col = lambda cb: pl.BlockSpec((tb, B_WIDTH), lambda i, cb=cb: (i, cb))
    return pl.pallas_call(
        body,
        name="hgrn_fwd",
        grid=(nblk,),
        in_specs=[col(col_q // B_WIDTH), col(col_f // B_WIDTH), col(col_i // B_WIDTH), pl.BlockSpec((1, B_WIDTH), lambda i: (0, 0))],
        out_specs=[pl.BlockSpec((tb, B_WIDTH), lambda i: (i, 0)),
                   pl.BlockSpec((B_HEADS, cpb, B_DK, B_DK), lambda i: (0, i, 0, 0))],
        out_shape=[jax.ShapeDtypeStruct((T, B_WIDTH), F32), jax.ShapeDtypeStruct((B_HEADS, T // CHUNK, B_DK, B_DK), F32)],
        scratch_shapes=[pltpu.VMEM((B_HEADS, B_DK, B_DK), F32)],
        compiler_params=_cparams(("arbitrary",)),
    )(rest, rest, rest, lb)


def _hgrn_bwd(rest, lb, states, do, col_q, col_f, col_i):
    T = rest.shape[0]
    tb = _pick(T, HG_ROWS)
    cpb = tb // CHUNK
    nblk = T // tb

    def body(q_ref, f_ref, i_ref, lb_ref, st_ref, do_ref, dq_ref, df_ref, di_ref, dlb_ref, ds_ref):
        @pl.when(pl.program_id(0) == 0)
        def _():
            ds_ref[...] = jnp.zeros_like(ds_ref)
            dlb_ref[...] = jnp.zeros_like(dlb_ref)

        tril = _tri(CHUNK, False)
        triu = _tri(CHUNK, True)
        mask = tril > 0.5

        def chunk(cc, carry):
            c = cpb - 1 - cc
            rows = pl.ds(pl.multiple_of(c * CHUNK, CHUNK), CHUNK)
            for h in range(B_HEADS):
                cols = slice(h * B_DK, (h + 1) * B_DK)
                lbv = lb_ref[:, cols]
                qb = q_ref[rows, cols]
                sig, f, sq, q = _hgrn_gates(qb, f_ref[rows, cols], lbv)
                v = i_ref[rows, cols]
                dov = do_ref[rows, cols]
                k = 1.0 - f
                b = jnp.dot(tril, jnp.log(f), precision=_HI, preferred_element_type=F32)
                bm = b[CHUNK // 2 - 1:CHUNK // 2, :]
                bl = b[CHUNK - 1:CHUNK, :]
                eb, eqm, ekm, ekl, ebl = jnp.exp(b), jnp.exp(b - bm), jnp.exp(bm - b), jnp.exp(bl - b), jnp.exp(bl)
                qt, kt, qa, kd = q * eqm, k * ekm, q * eb, k * ekl
                st0 = st_ref[h, c]
                dst1 = ds_ref[h]
                p = jnp.where(mask, _dot(qt, kt, _NT), 0.0)
                dp = jnp.where(mask, _dot(dov, v, _NT), 0.0)
                dv = _dot(p, dov, _TN) + _dot(kd, dst1, _NT)
                dqt = _dot(dp, kt, _NN)
                dkt = _dot(dp, qt, _TN)
                dqa = _dot(dov, st0, _NN)
                dkd = _dot(v, dst1, _NN)
                ds_ref[h] = _dot(dov, qa, _TN) + dst1 * ebl
                dq = dqt * eqm + dqa * eb
                dk = dkt * ekm + dkd * ekl
                rnd = lambda a: a.astype(BF16).astype(F32)
                db = dqt * rnd(qt) - dkt * rnd(kt) + dqa * qa - dkd * kd
                last = jnp.sum(dkd * kd, axis=0, keepdims=True) + ebl * jnp.sum(dst1 * st0, axis=0, keepdims=True)
                dg = jnp.dot(triu, db, precision=_HI, preferred_element_type=F32) + last
                dfv = dg / f - dk
                dq_ref[rows, cols] = (dq * (sq * (1.0 + qb * (1.0 - sq)))).astype(BF16)
                df_ref[rows, cols] = (dfv * (1.0 - lbv) * sig * (1.0 - sig)).astype(BF16)
                di_ref[rows, cols] = dv.astype(BF16)
                dlb_ref[0:1, cols] += jnp.sum(dfv * (1.0 - sig), axis=0, keepdims=True)
            return carry

        lax.fori_loop(0, cpb, chunk, 0)

    rev = lambda i: nblk - 1 - i
    col = lambda cb: pl.BlockSpec((tb, B_WIDTH), lambda i, cb=cb: (rev(i), cb))
    return pl.pallas_call(
        body,
        name="hgrn_bwd",
        grid=(nblk,),
        in_specs=[col(col_q // B_WIDTH), col(col_f // B_WIDTH), col(col_i // B_WIDTH), pl.BlockSpec((1, B_WIDTH), lambda i: (0, 0)),
                  pl.BlockSpec((B_HEADS, cpb, B_DK, B_DK), lambda i: (0, rev(i), 0, 0)), col(0)],
        out_specs=[col(0), col(0), col(0), pl.BlockSpec((8, B_WIDTH), lambda i: (0, 0))],
        out_shape=[jax.ShapeDtypeStruct((T, B_WIDTH), BF16)] * 3 + [jax.ShapeDtypeStruct((8, B_WIDTH), F32)],
        scratch_shapes=[pltpu.VMEM((B_HEADS, B_DK, B_DK), F32)],
        compiler_params=_cparams(("arbitrary",)),
    )(rest, rest, rest, lb, states, do)


SCAN_ROWS = 256


def _log_sigmoid(x):
    return jnp.minimum(x, 0.0) - jnp.log(1.0 + jnp.exp(-jnp.abs(x)))


def _head_cols(shape):
    return lax.broadcasted_iota(jnp.int32, shape, 1) < A_HEADS


def _fgate_fwd(rest, bf_pad):
    T = rest.shape[0]
    tr = _pick(T, SCAN_ROWS)

    def body(fa_ref, bf_ref, f_ref, carry_ref):
        @pl.when(pl.program_id(0) == 0)
        def _():
            carry_ref[...] = jnp.zeros_like(carry_ref)

        lf = jnp.where(_head_cols((tr, LANE)), _log_sigmoid(fa_ref[...] + bf_ref[...]), 0.0)
        f = jnp.dot(_tri(tr, False), lf, precision=_HI, preferred_element_type=F32) + carry_ref[...]
        f_ref[...] = f
        carry_ref[...] = f[tr - 1:tr, :]

    return pl.pallas_call(
        body,
        name="fgate_fwd",
        grid=(T // tr,),
        in_specs=[pl.BlockSpec((tr, LANE), lambda i: (i, C_FA // LANE)), pl.BlockSpec((1, LANE), lambda i: (0, 0))],
        out_specs=pl.BlockSpec((tr, LANE), lambda i: (i, 0)),
        out_shape=jax.ShapeDtypeStruct((T, LANE), F32),
        scratch_shapes=[pltpu.VMEM((1, LANE), F32)],
        compiler_params=_cparams(("arbitrary",)),
    )(rest, bf_pad)


def _fgate_bwd(dfq, dfk, rest, bf_pad):
    T = rest.shape[0]
    tr = _pick(T, SCAN_ROWS)
    nblk = T // tr

    def body(dq_ref, dk_ref, fa_ref, bf_ref, dfa_ref, db_ref, carry_ref):
        @pl.when(pl.program_id(0) == 0)
        def _():
            carry_ref[...] = jnp.zeros_like(carry_ref)
            db_ref[...] = jnp.zeros_like(db_ref)

        df = dq_ref[...] + dk_ref[0]
        for hp in range(1, PAIRS):
            df = df + pltpu.roll(dk_ref[hp], 2 * hp, 1)
        dlf = jnp.dot(_tri(tr, True), df, precision=_HI, preferred_element_type=F32) + carry_ref[...]
        carry_ref[...] = dlf[0:1, :]
        dfa = jnp.where(_head_cols((tr, LANE)), dlf * jax.nn.sigmoid(-(fa_ref[...] + bf_ref[...])), 0.0)
        dfa_ref[...] = dfa.astype(BF16)
        db_ref[...] += _colsum8(dfa)

    rev = lambda i: nblk - 1 - i
    return pl.pallas_call(
        body,
        name="fgate_bwd",
        grid=(nblk,),
        in_specs=[pl.BlockSpec((tr, LANE), lambda i: (rev(i), 0)), pl.BlockSpec((PAIRS, tr, LANE), lambda i: (0, rev(i), 0)),
                  pl.BlockSpec((tr, LANE), lambda i: (rev(i), C_FA // LANE)), pl.BlockSpec((1, LANE), lambda i: (0, 0))],
        out_specs=[pl.BlockSpec((tr, LANE), lambda i: (rev(i), 0)), pl.BlockSpec((8, LANE), lambda i: (0, 0))],
        out_shape=[jax.ShapeDtypeStruct((T, LANE), BF16), jax.ShapeDtypeStruct((8, LANE), F32)],
        scratch_shapes=[pltpu.VMEM((1, LANE), F32)],
        compiler_params=_cparams(("arbitrary",)),
    )(dfq, dfk, rest, bf_pad)


def _head_rows(a, t):
    return a[:, :A_HEADS].T.reshape(A_HEADS, a.shape[0] // t, 1, t)


def _pair_rows(a, t):
    return a[:, :, :2].transpose(0, 2, 1).reshape(A_HEADS, a.shape[1] // t, 1, t)


def _ln_stats(z):
    mu = jnp.mean(z, axis=-1, keepdims=True)
    zc = z - mu
    rstd = lax.rsqrt(jnp.mean(zc * zc, axis=-1, keepdims=True) + LN_EPS)
    return zc * rstd, rstd


def _ln_fwd(x, y, g, b, name):
    def fn(xv, yv, gv, bv):
        xhat, _ = _ln_stats(ALPHA * xv + yv)
        out = xhat * gv + bv
        return out, out
    return _rowwise(fn, name, [(x, 0, D_MODEL), (y, 0, D_MODEL)], [g, b], [(D_MODEL, F32), (D_MODEL, BF16)])


def _ln_bwd_core(xv, yv, gv, dy):
    xhat, rstd = _ln_stats(ALPHA * xv + yv)
    dxh = dy * gv
    dz = rstd * (dxh - jnp.mean(dxh, axis=-1, keepdims=True) - xhat * jnp.mean(dxh * xhat, axis=-1, keepdims=True))
    return dz, _colsum8(dy * xhat), _colsum8(dy)


def _ln_bwd(x, y, g, ua, ub, name):
    def fn(xv, yv, uav, ubv, gv):
        dz, dg, db = _ln_bwd_core(xv, yv, gv, ALPHA * uav + ubv)
        return dz, dz, dg, db
    return _rowwise(fn, name, [(x, 0, D_MODEL), (y, 0, D_MODEL), (ua, 0, D_MODEL), (ub, 0, D_MODEL)], [g],
                    [(D_MODEL, F32), (D_MODEL, BF16)], [D_MODEL, D_MODEL])


def _ln_bwd_loss(x, y, g, b, target, name):
    def fn(xv, yv, tv, gv, bv):
        xhat, _ = _ln_stats(ALPHA * xv + yv)
        err = xhat * gv + bv - tv
        dz, dg, db = _ln_bwd_core(xv, yv, gv, err * (1.0 / D_MODEL))
        return dz, dz, dg, db, _colsum8(err * err)
    return _rowwise(fn, name, [(x, 0, D_MODEL), (y, 0, D_MODEL), (target, 0, D_MODEL)], [g, b],
                    [(D_MODEL, F32), (D_MODEL, BF16)], [D_MODEL, D_MODEL, D_MODEL])


def _silu_parts(g):
    s = jax.nn.sigmoid(g)
    return g * s, s * (1.0 + g * (1.0 - s))


def _layer_fwd(x, xb, w, bf_pad, lb, norm_g, ln1_g, ln1_b, ln2_g, ln2_b, l, ride=None):
    T = x.shape[0]
    tq, tk = min(T, FOX_TQ), min(T, FOX_TK)
    sv = {}
    qkv = _matmul(xb, w["in"], "nn", BF16, f"l{l}_qkv", n_off=0, n_size=QKV_W, tn=768)
    rest = _matmul(xb, w["in"], "nn", F32, f"l{l}_rest", n_off=QKV_W, n_size=REST_W, tn=384)
    fcum = _fgate_fwd(rest, bf_pad)
    frow = _head_rows(fcum, tk)
    o_a, ya, lse, *rode = _fox_fwd(qkv, frow, tq, tk, ride)
    o_b, states = _hgrn_fwd(rest, lb, C_QB, C_FB, C_IB)

    def yb_fn(ov, gbv, ngv):
        outs = []
        for h in range(B_HEADS):
            oh = ov[:, h * B_DK:(h + 1) * B_DK]
            r = lax.rsqrt(jnp.mean(oh * oh, axis=-1, keepdims=True) + RMS_EPS)
            outs.append(oh * r * ngv)
        return jnp.concatenate(outs, axis=1) * jax.nn.sigmoid(gbv)
    (yb,) = _rowwise(yb_fn, f"l{l}_yb", [(o_b, 0, B_WIDTH), (rest, C_GB, B_WIDTH)], [norm_g], [(B_WIDTH, BF16)])
    pa = _matmul(ya, w["pa"], "nn", F32, f"l{l}_pa")
    pb = _matmul(yb, w["pb"], "nn", F32, f"l{l}_pb")

    def merge_fn(gav, gbv, pav, pbv):
        return jax.nn.sigmoid(gav) * pav + jax.nn.sigmoid(gbv) * pbv
    (merged,) = _rowwise(merge_fn, f"l{l}_merge", [(rest, C_GA, D_MODEL), (rest, C_GB2, D_MODEL), (pa, 0, D_MODEL), (pb, 0, D_MODEL)],
                         [], [(D_MODEL, BF16)])
    mix = _matmul(merged, w["o"], "nn", F32, f"l{l}_mix")
    x1, x1b = _ln_fwd(x, mix, ln1_g, ln1_b, f"l{l}_ln1")
    hid = _matmul(x1b, w["ffin"], "nn", F32, f"l{l}_ffin", tn=512)

    def act_fn(uv, gv):
        return _silu_parts(gv)[0] * uv
    (act,) = _rowwise(act_fn, f"l{l}_act", [(hid, 0, FFN_HIDDEN), (hid, FFN_HIDDEN, FFN_HIDDEN)], [], [(FFN_HIDDEN, BF16)], tr=128)
    ffn = _matmul(act, w["ffout"], "nn", F32, f"l{l}_ffout", tm=1024)
    x2, x2b = _ln_fwd(x1, ffn, ln2_g, ln2_b, f"l{l}_ln2")
    sv.update(x=x, xb=xb, rest=rest, qkv=qkv, fcum=fcum, frow=frow, o_a=o_a, lse=lse, ya=ya, o_b=o_b, states=states,
              yb=yb, pa=pa, pb=pb, merged=merged, mix=mix, x1=x1, x1b=x1b, hid=hid, act=act, ffn=ffn)
    return x2, x2b, sv, rode


def _layer_bwd(sv, w, dz2, dz2b, dln2, bf_pad, lb, norm_g, ln1_g, l, make_ride=None):
    T = sv["x"].shape[0]
    tq, tk = min(T, FOX_BWD_TQ), min(T, FOX_TK)
    gr = {}
    dact = _matmul(dz2b, w["ffout"], "nt", F32, f"l{l}_dact", tm=1024, tn=1408)
    gr["ffout"] = _matmul(sv["act"], dz2b, "tn", F32, f"l{l}_dwffout", tm=1408, tn=512, tk=2048)

    def dhid_fn(dav, uv, gv):
        sg, dsg = _silu_parts(gv)
        return jnp.concatenate([dav * sg, dav * uv * dsg], axis=1)
    (dhid,) = _rowwise(dhid_fn, f"l{l}_dhid", [(dact, 0, FFN_HIDDEN), (sv["hid"], 0, FFN_HIDDEN), (sv["hid"], FFN_HIDDEN, FFN_HIDDEN)],
                       [], [(2 * FFN_HIDDEN, BF16)], tr=128)
    dx1f = _matmul(dhid, w["ffin"], "nt", F32, f"l{l}_dx1f", tm=1024)
    gr["ffin"] = _matmul(sv["x1b"], dhid, "tn", F32, f"l{l}_dwffin", tm=1024, tn=512, tk=4096)
    dz1, dz1b, dg1, db1 = _ln_bwd(sv["x"], sv["mix"], ln1_g, dz2, dx1f, f"l{l}_dln1")
    dmerged = _matmul(dz1b, w["o"], "nt", F32, f"l{l}_dmerged")
    gr["o"] = _matmul(sv["merged"], dz1b, "tn", F32, f"l{l}_dwo", tm=1024, tn=512, tk=4096)

    def dmerge_fn(dmv, gav, gbv, pav, pbv):
        sa, sb = jax.nn.sigmoid(gav), jax.nn.sigmoid(gbv)
        return dmv * sa, dmv * sb, dmv * pav * sa * (1.0 - sa), dmv * pbv * sb * (1.0 - sb)
    dpa, dpb, dga, dgb2 = _rowwise(
        dmerge_fn, f"l{l}_dmerge",
        [(dmerged, 0, D_MODEL), (sv["rest"], C_GA, D_MODEL), (sv["rest"], C_GB2, D_MODEL), (sv["pa"], 0, D_MODEL), (sv["pb"], 0, D_MODEL)],
        [], [(D_MODEL, BF16)] * 4)
    dya = _matmul(dpa, w["pa"], "nt", BF16, f"l{l}_dya", tn=512)
    gr["pa"] = _matmul(sv["ya"], dpa, "tn", F32, f"l{l}_dwpa", tm=512, tn=512, tk=4096)
    dyb = _matmul(dpb, w["pb"], "nt", F32, f"l{l}_dyb", tn=512)
    gr["pb"] = _matmul(sv["yb"], dpb, "tn", F32, f"l{l}_dwpb", tm=512, tn=512, tk=4096)

    def dyb_fn(dyv, ov, gbv, ngv):
        sg = jax.nn.sigmoid(gbv)
        dos, dgs = [], []
        dng = jnp.zeros((8, B_DK), F32)
        for h in range(B_HEADS):
            sl = slice(h * B_DK, (h + 1) * B_DK)
            oh, dyh, sgh = ov[:, sl], dyv[:, sl], sg[:, sl]
            r = lax.rsqrt(jnp.mean(oh * oh, axis=-1, keepdims=True) + RMS_EPS)
            n = oh * r
            dyn = dyh * sgh
            dgs.append(dyh * n * ngv * sgh * (1.0 - sgh))
            dng = dng + _colsum8(dyn * n)
            dn = dyn * ngv
            dos.append(r * (dn - n * jnp.mean(dn * n, axis=-1, keepdims=True)))
        return jnp.concatenate(dos, axis=1), jnp.concatenate(dgs, axis=1), dng
    do_b, dgb, dng = _rowwise(dyb_fn, f"l{l}_dyb2", [(dyb, 0, B_WIDTH), (sv["o_b"], 0, B_WIDTH), (sv["rest"], C_GB, B_WIDTH)], [norm_g],
                              [(B_WIDTH, F32), (B_WIDTH, BF16)], [B_DK])
    dqb, dfb, dib, dlb = _hgrn_bwd(sv["rest"], lb, sv["states"], do_b, C_QB, C_FB, C_IB)
    delta = _fox_delta(dya, sv["o_a"])
    ride = (make_ride(gr), False) if make_ride else None
    dq, dk, dv, dfk, dfq, *rode = _fox_bwd(sv["qkv"], dya, _pair_rows(sv["lse"], tq), _pair_rows(delta, tq), sv["fcum"], tq, tk, ride)
    dfq_pad = jnp.pad(dfq.reshape(A_HEADS, T).T, ((0, 0), (0, LANE - A_HEADS)))
    dfa, dbf = _fgate_bwd(dfq_pad, dfk, sv["rest"], bf_pad)
    dproj = jnp.concatenate([dq.astype(BF16), dk, dv, dga, dgb2, dqb, dfb, dib, dgb, dfa], axis=1)
    dxm = _matmul(dproj, w["in"], "nt", F32, f"l{l}_dxm", tm=1024, tk=1920)
    gr["in"] = _matmul(sv["xb"], dproj, "tn", F32, f"l{l}_dwin", tm=1024, tn=640, tk=4096)
    small = dict(dln2=dln2, dln1=(dg1, db1), dng=dng, dlb=dlb, dbf=dbf)
    return dz1, dxm, gr, small, rode


MESH = pl.DeviceIdType.MESH
ANY = pl.BlockSpec(memory_space=pl.ANY)


def _me_and_peers():
    x, y, c = lax.axis_index("x"), lax.axis_index("y"), lax.axis_index("c")
    me = 4 * x + 2 * y + c
    peers = []
    for p in range(1, N_DEV):
        px, py, pc = x ^ ((p >> 2) & 1), y ^ ((p >> 1) & 1), c ^ (p & 1)
        peers.append(((px, py, pc), 4 * px + 2 * py + pc))
    return me, peers


PUSH_SCRATCH = [pltpu.SemaphoreType.DMA((N_DEV - 1,)), pltpu.SemaphoreType.DMA((N_DEV - 1,)), pltpu.SemaphoreType.DMA]


def _push_copies(gather, src_ref, out_ref, send_sems, recv_sems, local_sem):
    me, peers = _me_and_peers()
    part = (lambda idx: src_ref) if gather else (lambda idx: src_ref.at[idx])
    mine = lambda: pltpu.make_async_copy(part(me), out_ref.at[me], local_sem)
    mk = lambda p, dev, idx, slot: pltpu.make_async_remote_copy(
        src_ref=part(idx), dst_ref=out_ref.at[slot], send_sem=send_sems.at[p], recv_sem=recv_sems.at[p], device_id=dev, device_id_type=MESH)
    sends = lambda: [mk(p, dev, idx, me) for p, (dev, idx) in enumerate(peers)]
    recvs = lambda: [mk(p, dev, idx, idx) for p, (dev, idx) in enumerate(peers)]
    return mine, sends, recvs


def _push_start(gather, *refs):
    mine, sends, _ = _push_copies(gather, *refs)
    mine().start()
    for cp in sends():
        cp.start()


def _push_wait(gather, *refs):
    mine, sends, recvs = _push_copies(gather, *refs)
    for cp in recvs():
        cp.wait_recv()
    for cp in sends():
        cp.wait_send()
    mine().wait()


def _push_shape(src, gather):
    return jax.ShapeDtypeStruct(((N_DEV,) + src.shape) if gather else src.shape, src.dtype)


def _push(src, gather, name, vmem=False):
    def body(*refs):
        _push_start(gather, *refs)
        _push_wait(gather, *refs)

    spec = pl.BlockSpec(memory_space=pltpu.VMEM) if vmem else ANY
    return pl.pallas_call(body, name=name, in_specs=[spec], out_specs=spec, out_shape=_push_shape(src, gather),
                          scratch_shapes=PUSH_SCRATCH)(src)


def _sum_adam(parts, w, m, v, name, tr=ADAM_ROWS):
    _, R, W = parts.shape
    tr = _pick(R, tr) if R % LANE == 0 else R
    c1 = 1.0 / (1.0 - ADAM_B1 ** ADAM_STEP)
    c2 = 1.0 / (1.0 - ADAM_B2 ** ADAM_STEP)

    def body(p_ref, w_ref, m_ref, v_ref, g_ref, d_ref, nm_ref, nv_ref):
        g = p_ref[0]
        for s in range(1, N_DEV):
            g = g + p_ref[s]
        nm = ADAM_B1 * m_ref[...] + (1.0 - ADAM_B1) * g
        nv = ADAM_B2 * v_ref[...] + (1.0 - ADAM_B2) * (g * g)
        g_ref[...] = g
        nm_ref[...] = nm
        nv_ref[...] = nv
        d_ref[...] = -ADAM_LR * ((nm * c1) / (jnp.sqrt(nv * c2) + ADAM_EPS) + ADAM_WD * w_ref[...])

    blk = pl.BlockSpec((tr, W), lambda i: (i, 0))
    return pl.pallas_call(
        body,
        name=name,
        grid=(R // tr,),
        in_specs=[pl.BlockSpec((N_DEV, tr, W), lambda i: (0, i, 0)), blk, blk, blk],
        out_specs=[blk] * 4,
        out_shape=[jax.ShapeDtypeStruct((R, W), F32)] * 4,
        compiler_params=_cparams(("parallel",)),
    )(parts, w, m, v)


BIG = ("in", "pa", "pb", "o", "ffin", "ffout")
COL_SHARDED = {"in": True, "pa": True, "pb": True, "o": False, "ffin": True, "ffout": False}
SHARD_ROWS = {"in": 5640, "pa": 512, "pb": 512, "o": 1024, "ffin": 5632, "ffout": 2816}
_round_up = lambda n, m: -(-n // m) * m
GATHER_ROWS = _round_up(sum(SHARD_ROWS.values()), ADAM_ROWS)
EARLY = tuple((n, 1) for n in BIG) + (("ffin", 0), ("ffout", 0))
LATE = (("in", 0), ("pa", 0), ("pb", 0), ("o", 0))
EARLY_ROWS = _round_up(sum(SHARD_ROWS[n] for n, _ in EARLY), ADAM_ROWS)
LATE_ROWS = _round_up(sum(SHARD_ROWS[n] for n, _ in LATE), ADAM_ROWS)
SMALL_ROWS = 80


def _pack(parts, rows, lead=()):
    n = len(lead)
    cat = jnp.concatenate([p.reshape(lead + (-1, LANE)) for p in parts], axis=n)
    return jnp.pad(cat, ((0, 0),) * n + ((0, rows - cat.shape[n]), (0, 0)))


def _unpack(packed, shapes, lead=()):
    out, off = [], 0
    for s in shapes:
        r = math.prod(s) // LANE
        out.append(packed[..., off:off + r, :].reshape(lead + tuple(s)))
        off += r
    return out


def _pack_flat(parts, rows):
    flat = jnp.concatenate([p.reshape(-1) for p in parts])
    return jnp.pad(flat, (0, rows * LANE - flat.shape[0])).reshape(rows, LANE)


def _unpack_flat(packed, shapes):
    flat = packed.reshape(-1)
    out, off = [], 0
    for s in shapes:
        n = math.prod(s)
        out.append(flat[off:off + n].reshape(s))
        off += n
    return out


def _join_shards(g, col_sharded):
    if col_sharded:
        return g.transpose(1, 0, 2).reshape(g.shape[1], N_DEV * g.shape[2])
    return g.reshape(N_DEV * g.shape[1], g.shape[2])


def _split_shards(full, col_sharded):
    a, b = full.shape
    if col_sharded:
        return full.reshape(a, N_DEV, b // N_DEV).transpose(1, 0, 2)
    return full.reshape(N_DEV, a // N_DEV, b)


def _permute_in(w):
    a, b = QKV_W + A_HEADS, QKV_W + A_HEADS + 4 * B_WIDTH
    return jnp.concatenate([w[:, :QKV_W], w[:, b:], w[:, a:b], w[:, QKV_W:a],
                            jnp.zeros((w.shape[0], LANE - A_HEADS), w.dtype)], axis=1)


def _unpermute_in(d):
    g0, h0, f0 = QKV_W, QKV_W + 2 * D_MODEL, QKV_W + C_FA
    return jnp.concatenate([d[:, :QKV_W], d[:, f0:f0 + A_HEADS], d[:, h0:f0], d[:, g0:h0]], axis=1)


def _lower_bounds(logits):
    sm = jax.nn.softmax(logits.astype(F32), axis=0)
    return jnp.cumsum(sm, axis=0) - sm[0:1]


def kernel(x, w_in, b_fgate, hgrn_lb_logits, hgrn_norm_g, w_branch_a, w_branch_b, w_out, ln1_g, ln1_b, w_ff_in, w_ff_out, ln2_g, ln2_b, loss_target, m_w_in, m_b_fgate, m_hgrn_lb_logits, m_hgrn_norm_g, m_w_branch_a, m_w_branch_b, m_w_out, m_ln1_g, m_ln1_b, m_w_ff_in, m_w_ff_out, m_ln2_g, m_ln2_b, v_w_in, v_b_fgate, v_hgrn_lb_logits, v_hgrn_norm_g, v_w_branch_a, v_w_branch_b, v_w_out, v_ln1_g, v_ln1_b, v_w_ff_in, v_w_ff_out, v_ln2_g, v_ln2_b):
    big = {"in": (w_in, m_w_in, v_w_in), "pa": (w_branch_a, m_w_branch_a, v_w_branch_a), "pb": (w_branch_b, m_w_branch_b, v_w_branch_b),
           "o": (w_out, m_w_out, v_w_out), "ffin": (w_ff_in, m_w_ff_in, v_w_ff_in), "ffout": (w_ff_out, m_w_ff_out, v_w_ff_out)}
    small = [(b_fgate, m_b_fgate, v_b_fgate), (hgrn_lb_logits, m_hgrn_lb_logits, v_hgrn_lb_logits), (hgrn_norm_g, m_hgrn_norm_g, v_hgrn_norm_g),
             (ln1_g, m_ln1_g, v_ln1_g), (ln1_b, m_ln1_b, v_ln1_b), (ln2_g, m_ln2_g, v_ln2_g), (ln2_b, m_ln2_b, v_ln2_b)]
    shard_shapes = [big[n][0].shape[1:] for n in BIG]
    small_shapes = [s[0].shape for s in small]

    def weight_pack(l):
        return _pack([big[n][0][l].astype(BF16) for n in BIG], GATHER_ROWS)

    def layer_weights(gathered):
        w = {n: _join_shards(g, COL_SHARDED[n]) for n, g in zip(BIG, _unpack(gathered, shard_shapes, lead=(N_DEV,)))}
        w["in"] = _permute_in(w["in"])
        return w

    weights = [layer_weights(_push(weight_pack(0), True, "gather_weights")), None]
    lbounds = _lower_bounds(hgrn_lb_logits)

    def layer_params(l):
        bf_pad = jnp.pad(b_fgate[l].astype(F32), (0, LANE - A_HEADS)).reshape(1, LANE)
        row = lambda a: a[l].astype(F32).reshape(1, -1)
        return dict(bf_pad=bf_pad, lb=lbounds[l].reshape(1, -1), norm_g=row(hgrn_norm_g), ln1_g=row(ln1_g), ln1_b=row(ln1_b),
                    ln2_g=row(ln2_g), ln2_b=row(ln2_b))

    xl = x[0]
    xlb = xl.astype(BF16)
    saved = []
    for l in range(DEPTH):
        p = layer_params(l)
        ride = (weight_pack(l + 1), True) if l + 1 < DEPTH else None
        xl, xlb, sv, rode = _layer_fwd(xl, xlb, weights[l], p["bf_pad"], p["lb"], p["norm_g"], p["ln1_g"], p["ln1_b"],
                                       p["ln2_g"], p["ln2_b"], l, ride)
        if rode:
            weights[l + 1] = layer_weights(rode[0])
        saved.append(sv)

    grads = [None] * DEPTH
    smalls = [None] * DEPTH

    def contributions(group, rows):
        parts = []
        for n, l in group:
            g = _unpermute_in(grads[l][n]) if n == "in" else grads[l][n]
            parts.append(_split_shards(g, COL_SHARDED[n]))
        return _pack(parts, rows, lead=(N_DEV,))

    def early_ride(gr0):
        grads[0] = gr0
        return contributions(EARLY, EARLY_ROWS)

    loss_part = None
    ua = ub = None
    received = {}
    for l in reversed(range(DEPTH)):
        p, sv = layer_params(l), saved[l]
        if l == DEPTH - 1:
            dz2, dz2b, dg2, db2, loss_part = _ln_bwd_loss(sv["x1"], sv["ffn"], p["ln2_g"], p["ln2_b"], loss_target[0], f"l{l}_dln2")
        else:
            dz2, dz2b, dg2, db2 = _ln_bwd(sv["x1"], sv["ffn"], p["ln2_g"], ua, ub, f"l{l}_dln2")
        ua, ub, grads[l], smalls[l], rode = _layer_bwd(sv, weights[l], dz2, dz2b, (dg2, db2), p["bf_pad"], p["lb"], p["norm_g"],
                                                       p["ln1_g"], l, early_ride if l == 0 else None)
        if rode:
            received["early"] = rode[0]
    (grad_x,) = _rowwise(lambda a, b: ALPHA * a + b, "grad_x", [(ua, 0, D_MODEL), (ub, 0, D_MODEL)], [], [(D_MODEL, F32)])
    received["late"] = _push(contributions(LATE, LATE_ROWS), False, "exchange_grads")

    by_kind = [{} for _ in range(4)]
    for key, group, rows in (("early", EARLY, EARLY_ROWS), ("late", LATE, LATE_ROWS)):
        wmv = [_pack([big[n][i][l].astype(F32) for n, l in group], rows) for i in range(3)]
        shapes = [big[n][0].shape[1:] for n, _ in group]
        for kind, packed in enumerate(_sum_adam(received[key], *wmv, f"adam_{key}")):
            by_kind[kind].update(zip(group, _unpack(packed, shapes)))
    outs_big = [[jnp.stack([d[(n, l)] for l in range(DEPTH)]) for n in BIG] for d in by_kind]

    fold = lambda a: jnp.sum(a, axis=0)
    dlb = jnp.stack([fold(smalls[l]["dlb"]) for l in range(DEPTH)])
    _, lb_vjp = jax.vjp(_lower_bounds, hgrn_lb_logits)
    small_grads = [jnp.stack([fold(smalls[l]["dbf"])[:A_HEADS] for l in range(DEPTH)]), lb_vjp(dlb)[0],
                   jnp.stack([fold(smalls[l]["dng"]) for l in range(DEPTH)]),
                   jnp.stack([fold(smalls[l]["dln1"][0]) for l in range(DEPTH)]), jnp.stack([fold(smalls[l]["dln1"][1]) for l in range(DEPTH)]),
                   jnp.stack([fold(smalls[l]["dln2"][0]) for l in range(DEPTH)]), jnp.stack([fold(smalls[l]["dln2"][1]) for l in range(DEPTH)])]
    loss_local = (0.5 / D_MODEL) * jnp.sum(loss_part)
    small_all = _push(_pack_flat(small_grads + [loss_local.reshape(1)], SMALL_ROWS), True, "gather_small", vmem=True)
    one = jnp.zeros((1,), F32)
    swmv = [_pack_flat([s[i].astype(F32) for s in small] + [one], SMALL_ROWS) for i in range(3)]
    outs_small = [_unpack_flat(o, small_shapes + [(1,)]) for o in _sum_adam(small_all, *swmv, "adam_small")]
    loss = outs_small[0][-1].reshape(())

    order = {"in": 0, "pa": 4, "pb": 5, "o": 6, "ffin": 9, "ffout": 10}
    small_pos = [1, 2, 3, 7, 8, 11, 12]
    result = [loss, grad_x.reshape(x.shape)]
    for kind in range(4):
        slot = [None] * 13
        for n, arr in zip(BIG, outs_big[kind]):
            slot[order[n]] = arr
        for pos, arr in zip(small_pos, outs_small[kind][:-1]):
            slot[pos] = arr
        result.extend(slot)
    return tuple(result)
```

```python
import functools
import math

import jax
import jax.numpy as jnp
from jax import lax
from jax.experimental import pallas as pl
from jax.experimental.pallas import tpu as pltpu

F32 = jnp.float32
BF16 = jnp.bfloat16

D_MODEL = 1024
DEPTH = 2
CHUNK = 64
A_HEADS = 8
A_HEAD_DIM = 64
A_WIDTH = 512
B_WIDTH = 512
B_HEADS = 4
B_DK = 128
FFN_HIDDEN = 2816
IN_TOTAL = 5640
ALPHA = (2 * DEPTH) ** 0.25
LN_EPS = 1e-5
RMS_EPS = 1e-6
ADAM_LR = 0.001
ADAM_B1 = 0.9
ADAM_B2 = 0.999
ADAM_EPS = 1e-08
ADAM_WD = 0.01
ADAM_STEP = 10

N_DEV = 8
LANE = 128
VMEM_LIMIT = 48 * 1024 * 1024
ADAM_ROWS = 512

QKV_W = 3 * A_WIDTH
C_GA, C_GB2 = 0, D_MODEL
C_QB = 2 * D_MODEL
C_FB, C_IB, C_GB = C_QB + B_WIDTH, C_QB + 2 * B_WIDTH, C_QB + 3 * B_WIDTH
C_FA = C_QB + 4 * B_WIDTH
REST_W = C_FA + LANE
IN_PERM = QKV_W + REST_W


def _cparams(sem):
    return pltpu.CompilerParams(dimension_semantics=sem, vmem_limit_bytes=VMEM_LIMIT)


def _pick(n, pref):
    if n <= pref:
        return n
    t = pref
    while n % t:
        t -= LANE
    return t


def _matmul(a, b, mode, out_dtype, name, *, n_off=0, n_size=None, tm=2048, tn=1024, tk=2816):
    if mode == "nn":
        M, K = a.shape
        N = b.shape[1] if n_size is None else n_size
    elif mode == "nt":
        M, K = a.shape
        N = b.shape[0]
    else:
        K, M = a.shape
        N = b.shape[1]
    tm, tn, tk = _pick(M, tm), _pick(N, tn), _pick(K, tk)
    assert n_off % tn == 0
    noff = n_off // tn
    nk = K // tk

    dn = {"nn": (((1,), (0,)), ((), ())), "nt": (((1,), (1,)), ((), ())), "tn": (((0,), (0,)), ((), ()))}[mode]

    def body(a_ref, b_ref, o_ref, *scratch):
        prod = lax.dot_general(a_ref[...], b_ref[...], dn, preferred_element_type=F32)
        if nk == 1:
            o_ref[...] = prod.astype(o_ref.dtype)
            return
        (acc_ref,) = scratch
        k = pl.program_id(2)

        @pl.when(k == 0)
        def _():
            acc_ref[...] = prod

        @pl.when(k > 0)
        def _():
            acc_ref[...] += prod

        @pl.when(k == nk - 1)
        def _():
            o_ref[...] = acc_ref[...].astype(o_ref.dtype)

    if mode == "nn":
        a_spec = pl.BlockSpec((tm, tk), lambda i, j, k: (i, k))
        b_spec = pl.BlockSpec((tk, tn), lambda i, j, k: (k, j + noff))
    elif mode == "nt":
        a_spec = pl.BlockSpec((tm, tk), lambda i, j, k: (i, k))
        b_spec = pl.BlockSpec((tn, tk), lambda i, j, k: (j, k))
    else:
        a_spec = pl.BlockSpec((tk, tm), lambda i, j, k: (k, i))
        b_spec = pl.BlockSpec((tk, tn), lambda i, j, k: (k, j))
    return pl.pallas_call(
        body,
        name=name,
        grid=(M // tm, N // tn, nk),
        in_specs=[a_spec, b_spec],
        out_specs=pl.BlockSpec((tm, tn), lambda i, j, k: (i, j)),
        out_shape=jax.ShapeDtypeStruct((M, N), out_dtype),
        scratch_shapes=[pltpu.VMEM((tm, tn), F32)] if nk > 1 else [],
        compiler_params=_cparams(("parallel", "parallel", "arbitrary")),
    )(a, b)


def _rowwise(fn, name, tiled, params, outs, sums=(), *, tr=256):
    T = tiled[0][0].shape[0]
    tr = _pick(T, tr)
    nt, npar, no, ns = len(tiled), len(params), len(outs), len(sums)

    def body(*refs):
        ins = [r[...] for r in refs[:nt + npar]]
        res = fn(*ins)
        if not isinstance(res, (tuple, list)):
            res = (res,)
        o_refs = refs[nt + npar:nt + npar + no]
        s_refs = refs[nt + npar + no:]
        for r, v in zip(o_refs, res[:no]):
            r[...] = v.astype(r.dtype)
        if ns:
            @pl.when(pl.program_id(0) == 0)
            def _():
                for r in s_refs:
                    r[...] = jnp.zeros_like(r)
            for r, v in zip(s_refs, res[no:]):
                r[...] += v

    in_specs = []
    args = []
    for arr, off, w in tiled:
        assert off % w == 0 or w == arr.shape[1]
        cb = off // w
        in_specs.append(pl.BlockSpec((tr, w), lambda i, cb=cb: (i, cb)))
        args.append(arr)
    for p in params:
        in_specs.append(pl.BlockSpec(p.shape, lambda i, nd=p.ndim: (0,) * nd))
        args.append(p)
    out_specs = [pl.BlockSpec((tr, w), lambda i: (i, 0)) for w, _ in outs]
    out_shape = [jax.ShapeDtypeStruct((T, w), dt) for w, dt in outs]
    out_specs += [pl.BlockSpec((8, w), lambda i: (0, 0)) for w in sums]
    out_shape += [jax.ShapeDtypeStruct((8, w), F32) for w in sums]
    res = pl.pallas_call(
        body,
        name=name,
        grid=(T // tr,),
        in_specs=in_specs,
        out_specs=out_specs,
        out_shape=out_shape,
        compiler_params=_cparams(("arbitrary",)),
    )(*args)
    return res


def _colsum8(v):
    r, w = v.shape
    return jnp.sum(v.reshape(r // 8, 8, w), axis=0)


NEG_BIG = -1e30
QK_SCALE = A_HEAD_DIM ** -0.5
PAIRS = A_HEADS // 2
QKV_BLOCKS = A_WIDTH // LANE
FOX_TQ = 1024
FOX_TK = 1024
FOX_BWD_TQ = 512
_NT = (((1,), (1,)), ((), ()))
_NN = (((1,), (0,)), ((), ()))
_TN = (((0,), (0,)), ((), ()))


def _first_head(shape):
    return lax.broadcasted_iota(jnp.int32, shape, 1) < A_HEAD_DIM


def _split_heads(a, first):
    return jnp.where(first, a, 0), jnp.where(first, 0, a)


def _visible(i, j, tq, tk, keys_on_rows=False):
    shape = (tk, tq) if keys_on_rows else (tq, tk)
    r = lax.broadcasted_iota(jnp.int32, shape, 0)
    c = lax.broadcasted_iota(jnp.int32, shape, 1)
    return ((c - r) if keys_on_rows else (r - c)) >= j * tk - i * tq


def _lanes01(a0, a1, rows):
    lane = lax.broadcasted_iota(jnp.int32, (rows, LANE), 1)
    return jnp.where(lane == 0, a0, jnp.where(lane == 1, a1, 0.0))


def _ride(ride):
    if ride is None:
        return [], [], [], [], []
    src, gather = ride
    return [ANY], [ANY], [_push_shape(src, gather)], PUSH_SCRATCH, [src]


def _ride_steps(ride, refs, n_in, n_out, first, last):
    if ride is None:
        return refs
    src_ref, dst_ref, sems = refs[n_in], refs[n_in + 1 + n_out], refs[-3:]
    pl.when(first)(lambda: _push_start(ride[1], src_ref, dst_ref, *sems))
    pl.when(last)(lambda: _push_wait(ride[1], src_ref, dst_ref, *sems))
    return refs[:n_in] + refs[n_in + 1:n_in + 1 + n_out] + refs[n_in + 2 + n_out:-3]


def _fox_fwd(qkv, frow, tq, tk, ride=None):
    T = qkv.shape[0]
    assert tq == tk, "k blocks before q block i are fully visible, block i is the diagonal one"
    nbk = T // tk
    nq = T // tq
    r_in, r_out, r_shape, r_scratch, r_args = _ride(ride)

    def body(*refs):
        hp, i = pl.program_id(0), pl.program_id(1)
        refs = _ride_steps(ride, refs, 4, 3, (hp == 0) & (i == 0), (hp == PAIRS - 1) & (i == nq - 1))
        q_ref, k_ref, v_ref, fk_ref, o_ref, ob_ref, lse_ref, m_ref, l_ref, acc_ref, s_buf = refs
        first = _first_head((tq, LANE))
        qs = _split_heads(q_ref[...] * QK_SCALE, first)
        m_ref[...] = jnp.full_like(m_ref, NEG_BIG)
        l_ref[...] = jnp.zeros_like(l_ref)
        acc_ref[...] = jnp.zeros_like(acc_ref)

        def logits(j):
            k2 = k_ref[pl.ds(pl.multiple_of(j * tk, tk), tk), :]
            return [lax.dot_general(qs[hh], k2, _NT, preferred_element_type=F32) - fk_ref[hh, j] for hh in range(2)]

        def softmax_pv(j, ss, diag):
            v2 = v_ref[pl.ds(pl.multiple_of(j * tk, tk), tk), :]
            for hh in range(2):
                s = jnp.where(_visible(i, j, tq, tk), ss[hh], NEG_BIG) if diag else ss[hh]
                m_old = m_ref[hh]
                m_new = jnp.maximum(m_old, jnp.max(s, axis=1, keepdims=True))
                p = jnp.exp(s - m_new[:, :1])
                a = jnp.exp(m_old - m_new)
                l_ref[hh] = a * l_ref[hh] + jnp.sum(p, axis=1, keepdims=True)
                acc_ref[hh] = a * acc_ref[hh] + lax.dot_general(p.astype(BF16), v2, _NN, preferred_element_type=F32)
                m_ref[hh] = m_new

        s_buf[0], s_buf[1] = logits(0)

        def off_diag(j, c):
            cur = [s_buf[0], s_buf[1]]
            nxt = logits(j + 1)
            softmax_pv(j, cur, False)
            s_buf[0], s_buf[1] = nxt
            return c

        lax.fori_loop(0, i, off_diag, 0)
        softmax_pv(i, [s_buf[0], s_buf[1]], True)
        o = jnp.where(first, acc_ref[0] / l_ref[0], acc_ref[1] / l_ref[1])
        o_ref[...] = o
        ob_ref[...] = o.astype(BF16)
        lse_ref[0] = _lanes01((m_ref[0] + jnp.log(l_ref[0]))[:, :1], (m_ref[1] + jnp.log(l_ref[1]))[:, :1], tq)

    col_blk = pl.BlockSpec((tq, LANE), lambda hp, i: (i, hp))
    return pl.pallas_call(
        body,
        name="fox_fwd",
        grid=(PAIRS, nq),
        in_specs=[col_blk,
                  pl.BlockSpec((T, LANE), lambda hp, i: (0, QKV_BLOCKS + hp)),
                  pl.BlockSpec((T, LANE), lambda hp, i: (0, 2 * QKV_BLOCKS + hp)),
                  pl.BlockSpec((2, nbk, 1, tk), lambda hp, i: (hp, 0, 0, 0))] + r_in,
        out_specs=[col_blk, col_blk, pl.BlockSpec((1, tq, LANE), lambda hp, i: (hp, i, 0))] + r_out,
        out_shape=[jax.ShapeDtypeStruct((T, A_WIDTH), F32), jax.ShapeDtypeStruct((T, A_WIDTH), BF16),
                   jax.ShapeDtypeStruct((PAIRS, T, LANE), F32)] + r_shape,
        scratch_shapes=[pltpu.VMEM((2, tq, LANE), F32)] * 3 + [pltpu.VMEM((2, tq, tk), F32)] + r_scratch,
        compiler_params=_cparams(("arbitrary", "arbitrary")),
    )(qkv, qkv, qkv, frow, *r_args)


def _fox_delta(do, o):
    T = do.shape[0]
    tr = min(T, FOX_TQ)

    def body(do_ref, o_ref, dl_ref):
        first = _first_head((tr, LANE))
        prod = do_ref[...].astype(F32) * o_ref[...]
        dl_ref[0] = _lanes01(jnp.sum(jnp.where(first, prod, 0.0), axis=1, keepdims=True),
                             jnp.sum(jnp.where(first, 0.0, prod), axis=1, keepdims=True), tr)

    blk = pl.BlockSpec((tr, LANE), lambda hp, i: (i, hp))
    return pl.pallas_call(
        body,
        name="fox_delta",
        grid=(PAIRS, T // tr),
        in_specs=[blk, blk],
        out_specs=pl.BlockSpec((1, tr, LANE), lambda hp, i: (hp, i, 0)),
        out_shape=jax.ShapeDtypeStruct((PAIRS, T, LANE), F32),
        compiler_params=_cparams(("parallel", "parallel")),
    )(do, o)


def _fox_bwd(qkv, do, lse_row, delta_row, fpad, tq, tk, ride=None):
    T = qkv.shape[0]
    nbq = T // tq
    nbk = T // tk
    r_in, r_out, r_shape, r_scratch, r_args = _ride(ride)

    def body(*refs):
        hp = pl.program_id(0)
        j = pl.program_id(1)
        refs = _ride_steps(ride, refs, 7, 5, (hp == 0) & (j == 0), (hp == PAIRS - 1) & (j == nbk - 1))
        q_ref, k_ref, v_ref, do_ref, lse_ref, dl_ref, f_ref, dq_ref, dk_ref, dv_ref, df_ref, dfq_ref, dk_acc, dv_acc, df_acc = refs
        first = _first_head((tk, LANE))
        k2 = k_ref[...]
        ks = _split_heads(k2, first)
        ksq = _split_heads(k2 * QK_SCALE, first)
        vs = _split_heads(v_ref[...], first)
        lane = lax.broadcasted_iota(jnp.int32, (tk, LANE), 1)
        fks = tuple(jnp.sum(jnp.where(lane == 2 * hp + hh, f_ref[...], 0.0), axis=1, keepdims=True) for hh in range(2))
        dk_acc[...] = jnp.zeros_like(dk_acc)
        dv_acc[...] = jnp.zeros_like(dv_acc)
        df_acc[...] = jnp.zeros_like(df_acc)

        @pl.when(j == 0)
        def _():
            dq_ref[...] = jnp.zeros_like(dq_ref)
            dfq_ref[...] = jnp.zeros_like(dfq_ref)

        def step(i, diag):
            rows = pl.ds(pl.multiple_of(i * tq, tq), tq)
            q2 = q_ref[rows, :] * QK_SCALE
            do2 = do_ref[rows, :]
            dq_add = None
            for hh in range(2):
                st = lax.dot_general(ks[hh], q2, _NT, preferred_element_type=F32) - fks[hh]
                pt = jnp.exp(st - lse_ref[hh, i])
                if diag:
                    pt = jnp.where(_visible(i, j, tq, tk, keys_on_rows=True), pt, 0.0)
                dpt = lax.dot_general(vs[hh], do2, _NT, preferred_element_type=F32)
                dst = pt * (dpt - dl_ref[hh, i])
                dsb = dst.astype(BF16)
                dv_acc[hh] += lax.dot_general(pt.astype(BF16), do2, _NN, preferred_element_type=F32)
                dk_acc[hh] += lax.dot_general(dsb, q2, _NN, preferred_element_type=F32)
                df_acc[hh] += jnp.sum(dst, axis=1, keepdims=True)
                dfq_ref[hh, i] += jnp.sum(dst, axis=0, keepdims=True)
                part = lax.dot_general(dsb, ksq[hh], _TN, preferred_element_type=F32)
                dq_add = part if dq_add is None else dq_add + part
            dq_ref[rows, :] += dq_add

        i0 = lax.shift_right_logical(j * tk, tq.bit_length() - 1)
        nd = max(1, tk // tq)
        for d in range(nd):
            step(i0 + d, True)

        def off_diag(i, c):
            step(i, False)
            return c

        lax.fori_loop(i0 + nd, nbq, off_diag, 0)
        dk_ref[...] = jnp.where(first, dk_acc[0], dk_acc[1]).astype(BF16)
        dv_ref[...] = jnp.where(first, dv_acc[0], dv_acc[1]).astype(BF16)
        df_ref[0] = _lanes01(-df_acc[0][:, 0:1], -df_acc[1][:, 0:1], tk)

    whole = lambda cb: pl.BlockSpec((T, LANE), lambda hp, j, cb=cb: (0, cb + hp), pipeline_mode=pl.Buffered(1))
    blk = lambda cb: pl.BlockSpec((tk, LANE), lambda hp, j, cb=cb: (j, cb + hp))
    rows = pl.BlockSpec((2, nbq, 1, tq), lambda hp, j: (hp, 0, 0, 0))
    return pl.pallas_call(
        body,
        name="fox_bwd",
        grid=(PAIRS, nbk),
        in_specs=[whole(0), blk(QKV_BLOCKS), blk(2 * QKV_BLOCKS), whole(0), rows, rows, pl.BlockSpec((tk, LANE), lambda hp, j: (j, 0))] + r_in,
        out_specs=[whole(0), blk(0), blk(0), pl.BlockSpec((1, tk, LANE), lambda hp, j: (hp, j, 0)), rows] + r_out,
        out_shape=[jax.ShapeDtypeStruct((T, A_WIDTH), F32), jax.ShapeDtypeStruct((T, A_WIDTH), BF16), jax.ShapeDtypeStruct((T, A_WIDTH), BF16),
                   jax.ShapeDtypeStruct((PAIRS, T, LANE), F32), jax.ShapeDtypeStruct((A_HEADS, nbq, 1, tq), F32)] + r_shape,
        scratch_shapes=[pltpu.VMEM((2, tk, LANE), F32)] * 3 + r_scratch,
        compiler_params=_cparams(("arbitrary", "arbitrary")),
    )(qkv, qkv, qkv, do, lse_row, delta_row, fpad, *r_args)


HG_ROWS = 512
_HI = lax.Precision.HIGHEST


def _tri(n, upper):
    r = lax.broadcasted_iota(jnp.int32, (n, n), 0)
    c = lax.broadcasted_iota(jnp.int32, (n, n), 1)
    return jnp.where((r <= c) if upper else (r >= c), 1.0, 0.0).astype(F32)


def _dot(a, b, dn):
    return lax.dot_general(a.astype(BF16), b.astype(BF16), dn, preferred_element_type=F32)


def _hgrn_gates(qb, fb, lb):
    sig = jax.nn.sigmoid(fb)
    f = lb + (1.0 - lb) * sig
    sq = jax.nn.sigmoid(qb)
    return sig, f, sq, qb * sq


def _hgrn_fwd(rest, lb, col_q, col_f, col_i):
    T = rest.shape[0]
    tb = _pick(T, HG_ROWS)
    cpb = tb // CHUNK
    nblk = T // tb

    def body(q_ref, f_ref, i_ref, lb_ref, o_ref, st_ref, s_ref):
        @pl.when(pl.program_id(0) == 0)
        def _():
            s_ref[...] = jnp.zeros_like(s_ref)

        tril = _tri(CHUNK, False)
        mask = tril > 0.5

        def chunk(c, carry):
            rows = pl.ds(pl.multiple_of(c * CHUNK, CHUNK), CHUNK)
            for h in range(B_HEADS):
                cols = slice(h * B_DK, (h + 1) * B_DK)
                _, f, _, q = _hgrn_gates(q_ref[rows, cols], f_ref[rows, cols], lb_ref[:, cols])
                v = i_ref[rows, cols]
                k = 1.0 - f
                b = jnp.dot(tril, jnp.log(f), precision=_HI, preferred_element_type=F32)
                bm = b[CHUNK // 2 - 1:CHUNK // 2, :]
                bl = b[CHUNK - 1:CHUNK, :]
                st = s_ref[h]
                st_ref[h, c] = st
                p = jnp.where(mask, _dot(q * jnp.exp(b - bm), k * jnp.exp(bm - b), _NT), 0.0)
                o_ref[rows, cols] = _dot(p, v, _NN) + _dot(q * jnp.exp(b), st, _NT)
                s_ref[h] = st * jnp.exp(bl) + _dot(v, k * jnp.exp(bl - b), _TN)
            return carry

        lax.fori_loop(0, cpb, chunk, 0)

    col = lambda cb: pl.BlockSpec((tb, B_WIDTH), lambda i, cb=cb: (i, cb))
    return pl.pallas_call(
        body,
        name="hgrn_fwd",
        grid=(nblk,),
        in_specs=[col(col_q // B_WIDTH), col(col_f // B_WIDTH), col(col_i // B_WIDTH), pl.BlockSpec((1, B_WIDTH), lambda i: (0, 0))],
        out_specs=[pl.BlockSpec((tb, B_WIDTH), lambda i: (i, 0)),
                   pl.BlockSpec((B_HEADS, cpb, B_DK, B_DK), lambda i: (0, i, 0, 0))],
        out_shape=[jax.ShapeDtypeStruct((T, B_WIDTH), F32), jax.ShapeDtypeStruct((B_HEADS, T // CHUNK, B_DK, B_DK), F32)],
        scratch_shapes=[pltpu.VMEM((B_HEADS, B_DK, B_DK), F32)],
        compiler_params=_cparams(("arbitrary",)),
    )(rest, rest, rest, lb)


def _hgrn_bwd(rest, lb, states, do, col_q, col_f, col_i):
    T = rest.shape[0]
    tb = _pick(T, HG_ROWS)
    cpb = tb // CHUNK
    nblk = T // tb

    def body(q_ref, f_ref, i_ref, lb_ref, st_ref, do_ref, dq_ref, df_ref, di_ref, dlb_ref, ds_ref):
        @pl.when(pl.program_id(0) == 0)
        def _():
            ds_ref[...] = jnp.zeros_like(ds_ref)
            dlb_ref[...] = jnp.zeros_like(dlb_ref)

        tril = _tri(CHUNK, False)
        triu = _tri(CHUNK, True)
        mask = tril > 0.5

        def chunk(cc, carry):
            c = cpb - 1 - cc
            rows = pl.ds(pl.multiple_of(c * CHUNK, CHUNK), CHUNK)
            for h in range(B_HEADS):
                cols = slice(h * B_DK, (h + 1) * B_DK)
                lbv = lb_ref[:, cols]
                qb = q_ref[rows, cols]
                sig, f, sq, q = _hgrn_gates(qb, f_ref[rows, cols], lbv)
                v = i_ref[rows, cols]
                dov = do_ref[rows, cols]
                k = 1.0 - f
                b = jnp.dot(tril, jnp.log(f), precision=_HI, preferred_element_type=F32)
                bm = b[CHUNK // 2 - 1:CHUNK // 2, :]
                bl = b[CHUNK - 1:CHUNK, :]
                eb, eqm, ekm, ekl, ebl = jnp.exp(b), jnp.exp(b - bm), jnp.exp(bm - b), jnp.exp(bl - b), jnp.exp(bl)
                qt, kt, qa, kd = q * eqm, k * ekm, q * eb, k * ekl
                st0 = st_ref[h, c]
                dst1 = ds_ref[h]
                p = jnp.where(mask, _dot(qt, kt, _NT), 0.0)
                dp = jnp.where(mask, _dot(dov, v, _NT), 0.0)
                dv = _dot(p, dov, _TN) + _dot(kd, dst1, _NT)
                dqt = _dot(dp, kt, _NN)
                dkt = _dot(dp, qt, _TN)
                dqa = _dot(dov, st0, _NN)
                dkd = _dot(v, dst1, _NN)
                ds_ref[h] = _dot(dov, qa, _TN) + dst1 * ebl
                dq = dqt * eqm + dqa * eb
                dk = dkt * ekm + dkd * ekl
                rnd = lambda a: a.astype(BF16).astype(F32)
                db = dqt * rnd(qt) - dkt * rnd(kt) + dqa * qa - dkd * kd
                last = jnp.sum(dkd * kd, axis=0, keepdims=True) + ebl * jnp.sum(dst1 * st0, axis=0, keepdims=True)
                dg = jnp.dot(triu, db, precision=_HI, preferred_element_type=F32) + last
                dfv = dg / f - dk
                dq_ref[rows, cols] = (dq * (sq * (1.0 + qb * (1.0 - sq)))).astype(BF16)
                df_ref[rows, cols] = (dfv * (1.0 - lbv) * sig * (1.0 - sig)).astype(BF16)
                di_ref[rows, cols] = dv.astype(BF16)
                dlb_ref[0:1, cols] += jnp.sum(dfv * (1.0 - sig), axis=0, keepdims=True)
            return carry

        lax.fori_loop(0, cpb, chunk, 0)

    rev = lambda i: nblk - 1 - i
    col = lambda cb: pl.BlockSpec((tb, B_WIDTH), lambda i, cb=cb: (rev(i), cb))
    return pl.pallas_call(
        body,
        name="hgrn_bwd",
        grid=(nblk,),
        in_specs=[col(col_q // B_WIDTH), col(col_f // B_WIDTH), col(col_i // B_WIDTH), pl.BlockSpec((1, B_WIDTH), lambda i: (0, 0)),
                  pl.BlockSpec((B_HEADS, cpb, B_DK, B_DK), lambda i: (0, rev(i), 0, 0)), col(0)],
        out_specs=[col(0), col(0), col(0), pl.BlockSpec((8, B_WIDTH), lambda i: (0, 0))],
        out_shape=[jax.ShapeDtypeStruct((T, B_WIDTH), BF16)] * 3 + [jax.ShapeDtypeStruct((8, B_WIDTH), F32)],
        scratch_shapes=[pltpu.VMEM((B_HEADS, B_DK, B_DK), F32)],
        compiler_params=_cparams(("arbitrary",)),
    )(rest, rest, rest, lb, states, do)


SCAN_ROWS = 256


def _log_sigmoid(x):
    return jnp.minimum(x, 0.0) - jnp.log(1.0 + jnp.exp(-jnp.abs(x)))


def _head_cols(shape):
    return lax.broadcasted_iota(jnp.int32, shape, 1) < A_HEADS


def _fgate_fwd(rest, bf_pad):
    T = rest.shape[0]
    tr = _pick(T, SCAN_ROWS)

    def body(fa_ref, bf_ref, f_ref, carry_ref):
        @pl.when(pl.program_id(0) == 0)
        def _():
            carry_ref[...] = jnp.zeros_like(carry_ref)

        lf = jnp.where(_head_cols((tr, LANE)), _log_sigmoid(fa_ref[...] + bf_ref[...]), 0.0)
        f = jnp.dot(_tri(tr, False), lf, precision=_HI, preferred_element_type=F32) + carry_ref[...]
        f_ref[...] = f
        carry_ref[...] = f[tr - 1:tr, :]

    return pl.pallas_call(
        body,
        name="fgate_fwd",
        grid=(T // tr,),
        in_specs=[pl.BlockSpec((tr, LANE), lambda i: (i, C_FA // LANE)), pl.BlockSpec((1, LANE), lambda i: (0, 0))],
        out_specs=pl.BlockSpec((tr, LANE), lambda i: (i, 0)),
        out_shape=jax.ShapeDtypeStruct((T, LANE), F32),
        scratch_shapes=[pltpu.VMEM((1, LANE), F32)],
        compiler_params=_cparams(("arbitrary",)),
    )(rest, bf_pad)


def _fgate_bwd(dfq, dfk, rest, bf_pad):
    T = rest.shape[0]
    tr = _pick(T, SCAN_ROWS)
    nblk = T // tr

    def body(dq_ref, dk_ref, fa_ref, bf_ref, dfa_ref, db_ref, carry_ref):
        @pl.when(pl.program_id(0) == 0)
        def _():
            carry_ref[...] = jnp.zeros_like(carry_ref)
            db_ref[...] = jnp.zeros_like(db_ref)

        df = dq_ref[...] + dk_ref[0]
        for hp in range(1, PAIRS):
            df = df + pltpu.roll(dk_ref[hp], 2 * hp, 1)
        dlf = jnp.dot(_tri(tr, True), df, precision=_HI, preferred_element_type=F32) + carry_ref[...]
        carry_ref[...] = dlf[0:1, :]
        dfa = jnp.where(_head_cols((tr, LANE)), dlf * jax.nn.sigmoid(-(fa_ref[...] + bf_ref[...])), 0.0)
        dfa_ref[...] = dfa.astype(BF16)
        db_ref[...] += _colsum8(dfa)

    rev = lambda i: nblk - 1 - i
    return pl.pallas_call(
        body,
        name="fgate_bwd",
        grid=(nblk,),
        in_specs=[pl.BlockSpec((tr, LANE), lambda i: (rev(i), 0)), pl.BlockSpec((PAIRS, tr, LANE), lambda i: (0, rev(i), 0)),
                  pl.BlockSpec((tr, LANE), lambda i: (rev(i), C_FA // LANE)), pl.BlockSpec((1, LANE), lambda i: (0, 0))],
        out_specs=[pl.BlockSpec((tr, LANE), lambda i: (rev(i), 0)), pl.BlockSpec((8, LANE), lambda i: (0, 0))],
        out_shape=[jax.ShapeDtypeStruct((T, LANE), BF16), jax.ShapeDtypeStruct((8, LANE), F32)],
        scratch_shapes=[pltpu.VMEM((1, LANE), F32)],
        compiler_params=_cparams(("arbitrary",)),
    )(dfq, dfk, rest, bf_pad)


def _head_rows(a, t):
    return a[:, :A_HEADS].T.reshape(A_HEADS, a.shape[0] // t, 1, t)


def _pair_rows(a, t):
    return a[:, :, :2].transpose(0, 2, 1).reshape(A_HEADS, a.shape[1] // t, 1, t)


def _ln_stats(z):
    mu = jnp.mean(z, axis=-1, keepdims=True)
    zc = z - mu
    rstd = lax.rsqrt(jnp.mean(zc * zc, axis=-1, keepdims=True) + LN_EPS)
    return zc * rstd, rstd


def _ln_fwd(x, y, g, b, name):
    def fn(xv, yv, gv, bv):
        xhat, _ = _ln_stats(ALPHA * xv + yv)
        out = xhat * gv + bv
        return out, out
    return _rowwise(fn, name, [(x, 0, D_MODEL), (y, 0, D_MODEL)], [g, b], [(D_MODEL, F32), (D_MODEL, BF16)])


def _ln_bwd_core(xv, yv, gv, dy):
    xhat, rstd = _ln_stats(ALPHA * xv + yv)
    dxh = dy * gv
    dz = rstd * (dxh - jnp.mean(dxh, axis=-1, keepdims=True) - xhat * jnp.mean(dxh * xhat, axis=-1, keepdims=True))
    return dz, _colsum8(dy * xhat), _colsum8(dy)


def _ln_bwd(x, y, g, ua, ub, name):
    def fn(xv, yv, uav, ubv, gv):
        dz, dg, db = _ln_bwd_core(xv, yv, gv, ALPHA * uav + ubv)
        return dz, dz, dg, db
    return _rowwise(fn, name, [(x, 0, D_MODEL), (y, 0, D_MODEL), (ua, 0, D_MODEL), (ub, 0, D_MODEL)], [g],
                    [(D_MODEL, F32), (D_MODEL, BF16)], [D_MODEL, D_MODEL])


def _ln_bwd_loss(x, y, g, b, target, name):
    def fn(xv, yv, tv, gv, bv):
        xhat, _ = _ln_stats(ALPHA * xv + yv)
        err = xhat * gv + bv - tv
        dz, dg, db = _ln_bwd_core(xv, yv, gv, err * (1.0 / D_MODEL))
        return dz, dz, dg, db, _colsum8(err * err)
    return _rowwise(fn, name, [(x, 0, D_MODEL), (y, 0, D_MODEL), (target, 0, D_MODEL)], [g, b],
                    [(D_MODEL, F32), (D_MODEL, BF16)], [D_MODEL, D_MODEL, D_MODEL])


def _silu_parts(g):
    s = jax.nn.sigmoid(g)
    return g * s, s * (1.0 + g * (1.0 - s))


def _layer_fwd(x, xb, w, bf_pad, lb, norm_g, ln1_g, ln1_b, ln2_g, ln2_b, l, ride=None, on_ride=None):
    T = x.shape[0]
    tq, tk = min(T, FOX_TQ), min(T, FOX_TK)
    sv = {}
    qkv = _matmul(xb, w["in"], "nn", BF16, f"l{l}_qkv", n_off=0, n_size=QKV_W, tn=768)
    rest = _matmul(xb, w["in"], "nn", F32, f"l{l}_rest", n_off=QKV_W, n_size=REST_W, tn=384)
    fcum = _fgate_fwd(rest, bf_pad)
    frow = _head_rows(fcum, tk)
    o_a, ya, lse, *rode = _fox_fwd(qkv, frow, tq, tk, ride)
    if rode:
        on_ride(rode[0])
    o_b, states = _hgrn_fwd(rest, lb, C_QB, C_FB, C_IB)

    def yb_fn(ov, gbv, ngv):
        outs = []
        for h in range(B_HEADS):
            oh = ov[:, h * B_DK:(h + 1) * B_DK]
            r = lax.rsqrt(jnp.mean(oh * oh, axis=-1, keepdims=True) + RMS_EPS)
            outs.append(oh * r * ngv)
        return jnp.concatenate(outs, axis=1) * jax.nn.sigmoid(gbv)
    (yb,) = _rowwise(yb_fn, f"l{l}_yb", [(o_b, 0, B_WIDTH), (rest, C_GB, B_WIDTH)], [norm_g], [(B_WIDTH, BF16)])
    pa = _matmul(ya, w["pa"], "nn", F32, f"l{l}_pa")
    pb = _matmul(yb, w["pb"], "nn", F32, f"l{l}_pb")

    def merge_fn(gav, gbv, pav, pbv):
        return jax.nn.sigmoid(gav) * pav + jax.nn.sigmoid(gbv) * pbv
    (merged,) = _rowwise(merge_fn, f"l{l}_merge", [(rest, C_GA, D_MODEL), (rest, C_GB2, D_MODEL), (pa, 0, D_MODEL), (pb, 0, D_MODEL)],
                         [], [(D_MODEL, BF16)])
    mix = _matmul(merged, w["o"], "nn", F32, f"l{l}_mix")
    x1, x1b = _ln_fwd(x, mix, ln1_g, ln1_b, f"l{l}_ln1")
    hid = _matmul(x1b, w["ffin"], "nn", F32, f"l{l}_ffin", tn=512)

    def act_fn(uv, gv):
        return _silu_parts(gv)[0] * uv
    (act,) = _rowwise(act_fn, f"l{l}_act", [(hid, 0, FFN_HIDDEN), (hid, FFN_HIDDEN, FFN_HIDDEN)], [], [(FFN_HIDDEN, BF16)], tr=128)
    ffn = _matmul(act, w["ffout"], "nn", F32, f"l{l}_ffout", tm=1024)
    x2, x2b = _ln_fwd(x1, ffn, ln2_g, ln2_b, f"l{l}_ln2")
    sv.update(x=x, xb=xb, rest=rest, qkv=qkv, fcum=fcum, frow=frow, o_a=o_a, lse=lse, ya=ya, o_b=o_b, states=states,
              yb=yb, pa=pa, pb=pb, merged=merged, mix=mix, x1=x1, x1b=x1b, hid=hid, act=act, ffn=ffn)
    return x2, x2b, sv


def _layer_bwd(sv, w, dz2, dz2b, dln2, bf_pad, lb, norm_g, ln1_g, l, make_ride=None):
    T = sv["x"].shape[0]
    tq, tk = min(T, FOX_BWD_TQ), min(T, FOX_TK)
    gr = {}
    dact = _matmul(dz2b, w["ffout"], "nt", F32, f"l{l}_dact", tm=1024, tn=1408)
    gr["ffout"] = _matmul(sv["act"], dz2b, "tn", F32, f"l{l}_dwffout", tm=1408, tn=512, tk=2048)

    def dhid_fn(dav, uv, gv):
        sg, dsg = _silu_parts(gv)
        return jnp.concatenate([dav * sg, dav * uv * dsg], axis=1)
    (dhid,) = _rowwise(dhid_fn, f"l{l}_dhid", [(dact, 0, FFN_HIDDEN), (sv["hid"], 0, FFN_HIDDEN), (sv["hid"], FFN_HIDDEN, FFN_HIDDEN)],
                       [], [(2 * FFN_HIDDEN, BF16)], tr=128)
    dx1f = _matmul(dhid, w["ffin"], "nt", F32, f"l{l}_dx1f", tm=1024)
    gr["ffin"] = _matmul(sv["x1b"], dhid, "tn", F32, f"l{l}_dwffin", tm=1024, tn=512, tk=4096)
    dz1, dz1b, dg1, db1 = _ln_bwd(sv["x"], sv["mix"], ln1_g, dz2, dx1f, f"l{l}_dln1")
    dmerged = _matmul(dz1b, w["o"], "nt", F32, f"l{l}_dmerged")
    gr["o"] = _matmul(sv["merged"], dz1b, "tn", F32, f"l{l}_dwo", tm=1024, tn=512, tk=4096)

    def dmerge_fn(dmv, gav, gbv, pav, pbv):
        sa, sb = jax.nn.sigmoid(gav), jax.nn.sigmoid(gbv)
        return dmv * sa, dmv * sb, dmv * pav * sa * (1.0 - sa), dmv * pbv * sb * (1.0 - sb)
    dpa, dpb, dga, dgb2 = _rowwise(
        dmerge_fn, f"l{l}_dmerge",
        [(dmerged, 0, D_MODEL), (sv["rest"], C_GA, D_MODEL), (sv["rest"], C_GB2, D_MODEL), (sv["pa"], 0, D_MODEL), (sv["pb"], 0, D_MODEL)],
        [], [(D_MODEL, BF16)] * 4)
    dya = _matmul(dpa, w["pa"], "nt", BF16, f"l{l}_dya", tn=512)
    gr["pa"] = _matmul(sv["ya"], dpa, "tn", F32, f"l{l}_dwpa", tm=512, tn=512, tk=4096)
    dyb = _matmul(dpb, w["pb"], "nt", F32, f"l{l}_dyb", tn=512)
    gr["pb"] = _matmul(sv["yb"], dpb, "tn", F32, f"l{l}_dwpb", tm=512, tn=512, tk=4096)

    def dyb_fn(dyv, ov, gbv, ngv):
        sg = jax.nn.sigmoid(gbv)
        dos, dgs = [], []
        dng = jnp.zeros((8, B_DK), F32)
        for h in range(B_HEADS):
            sl = slice(h * B_DK, (h + 1) * B_DK)
            oh, dyh, sgh = ov[:, sl], dyv[:, sl], sg[:, sl]
            r = lax.rsqrt(jnp.mean(oh * oh, axis=-1, keepdims=True) + RMS_EPS)
            n = oh * r
            dyn = dyh * sgh
            dgs.append(dyh * n * ngv * sgh * (1.0 - sgh))
            dng = dng + _colsum8(dyn * n)
            dn = dyn * ngv
            dos.append(r * (dn - n * jnp.mean(dn * n, axis=-1, keepdims=True)))
        return jnp.concatenate(dos, axis=1), jnp.concatenate(dgs, axis=1), dng
    do_b, dgb, dng = _rowwise(dyb_fn, f"l{l}_dyb2", [(dyb, 0, B_WIDTH), (sv["o_b"], 0, B_WIDTH), (sv["rest"], C_GB, B_WIDTH)], [norm_g],
                              [(B_WIDTH, F32), (B_WIDTH, BF16)], [B_DK])
    dqb, dfb, dib, dlb = _hgrn_bwd(sv["rest"], lb, sv["states"], do_b, C_QB, C_FB, C_IB)
    delta = _fox_delta(dya, sv["o_a"])
    ride = (make_ride(gr), False) if make_ride else None
    dq, dk, dv, dfk, dfq, *rode = _fox_bwd(sv["qkv"], dya, _pair_rows(sv["lse"], tq), _pair_rows(delta, tq), sv["fcum"], tq, tk, ride)
    dfq_pad = jnp.pad(dfq.reshape(A_HEADS, T).T, ((0, 0), (0, LANE - A_HEADS)))
    dfa, dbf = _fgate_bwd(dfq_pad, dfk, sv["rest"], bf_pad)
    dproj = jnp.concatenate([dq.astype(BF16), dk, dv, dga, dgb2, dqb, dfb, dib, dgb, dfa], axis=1)
    dxm = _matmul(dproj, w["in"], "nt", F32, f"l{l}_dxm", tm=1024, tk=1920)
    gr["in"] = _matmul(sv["xb"], dproj, "tn", F32, f"l{l}_dwin", tm=1024, tn=640, tk=4096)
    small = dict(dln2=dln2, dln1=(dg1, db1), dng=dng, dlb=dlb, dbf=dbf)
    return dz1, dxm, gr, small, rode


MESH = pl.DeviceIdType.MESH
ANY = pl.BlockSpec(memory_space=pl.ANY)


def _me_and_peers():
    x, y, c = lax.axis_index("x"), lax.axis_index("y"), lax.axis_index("c")
    me = 4 * x + 2 * y + c
    peers = []
    for p in range(1, N_DEV):
        px, py, pc = x ^ ((p >> 2) & 1), y ^ ((p >> 1) & 1), c ^ (p & 1)
        peers.append(((px, py, pc), 4 * px + 2 * py + pc))
    return me, peers


PUSH_SCRATCH = [pltpu.SemaphoreType.DMA((N_DEV - 1,)), pltpu.SemaphoreType.DMA((N_DEV - 1,)), pltpu.SemaphoreType.DMA]


def _push_copies(gather, src_ref, out_ref, send_sems, recv_sems, local_sem):
    me, peers = _me_and_peers()
    part = (lambda idx: src_ref) if gather else (lambda idx: src_ref.at[idx])
    mine = lambda: pltpu.make_async_copy(part(me), out_ref.at[me], local_sem)
    mk = lambda p, dev, idx, slot: pltpu.make_async_remote_copy(
        src_ref=part(idx), dst_ref=out_ref.at[slot], send_sem=send_sems.at[p], recv_sem=recv_sems.at[p], device_id=dev, device_id_type=MESH)
    sends = lambda: [mk(p, dev, idx, me) for p, (dev, idx) in enumerate(peers)]
    recvs = lambda: [mk(p, dev, idx, idx) for p, (dev, idx) in enumerate(peers)]
    return mine, sends, recvs


def _push_start(gather, *refs):
    mine, sends, _ = _push_copies(gather, *refs)
    mine().start()
    for cp in sends():
        cp.start()


def _push_wait(gather, *refs):
    mine, sends, recvs = _push_copies(gather, *refs)
    for cp in recvs():
        cp.wait_recv()
    for cp in sends():
        cp.wait_send()
    mine().wait()


def _push_shape(src, gather):
    return jax.ShapeDtypeStruct(((N_DEV,) + src.shape) if gather else src.shape, src.dtype)


def _push(src, gather, name, vmem=False):
    def body(*refs):
        _push_start(gather, *refs)
        _push_wait(gather, *refs)

    spec = pl.BlockSpec(memory_space=pltpu.VMEM) if vmem else ANY
    return pl.pallas_call(body, name=name, in_specs=[spec], out_specs=spec, out_shape=_push_shape(src, gather),
                          scratch_shapes=PUSH_SCRATCH)(src)


def _sum_adam(parts, w, m, v, name, tr=ADAM_ROWS):
    _, R, W = parts.shape
    tr = _pick(R, tr) if R % LANE == 0 else R
    c1 = 1.0 / (1.0 - ADAM_B1 ** ADAM_STEP)
    c2 = 1.0 / (1.0 - ADAM_B2 ** ADAM_STEP)

    def body(p_ref, w_ref, m_ref, v_ref, g_ref, d_ref, nm_ref, nv_ref):
        g = p_ref[0]
        for s in range(1, N_DEV):
            g = g + p_ref[s]
        nm = ADAM_B1 * m_ref[...] + (1.0 - ADAM_B1) * g
        nv = ADAM_B2 * v_ref[...] + (1.0 - ADAM_B2) * (g * g)
        g_ref[...] = g
        nm_ref[...] = nm
        nv_ref[...] = nv
        d_ref[...] = -ADAM_LR * ((nm * c1) / (jnp.sqrt(nv * c2) + ADAM_EPS) + ADAM_WD * w_ref[...])

    blk = pl.BlockSpec((tr, W), lambda i: (i, 0))
    return pl.pallas_call(
        body,
        name=name,
        grid=(R // tr,),
        in_specs=[pl.BlockSpec((N_DEV, tr, W), lambda i: (0, i, 0)), blk, blk, blk],
        out_specs=[blk] * 4,
        out_shape=[jax.ShapeDtypeStruct((R, W), F32)] * 4,
        compiler_params=_cparams(("parallel",)),
    )(parts, w, m, v)


BIG = ("in", "pa", "pb", "o", "ffin", "ffout")
COL_SHARDED = {"in": True, "pa": True, "pb": True, "o": False, "ffin": True, "ffout": False}
SHARD_ROWS = {"in": 5640, "pa": 512, "pb": 512, "o": 1024, "ffin": 5632, "ffout": 2816}
_round_up = lambda n, m: -(-n // m) * m
_group_rows = lambda group: _round_up(sum(SHARD_ROWS[n] for n, _ in group), ADAM_ROWS)
GATHER_FIRST = (("in", 0),)
GATHER_RIDE = tuple((n, 0) for n in BIG[1:]) + tuple((n, 1) for n in BIG)
FIRST_ROWS, RIDE_ROWS = _group_rows(GATHER_FIRST), _group_rows(GATHER_RIDE)
EARLY = tuple((n, 1) for n in BIG) + tuple((n, 0) for n in BIG[1:])
LATE = (("in", 0),)
EARLY_ROWS, LATE_ROWS = _group_rows(EARLY), _group_rows(LATE)
SMALL_ROWS = 80


class _LayerWeights:
    def __init__(self, full, layer):
        self.full, self.layer = full, layer

    def __getitem__(self, name):
        return self.full[(name, self.layer)]


def _pack(parts, rows, lead=()):
    n = len(lead)
    cat = jnp.concatenate([p.reshape(lead + (-1, LANE)) for p in parts], axis=n)
    return jnp.pad(cat, ((0, 0),) * n + ((0, rows - cat.shape[n]), (0, 0)))


def _unpack(packed, shapes, lead=()):
    out, off = [], 0
    for s in shapes:
        r = math.prod(s) // LANE
        out.append(packed[..., off:off + r, :].reshape(lead + tuple(s)))
        off += r
    return out


def _pack_flat(parts, rows):
    flat = jnp.concatenate([p.reshape(-1) for p in parts])
    return jnp.pad(flat, (0, rows * LANE - flat.shape[0])).reshape(rows, LANE)


def _unpack_flat(packed, shapes):
    flat = packed.reshape(-1)
    out, off = [], 0
    for s in shapes:
        n = math.prod(s)
        out.append(flat[off:off + n].reshape(s))
        off += n
    return out


def _join_shards(g, col_sharded):
    if col_sharded:
        return g.transpose(1, 0, 2).reshape(g.shape[1], N_DEV * g.shape[2])
    return g.reshape(N_DEV * g.shape[1], g.shape[2])


def _split_shards(full, col_sharded):
    a, b = full.shape
    if col_sharded:
        return full.reshape(a, N_DEV, b // N_DEV).transpose(1, 0, 2)
    return full.reshape(N_DEV, a // N_DEV, b)


def _permute_in(w):
    a, b = QKV_W + A_HEADS, QKV_W + A_HEADS + 4 * B_WIDTH
    return jnp.concatenate([w[:, :QKV_W], w[:, b:], w[:, a:b], w[:, QKV_W:a],
                            jnp.zeros((w.shape[0], LANE - A_HEADS), w.dtype)], axis=1)


def _unpermute_in(d):
    g0, h0, f0 = QKV_W, QKV_W + 2 * D_MODEL, QKV_W + C_FA
    return jnp.concatenate([d[:, :QKV_W], d[:, f0:f0 + A_HEADS], d[:, h0:f0], d[:, g0:h0]], axis=1)


def _lower_bounds(logits):
    sm = jax.nn.softmax(logits.astype(F32), axis=0)
    return jnp.cumsum(sm, axis=0) - sm[0:1]


def kernel(x, w_in, b_fgate, hgrn_lb_logits, hgrn_norm_g, w_branch_a, w_branch_b, w_out, ln1_g, ln1_b, w_ff_in, w_ff_out, ln2_g, ln2_b, loss_target, m_w_in, m_b_fgate, m_hgrn_lb_logits, m_hgrn_norm_g, m_w_branch_a, m_w_branch_b, m_w_out, m_ln1_g, m_ln1_b, m_w_ff_in, m_w_ff_out, m_ln2_g, m_ln2_b, v_w_in, v_b_fgate, v_hgrn_lb_logits, v_hgrn_norm_g, v_w_branch_a, v_w_branch_b, v_w_out, v_ln1_g, v_ln1_b, v_w_ff_in, v_w_ff_out, v_ln2_g, v_ln2_b):
    big = {"in": (w_in, m_w_in, v_w_in), "pa": (w_branch_a, m_w_branch_a, v_w_branch_a), "pb": (w_branch_b, m_w_branch_b, v_w_branch_b),
           "o": (w_out, m_w_out, v_w_out), "ffin": (w_ff_in, m_w_ff_in, v_w_ff_in), "ffout": (w_ff_out, m_w_ff_out, v_w_ff_out)}
    small = [(b_fgate, m_b_fgate, v_b_fgate), (hgrn_lb_logits, m_hgrn_lb_logits, v_hgrn_lb_logits), (hgrn_norm_g, m_hgrn_norm_g, v_hgrn_norm_g),
             (ln1_g, m_ln1_g, v_ln1_g), (ln1_b, m_ln1_b, v_ln1_b), (ln2_g, m_ln2_g, v_ln2_g), (ln2_b, m_ln2_b, v_ln2_b)]
    small_shapes = [s[0].shape for s in small]
    shard_shape = lambda n: big[n][0].shape[1:]

    full = {}

    def weight_pack(group, rows):
        return _pack([big[n][0][l].astype(BF16) for n, l in group], rows)

    def take_weights(gathered, group):
        for (n, l), g in zip(group, _unpack(gathered, [shard_shape(n) for n, _ in group], lead=(N_DEV,))):
            w = _join_shards(g, COL_SHARDED[n])
            full[(n, l)] = _permute_in(w) if n == "in" else w

    take_weights(_push(weight_pack(GATHER_FIRST, FIRST_ROWS), True, "gather_weights"), GATHER_FIRST)
    weights = [_LayerWeights(full, l) for l in range(DEPTH)]
    lbounds = _lower_bounds(hgrn_lb_logits)

    def layer_params(l):
        bf_pad = jnp.pad(b_fgate[l].astype(F32), (0, LANE - A_HEADS)).reshape(1, LANE)
        row = lambda a: a[l].astype(F32).reshape(1, -1)
        return dict(bf_pad=bf_pad, lb=lbounds[l].reshape(1, -1), norm_g=row(hgrn_norm_g), ln1_g=row(ln1_g), ln1_b=row(ln1_b),
                    ln2_g=row(ln2_g), ln2_b=row(ln2_b))

    xl = x[0]
    xlb = xl.astype(BF16)
    saved = []
    for l in range(DEPTH):
        p = layer_params(l)
        ride = (weight_pack(GATHER_RIDE, RIDE_ROWS), True) if l == 0 else None
        xl, xlb, sv = _layer_fwd(xl, xlb, weights[l], p["bf_pad"], p["lb"], p["norm_g"], p["ln1_g"], p["ln1_b"],
                                 p["ln2_g"], p["ln2_b"], l, ride, lambda g: take_weights(g, GATHER_RIDE))
        saved.append(sv)

    grads = [None] * DEPTH
    smalls = [None] * DEPTH

    def contributions(group, rows):
        parts = []
        for n, l in group:
            g = _unpermute_in(grads[l][n]) if n == "in" else grads[l][n]
            parts.append(_split_shards(g, COL_SHARDED[n]))
        return _pack(parts, rows, lead=(N_DEV,))

    def early_ride(gr0):
        grads[0] = gr0
        return contributions(EARLY, EARLY_ROWS)

    loss_part = None
    ua = ub = None
    received = {}
    for l in reversed(range(DEPTH)):
        p, sv = layer_params(l), saved[l]
        if l == DEPTH - 1:
            dz2, dz2b, dg2, db2, loss_part = _ln_bwd_loss(sv["x1"], sv["ffn"], p["ln2_g"], p["ln2_b"], loss_target[0], f"l{l}_dln2")
        else:
            dz2, dz2b, dg2, db2 = _ln_bwd(sv["x1"], sv["ffn"], p["ln2_g"], ua, ub, f"l{l}_dln2")
        ua, ub, grads[l], smalls[l], rode = _layer_bwd(sv, weights[l], dz2, dz2b, (dg2, db2), p["bf_pad"], p["lb"], p["norm_g"],
                                                       p["ln1_g"], l, early_ride if l == 0 else None)
        if rode:
            received["early"] = rode[0]
    (grad_x,) = _rowwise(lambda a, b: ALPHA * a + b, "grad_x", [(ua, 0, D_MODEL), (ub, 0, D_MODEL)], [], [(D_MODEL, F32)])
    received["late"] = _push(contributions(LATE, LATE_ROWS), False, "exchange_grads")

    by_kind = [{} for _ in range(4)]
    for key, group, rows in (("early", EARLY, EARLY_ROWS), ("late", LATE, LATE_ROWS)):
        wmv = [_pack([big[n][i][l].astype(F32) for n, l in group], rows) for i in range(3)]
        shapes = [shard_shape(n) for n, _ in group]
        for kind, packed in enumerate(_sum_adam(received[key], *wmv, f"adam_{key}")):
            by_kind[kind].update(zip(group, _unpack(packed, shapes)))
    outs_big = [[jnp.stack([d[(n, l)] for l in range(DEPTH)]) for n in BIG] for d in by_kind]

    fold = lambda a: jnp.sum(a, axis=0)
    dlb = jnp.stack([fold(smalls[l]["dlb"]) for l in range(DEPTH)])
    _, lb_vjp = jax.vjp(_lower_bounds, hgrn_lb_logits)
    small_grads = [jnp.stack([fold(smalls[l]["dbf"])[:A_HEADS] for l in range(DEPTH)]), lb_vjp(dlb)[0],
                   jnp.stack([fold(smalls[l]["dng"]) for l in range(DEPTH)]),
                   jnp.stack([fold(smalls[l]["dln1"][0]) for l in range(DEPTH)]), jnp.stack([fold(smalls[l]["dln1"][1]) for l in range(DEPTH)]),
                   jnp.stack([fold(smalls[l]["dln2"][0]) for l in range(DEPTH)]), jnp.stack([fold(smalls[l]["dln2"][1]) for l in range(DEPTH)])]
    loss_local = (0.5 / D_MODEL) * jnp.sum(loss_part)
    small_all = _push(_pack_flat(small_grads + [loss_local.reshape(1)], SMALL_ROWS), True, "gather_small", vmem=True)
    one = jnp.zeros((1,), F32)
    swmv = [_pack_flat([s[i].astype(F32) for s in small] + [one], SMALL_ROWS) for i in range(3)]
    outs_small = [_unpack_flat(o, small_shapes + [(1,)]) for o in _sum_adam(small_all, *swmv, "adam_small")]
    loss = outs_small[0][-1].reshape(())

    order = {"in": 0, "pa": 4, "pb": 5, "o": 6, "ffin": 9, "ffout": 10}
    small_pos = [1, 2, 3, 7, 8, 11, 12]
    result = [loss, grad_x.reshape(x.shape)]
    for kind in range(4):
        slot = [None] * 13
        for n, arr in zip(BIG, outs_big[kind]):
            slot[order[n]] = arr
        for pos, arr in zip(small_pos, outs_small[kind][:-1]):
            slot[pos] = arr
        result.extend(slot)
    return tuple(result)
```

```python
import functools
import math

import jax
import jax.numpy as jnp
from jax import lax
from jax.experimental import pallas as pl
from jax.experimental.pallas import tpu as pltpu

F32 = jnp.float32
BF16 = jnp.bfloat16

D_MODEL = 1024
DEPTH = 2
CHUNK = 64
A_HEADS = 8
A_HEAD_DIM = 64
A_WIDTH = 512
B_WIDTH = 512
B_HEADS = 4
B_DK = 128
FFN_HIDDEN = 2816
IN_TOTAL = 5640
ALPHA = (2 * DEPTH) ** 0.25
LN_EPS = 1e-5
RMS_EPS = 1e-6
ADAM_LR = 0.001
ADAM_B1 = 0.9
ADAM_B2 = 0.999
ADAM_EPS = 1e-08
ADAM_WD = 0.01
ADAM_STEP = 10

N_DEV = 8
LANE = 128
VMEM_LIMIT = 48 * 1024 * 1024
ADAM_ROWS = 512

QKV_W = 3 * A_WIDTH
C_GA, C_GB2 = 0, D_MODEL
C_QB = 2 * D_MODEL
C_FB, C_IB, C_GB = C_QB + B_WIDTH, C_QB + 2 * B_WIDTH, C_QB + 3 * B_WIDTH
C_FA = C_QB + 4 * B_WIDTH
REST_W = C_FA + LANE
IN_PERM = QKV_W + REST_W


def _cparams(sem):
    return pltpu.CompilerParams(dimension_semantics=sem, vmem_limit_bytes=VMEM_LIMIT)


def _pick(n, pref):
    if n <= pref:
        return n
    t = pref
    while n % t:
        t -= LANE
    return t


def _matmul(a, b, mode, out_dtype, name, *, n_off=0, n_size=None, tm=2048, tn=1024, tk=2816):
    if mode == "nn":
        M, K = a.shape
        N = b.shape[1] if n_size is None else n_size
    elif mode == "nt":
        M, K = a.shape
        N = b.shape[0]
    else:
        K, M = a.shape
        N = b.shape[1]
    tm, tn, tk = _pick(M, tm), _pick(N, tn), _pick(K, tk)
    assert n_off % tn == 0
    noff = n_off // tn
    nk = K // tk

    dn = {"nn": (((1,), (0,)), ((), ())), "nt": (((1,), (1,)), ((), ())), "tn": (((0,), (0,)), ((), ()))}[mode]

    def body(a_ref, b_ref, o_ref, *scratch):
        prod = lax.dot_general(a_ref[...], b_ref[...], dn, preferred_element_type=F32)
        if nk == 1:
            o_ref[...] = prod.astype(o_ref.dtype)
            return
        (acc_ref,) = scratch
        k = pl.program_id(2)

        @pl.when(k == 0)
        def _():
            acc_ref[...] = prod

        @pl.when(k > 0)
        def _():
            acc_ref[...] += prod

        @pl.when(k == nk - 1)
        def _():
            o_ref[...] = acc_ref[...].astype(o_ref.dtype)

    if mode == "nn":
        a_spec = pl.BlockSpec((tm, tk), lambda i, j, k: (i, k))
        b_spec = pl.BlockSpec((tk, tn), lambda i, j, k: (k, j + noff))
    elif mode == "nt":
        a_spec = pl.BlockSpec((tm, tk), lambda i, j, k: (i, k))
        b_spec = pl.BlockSpec((tn, tk), lambda i, j, k: (j, k))
    else:
        a_spec = pl.BlockSpec((tk, tm), lambda i, j, k: (k, i))
        b_spec = pl.BlockSpec((tk, tn), lambda i, j, k: (k, j))
    return pl.pallas_call(
        body,
        name=name,
        grid=(M // tm, N // tn, nk),
        in_specs=[a_spec, b_spec],
        out_specs=pl.BlockSpec((tm, tn), lambda i, j, k: (i, j)),
        out_shape=jax.ShapeDtypeStruct((M, N), out_dtype),
        scratch_shapes=[pltpu.VMEM((tm, tn), F32)] if nk > 1 else [],
        compiler_params=_cparams(("parallel", "parallel", "arbitrary")),
    )(a, b)


def _rowwise(fn, name, tiled, params, outs, sums=(), *, tr=512):
    T = tiled[0][0].shape[0]
    tr = _pick(T, tr)
    nt, npar, no, ns = len(tiled), len(params), len(outs), len(sums)

    def body(*refs):
        ins = [r[...] for r in refs[:nt + npar]]
        res = fn(*ins)
        if not isinstance(res, (tuple, list)):
            res = (res,)
        o_refs = refs[nt + npar:nt + npar + no]
        s_refs = refs[nt + npar + no:]
        for r, v in zip(o_refs, res[:no]):
            r[...] = v.astype(r.dtype)
        if ns:
            @pl.when(pl.program_id(0) == 0)
            def _():
                for r in s_refs:
                    r[...] = jnp.zeros_like(r)
            for r, v in zip(s_refs, res[no:]):
                r[...] += v

    in_specs = []
    args = []
    for arr, off, w in tiled:
        assert off % w == 0 or w == arr.shape[1]
        cb = off // w
        in_specs.append(pl.BlockSpec((tr, w), lambda i, cb=cb: (i, cb)))
        args.append(arr)
    for p in params:
        in_specs.append(pl.BlockSpec(p.shape, lambda i, nd=p.ndim: (0,) * nd))
        args.append(p)
    out_specs = [pl.BlockSpec((tr, w), lambda i: (i, 0)) for w, _ in outs]
    out_shape = [jax.ShapeDtypeStruct((T, w), dt) for w, dt in outs]
    out_specs += [pl.BlockSpec((8, w), lambda i: (0, 0)) for w in sums]
    out_shape += [jax.ShapeDtypeStruct((8, w), F32) for w in sums]
    res = pl.pallas_call(
        body,
        name=name,
        grid=(T // tr,),
        in_specs=in_specs,
        out_specs=out_specs,
        out_shape=out_shape,
        compiler_params=_cparams(("arbitrary",)),
    )(*args)
    return res


def _colsum8(v):
    r, w = v.shape
    return jnp.sum(v.reshape(r // 8, 8, w), axis=0)


NEG_BIG = -1e30
QK_SCALE = A_HEAD_DIM ** -0.5
PAIRS = A_HEADS // 2
QKV_BLOCKS = A_WIDTH // LANE
FOX_TQ = 1024
FOX_TK = 1024
FOX_BWD_TQ = 512
_NT = (((1,), (1,)), ((), ()))
_NN = (((1,), (0,)), ((), ()))
_TN = (((0,), (0,)), ((), ()))


def _first_head(shape):
    return lax.broadcasted_iota(jnp.int32, shape, 1) < A_HEAD_DIM


def _split_heads(a, first):
    return jnp.where(first, a, 0), jnp.where(first, 0, a)


def _visible(i, j, tq, tk, keys_on_rows=False):
    shape = (tk, tq) if keys_on_rows else (tq, tk)
    r = lax.broadcasted_iota(jnp.int32, shape, 0)
    c = lax.broadcasted_iota(jnp.int32, shape, 1)
    return ((c - r) if keys_on_rows else (r - c)) >= j * tk - i * tq


def _lanes01(a0, a1, rows):
    lane = lax.broadcasted_iota(jnp.int32, (rows, LANE), 1)
    return jnp.where(lane == 0, a0, jnp.where(lane == 1, a1, 0.0))


def _ride(ride):
    if ride is None:
        return [], [], [], [], []
    src, gather = ride
    return [ANY], [ANY], [_push_shape(src, gather)], PUSH_SCRATCH, [src]


def _ride_steps(ride, refs, n_in, n_out, first, last):
    if ride is None:
        return refs
    src_ref, dst_ref, sems = refs[n_in], refs[n_in + 1 + n_out], refs[-3:]
    pl.when(first)(lambda: _push_start(ride[1], src_ref, dst_ref, *sems))
    pl.when(last)(lambda: _push_wait(ride[1], src_ref, dst_ref, *sems))
    return refs[:n_in] + refs[n_in + 1:n_in + 1 + n_out] + refs[n_in + 2 + n_out:-3]


def _fox_fwd(qkv, frow, tq, tk, ride=None):
    T = qkv.shape[0]
    assert tq == tk, "k blocks before q block i are fully visible, block i is the diagonal one"
    nbk = T // tk
    nq = T // tq
    r_in, r_out, r_shape, r_scratch, r_args = _ride(ride)

    def body(*refs):
        hp, i = pl.program_id(0), pl.program_id(1)
        refs = _ride_steps(ride, refs, 4, 3, (hp == 0) & (i == 0), (hp == PAIRS - 1) & (i == nq - 1))
        q_ref, k_ref, v_ref, fk_ref, o_ref, ob_ref, lse_ref, m_ref, l_ref, acc_ref, s_buf = refs
        first = _first_head((tq, LANE))
        qs = _split_heads(q_ref[...] * QK_SCALE, first)
        m_ref[...] = jnp.full_like(m_ref, NEG_BIG)
        l_ref[...] = jnp.zeros_like(l_ref)
        acc_ref[...] = jnp.zeros_like(acc_ref)

        def logits(j):
            k2 = k_ref[pl.ds(pl.multiple_of(j * tk, tk), tk), :]
            return [lax.dot_general(qs[hh], k2, _NT, preferred_element_type=F32) - fk_ref[hh, j] for hh in range(2)]

        def softmax_pv(j, ss, diag):
            v2 = v_ref[pl.ds(pl.multiple_of(j * tk, tk), tk), :]
            for hh in range(2):
                s = jnp.where(_visible(i, j, tq, tk), ss[hh], NEG_BIG) if diag else ss[hh]
                m_old = m_ref[hh]
                m_new = jnp.maximum(m_old, jnp.max(s, axis=1, keepdims=True))
                p = jnp.exp(s - m_new[:, :1])
                a = jnp.exp(m_old - m_new)
                l_ref[hh] = a * l_ref[hh] + jnp.sum(p, axis=1, keepdims=True)
                acc_ref[hh] = a * acc_ref[hh] + lax.dot_general(p.astype(BF16), v2, _NN, preferred_element_type=F32)
                m_ref[hh] = m_new

        s_buf[0], s_buf[1] = logits(0)

        def off_diag(j, c):
            cur = [s_buf[0], s_buf[1]]
            nxt = logits(j + 1)
            softmax_pv(j, cur, False)
            s_buf[0], s_buf[1] = nxt
            return c

        lax.fori_loop(0, i, off_diag, 0)
        softmax_pv(i, [s_buf[0], s_buf[1]], True)
        o = jnp.where(first, acc_ref[0] / l_ref[0], acc_ref[1] / l_ref[1])
        o_ref[...] = o
        ob_ref[...] = o.astype(BF16)
        lse_ref[0] = _lanes01((m_ref[0] + jnp.log(l_ref[0]))[:, :1], (m_ref[1] + jnp.log(l_ref[1]))[:, :1], tq)

    col_blk = pl.BlockSpec((tq, LANE), lambda hp, i: (i, hp))
    return pl.pallas_call(
        body,
        name="fox_fwd",
        grid=(PAIRS, nq),
        in_specs=[col_blk,
                  pl.BlockSpec((T, LANE), lambda hp, i: (0, QKV_BLOCKS + hp)),
                  pl.BlockSpec((T, LANE), lambda hp, i: (0, 2 * QKV_BLOCKS + hp)),
                  pl.BlockSpec((2, nbk, 1, tk), lambda hp, i: (hp, 0, 0, 0))] + r_in,
        out_specs=[col_blk, col_blk, pl.BlockSpec((1, tq, LANE), lambda hp, i: (hp, i, 0))] + r_out,
        out_shape=[jax.ShapeDtypeStruct((T, A_WIDTH), F32), jax.ShapeDtypeStruct((T, A_WIDTH), BF16),
                   jax.ShapeDtypeStruct((PAIRS, T, LANE), F32)] + r_shape,
        scratch_shapes=[pltpu.VMEM((2, tq, LANE), F32)] * 3 + [pltpu.VMEM((2, tq, tk), F32)] + r_scratch,
        compiler_params=_cparams(("arbitrary", "arbitrary")),
    )(qkv, qkv, qkv, frow, *r_args)


def _fox_delta(do, o):
    T = do.shape[0]
    tr = min(T, FOX_TQ)

    def body(do_ref, o_ref, dl_ref):
        first = _first_head((tr, LANE))
        prod = do_ref[...].astype(F32) * o_ref[...]
        dl_ref[0] = _lanes01(jnp.sum(jnp.where(first, prod, 0.0), axis=1, keepdims=True),
                             jnp.sum(jnp.where(first, 0.0, prod), axis=1, keepdims=True), tr)

    blk = pl.BlockSpec((tr, LANE), lambda hp, i: (i, hp))
    return pl.pallas_call(
        body,
        name="fox_delta",
        grid=(PAIRS, T // tr),
        in_specs=[blk, blk],
        out_specs=pl.BlockSpec((1, tr, LANE), lambda hp, i: (hp, i, 0)),
        out_shape=jax.ShapeDtypeStruct((PAIRS, T, LANE), F32),
        compiler_params=_cparams(("parallel", "parallel")),
    )(do, o)


def _fox_bwd(qkv, do, lse_row, delta_row, fpad, tq, tk, ride=None):
    T = qkv.shape[0]
    nbq = T // tq
    nbk = T // tk
    r_in, r_out, r_shape, r_scratch, r_args = _ride(ride)

    def body(*refs):
        hp = pl.program_id(0)
        j = pl.program_id(1)
        refs = _ride_steps(ride, refs, 7, 5, (hp == 0) & (j == 0), (hp == PAIRS - 1) & (j == nbk - 1))
        q_ref, k_ref, v_ref, do_ref, lse_ref, dl_ref, f_ref, dq_ref, dk_ref, dv_ref, df_ref, dfq_ref, dk_acc, dv_acc, df_acc = refs
        first = _first_head((tk, LANE))
        k2 = k_ref[...]
        ks = _split_heads(k2, first)
        ksq = _split_heads(k2 * QK_SCALE, first)
        vs = _split_heads(v_ref[...], first)
        lane = lax.broadcasted_iota(jnp.int32, (tk, LANE), 1)
        fks = tuple(jnp.sum(jnp.where(lane == 2 * hp + hh, f_ref[...], 0.0), axis=1, keepdims=True) for hh in range(2))
        dk_acc[...] = jnp.zeros_like(dk_acc)
        dv_acc[...] = jnp.zeros_like(dv_acc)
        df_acc[...] = jnp.zeros_like(df_acc)

        @pl.when(j == 0)
        def _():
            dq_ref[...] = jnp.zeros_like(dq_ref)
            dfq_ref[...] = jnp.zeros_like(dfq_ref)

        def step(i, diag):
            rows = pl.ds(pl.multiple_of(i * tq, tq), tq)
            q2 = q_ref[rows, :] * QK_SCALE
            do2 = do_ref[rows, :]
            dq_add = None
            for hh in range(2):
                st = lax.dot_general(ks[hh], q2, _NT, preferred_element_type=F32) - fks[hh]
                pt = jnp.exp(st - lse_ref[hh, i])
                if diag:
                    pt = jnp.where(_visible(i, j, tq, tk, keys_on_rows=True), pt, 0.0)
                dpt = lax.dot_general(vs[hh], do2, _NT, preferred_element_type=F32)
                dst = pt * (dpt - dl_ref[hh, i])
                dsb = dst.astype(BF16)
                dv_acc[hh] += lax.dot_general(pt.astype(BF16), do2, _NN, preferred_element_type=F32)
                dk_acc[hh] += lax.dot_general(dsb, q2, _NN, preferred_element_type=F32)
                df_acc[hh] += jnp.sum(dst, axis=1, keepdims=True)
                dfq_ref[hh, i] += jnp.sum(dst, axis=0, keepdims=True)
                part = lax.dot_general(dsb, ksq[hh], _TN, preferred_element_type=F32)
                dq_add = part if dq_add is None else dq_add + part
            dq_ref[rows, :] += dq_add

        i0 = lax.shift_right_logical(j * tk, tq.bit_length() - 1)
        nd = max(1, tk // tq)
        for d in range(nd):
            step(i0 + d, True)

        def off_diag(i, c):
            step(i, False)
            return c

        lax.fori_loop(i0 + nd, nbq, off_diag, 0)
        dk_ref[...] = jnp.where(first, dk_acc[0], dk_acc[1]).astype(BF16)
        dv_ref[...] = jnp.where(first, dv_acc[0], dv_acc[1]).astype(BF16)
        df_ref[0] = _lanes01(-df_acc[0][:, 0:1], -df_acc[1][:, 0:1], tk)

    whole = lambda cb: pl.BlockSpec((T, LANE), lambda hp, j, cb=cb: (0, cb + hp), pipeline_mode=pl.Buffered(1))
    blk = lambda cb: pl.BlockSpec((tk, LANE), lambda hp, j, cb=cb: (j, cb + hp))
    rows = pl.BlockSpec((2, nbq, 1, tq), lambda hp, j: (hp, 0, 0, 0))
    return pl.pallas_call(
        body,
        name="fox_bwd",
        grid=(PAIRS, nbk),
        in_specs=[whole(0), blk(QKV_BLOCKS), blk(2 * QKV_BLOCKS), whole(0), rows, rows, pl.BlockSpec((tk, LANE), lambda hp, j: (j, 0))] + r_in,
        out_specs=[whole(0), blk(0), blk(0), pl.BlockSpec((1, tk, LANE), lambda hp, j: (hp, j, 0)), rows] + r_out,
        out_shape=[jax.ShapeDtypeStruct((T, A_WIDTH), F32), jax.ShapeDtypeStruct((T, A_WIDTH), BF16), jax.ShapeDtypeStruct((T, A_WIDTH), BF16),
                   jax.ShapeDtypeStruct((PAIRS, T, LANE), F32), jax.ShapeDtypeStruct((A_HEADS, nbq, 1, tq), F32)] + r_shape,
        scratch_shapes=[pltpu.VMEM((2, tk, LANE), F32)] * 3 + r_scratch,
        compiler_params=_cparams(("arbitrary", "arbitrary")),
    )(qkv, qkv, qkv, do, lse_row, delta_row, fpad, *r_args)


HG_ROWS = 512


def _tri(n, upper):
    r = lax.broadcasted_iota(jnp.int32, (n, n), 0)
    c = lax.broadcasted_iota(jnp.int32, (n, n), 1)
    return jnp.where((r <= c) if upper else (r >= c), 1.0, 0.0).astype(F32)


def _tri_dot(tri, x):
    hi = x.astype(BF16)
    r1 = x - hi.astype(F32)
    mid = r1.astype(BF16)
    lo = (r1 - mid.astype(F32)).astype(BF16)
    n = x.shape[1]
    out = jnp.dot(tri.astype(BF16), jnp.concatenate([hi, mid, lo], axis=1), preferred_element_type=F32)
    return out[:, :n] + out[:, n:2 * n] + out[:, 2 * n:]


def _dot(a, b, dn):
    return lax.dot_general(a.astype(BF16), b.astype(BF16), dn, preferred_element_type=F32)


def _hgrn_gates(qb, fb, lb):
    sig = jax.nn.sigmoid(fb)
    f = lb + (1.0 - lb) * sig
    sq = jax.nn.sigmoid(qb)
    return sig, f, sq, qb * sq


def _hgrn_fwd(rest, lb, col_q, col_f, col_i):
    T = rest.shape[0]
    tb = _pick(T, HG_ROWS)
    cpb = tb // CHUNK
    nblk = T // tb

    def body(q_ref, f_ref, i_ref, lb_ref, o_ref, st_ref, s_ref):
        @pl.when(pl.program_id(0) == 0)
        def _():
            s_ref[...] = jnp.zeros_like(s_ref)

        tril = _tri(CHUNK, False)
        mask = tril > 0.5

        def chunk(c, carry):
            rows = pl.ds(pl.multiple_of(c * CHUNK, CHUNK), CHUNK)
            for h in range(B_HEADS):
                cols = slice(h * B_DK, (h + 1) * B_DK)
                _, f, _, q = _hgrn_gates(q_ref[rows, cols], f_ref[rows, cols], lb_ref[:, cols])
                v = i_ref[rows, cols]
                k = 1.0 - f
                b = _tri_dot(tril, jnp.log(f))
                bm = b[CHUNK // 2 - 1:CHUNK // 2, :]
                bl = b[CHUNK - 1:CHUNK, :]
                st = s_ref[h]
                st_ref[h, c] = st
                p = jnp.where(mask, _dot(q * jnp.exp(b - bm), k * jnp.exp(bm - b), _NT), 0.0)
                o_ref[rows, cols] = _dot(p, v, _NN) + _dot(q * jnp.exp(b), st, _NT)
                s_ref[h] = st * jnp.exp(bl) + _dot(v, k * jnp.exp(bl - b), _TN)
            return carry

        lax.fori_loop(0, cpb, chunk, 0)

    col = lambda cb: pl.BlockSpec((tb, B_WIDTH), lambda i, cb=cb: (i, cb))
    return pl.pallas_call(
        body,
        name="hgrn_fwd",
        grid=(nblk,),
        in_specs=[col(col_q // B_WIDTH), col(col_f // B_WIDTH), col(col_i // B_WIDTH), pl.BlockSpec((1, B_WIDTH), lambda i: (0, 0))],
        out_specs=[pl.BlockSpec((tb, B_WIDTH), lambda i: (i, 0)),
                   pl.BlockSpec((B_HEADS, cpb, B_DK, B_DK), lambda i: (0, i, 0, 0))],
        out_shape=[jax.ShapeDtypeStruct((T, B_WIDTH), F32), jax.ShapeDtypeStruct((B_HEADS, T // CHUNK, B_DK, B_DK), F32)],
        scratch_shapes=[pltpu.VMEM((B_HEADS, B_DK, B_DK), F32)],
        compiler_params=_cparams(("arbitrary",)),
    )(rest, rest, rest, lb)


def _hgrn_bwd(rest, lb, states, do, col_q, col_f, col_i):
    T = rest.shape[0]
    tb = _pick(T, HG_ROWS)
    cpb = tb // CHUNK
    nblk = T // tb

    def body(q_ref, f_ref, i_ref, lb_ref, st_ref, do_ref, dq_ref, df_ref, di_ref, dlb_ref, ds_ref):
        @pl.when(pl.program_id(0) == 0)
        def _():
            ds_ref[...] = jnp.zeros_like(ds_ref)
            dlb_ref[...] = jnp.zeros_like(dlb_ref)

        tril = _tri(CHUNK, False)
        triu = _tri(CHUNK, True)
        mask = tril > 0.5

        def chunk(cc, carry):
            c = cpb - 1 - cc
            rows = pl.ds(pl.multiple_of(c * CHUNK, CHUNK), CHUNK)
            for h in range(B_HEADS):
                cols = slice(h * B_DK, (h + 1) * B_DK)
                lbv = lb_ref[:, cols]
                qb = q_ref[rows, cols]
                sig, f, sq, q = _hgrn_gates(qb, f_ref[rows, cols], lbv)
                v = i_ref[rows, cols]
                dov = do_ref[rows, cols]
                k = 1.0 - f
                b = _tri_dot(tril, jnp.log(f))
                bm = b[CHUNK // 2 - 1:CHUNK // 2, :]
                bl = b[CHUNK - 1:CHUNK, :]
                eb, eqm, ekm, ekl, ebl = jnp.exp(b), jnp.exp(b - bm), jnp.exp(bm - b), jnp.exp(bl - b), jnp.exp(bl)
                qt, kt, qa, kd = q * eqm, k * ekm, q * eb, k * ekl
                st0 = st_ref[h, c]
                dst1 = ds_ref[h]
                p = jnp.where(mask, _dot(qt, kt, _NT), 0.0)
                dp = jnp.where(mask, _dot(dov, v, _NT), 0.0)
                dv = _dot(p, dov, _TN) + _dot(kd, dst1, _NT)
                dqt = _dot(dp, kt, _NN)
                dkt = _dot(dp, qt, _TN)
                dqa = _dot(dov, st0, _NN)
                dkd = _dot(v, dst1, _NN)
                ds_ref[h] = _dot(dov, qa, _TN) + dst1 * ebl
                dq = dqt * eqm + dqa * eb
                dk = dkt * ekm + dkd * ekl
                rnd = lambda a: a.astype(BF16).astype(F32)
                db = dqt * rnd(qt) - dkt * rnd(kt) + dqa * qa - dkd * kd
                last = jnp.sum(dkd * kd, axis=0, keepdims=True) + ebl * jnp.sum(dst1 * st0, axis=0, keepdims=True)
                dg = _tri_dot(triu, db) + last
                dfv = dg / f - dk
                dq_ref[rows, cols] = (dq * (sq * (1.0 + qb * (1.0 - sq)))).astype(BF16)
                df_ref[rows, cols] = (dfv * (1.0 - lbv) * sig * (1.0 - sig)).astype(BF16)
                di_ref[rows, cols] = dv.astype(BF16)
                dlb_ref[0:1, cols] += jnp.sum(dfv * (1.0 - sig), axis=0, keepdims=True)
            return carry

        lax.fori_loop(0, cpb, chunk, 0)

    rev = lambda i: nblk - 1 - i
    col = lambda cb: pl.BlockSpec((tb, B_WIDTH), lambda i, cb=cb: (rev(i), cb))
    return pl.pallas_call(
        body,
        name="hgrn_bwd",
        grid=(nblk,),
        in_specs=[col(col_q // B_WIDTH), col(col_f // B_WIDTH), col(col_i // B_WIDTH), pl.BlockSpec((1, B_WIDTH), lambda i: (0, 0)),
                  pl.BlockSpec((B_HEADS, cpb, B_DK, B_DK), lambda i: (0, rev(i), 0, 0)), col(0)],
        out_specs=[col(0), col(0), col(0), pl.BlockSpec((8, B_WIDTH), lambda i: (0, 0))],
        out_shape=[jax.ShapeDtypeStruct((T, B_WIDTH), BF16)] * 3 + [jax.ShapeDtypeStruct((8, B_WIDTH), F32)],
        scratch_shapes=[pltpu.VMEM((B_HEADS, B_DK, B_DK), F32)],
        compiler_params=_cparams(("arbitrary",)),
    )(rest, rest, rest, lb, states, do)


SCAN_ROWS = 256


def _log_sigmoid(x):
    return jnp.minimum(x, 0.0) - jnp.log(1.0 + jnp.exp(-jnp.abs(x)))


def _head_cols(shape):
    return lax.broadcasted_iota(jnp.int32, shape, 1) < A_HEADS


def _fgate_fwd(rest, bf_pad):
    T = rest.shape[0]
    tr = _pick(T, SCAN_ROWS)

    def body(fa_ref, bf_ref, f_ref, carry_ref):
        @pl.when(pl.program_id(0) == 0)
        def _():
            carry_ref[...] = jnp.zeros_like(carry_ref)

        lf = jnp.where(_head_cols((tr, LANE)), _log_sigmoid(fa_ref[...] + bf_ref[...]), 0.0)
        f = _tri_dot(_tri(tr, False), lf) + carry_ref[...]
        f_ref[...] = f
        carry_ref[...] = f[tr - 1:tr, :]

    return pl.pallas_call(
        body,
        name="fgate_fwd",
        grid=(T // tr,),
        in_specs=[pl.BlockSpec((tr, LANE), lambda i: (i, C_FA // LANE)), pl.BlockSpec((1, LANE), lambda i: (0, 0))],
        out_specs=pl.BlockSpec((tr, LANE), lambda i: (i, 0)),
        out_shape=jax.ShapeDtypeStruct((T, LANE), F32),
        scratch_shapes=[pltpu.VMEM((1, LANE), F32)],
        compiler_params=_cparams(("arbitrary",)),
    )(rest, bf_pad)


def _fgate_bwd(dfq, dfk, rest, bf_pad):
    T = rest.shape[0]
    tr = _pick(T, SCAN_ROWS)
    nblk = T // tr

    def body(dq_ref, dk_ref, fa_ref, bf_ref, dfa_ref, db_ref, carry_ref):
        @pl.when(pl.program_id(0) == 0)
        def _():
            carry_ref[...] = jnp.zeros_like(carry_ref)
            db_ref[...] = jnp.zeros_like(db_ref)

        df = dq_ref[...] + dk_ref[0]
        for hp in range(1, PAIRS):
            df = df + pltpu.roll(dk_ref[hp], 2 * hp, 1)
        dlf = _tri_dot(_tri(tr, True), df) + carry_ref[...]
        carry_ref[...] = dlf[0:1, :]
        dfa = jnp.where(_head_cols((tr, LANE)), dlf * jax.nn.sigmoid(-(fa_ref[...] + bf_ref[...])), 0.0)
        dfa_ref[...] = dfa.astype(BF16)
        db_ref[...] += _colsum8(dfa)

    rev = lambda i: nblk - 1 - i
    return pl.pallas_call(
        body,
        name="fgate_bwd",
        grid=(nblk,),
        in_specs=[pl.BlockSpec((tr, LANE), lambda i: (rev(i), 0)), pl.BlockSpec((PAIRS, tr, LANE), lambda i: (0, rev(i), 0)),
                  pl.BlockSpec((tr, LANE), lambda i: (rev(i), C_FA // LANE)), pl.BlockSpec((1, LANE), lambda i: (0, 0))],
        out_specs=[pl.BlockSpec((tr, LANE), lambda i: (rev(i), 0)), pl.BlockSpec((8, LANE), lambda i: (0, 0))],
        out_shape=[jax.ShapeDtypeStruct((T, LANE), BF16), jax.ShapeDtypeStruct((8, LANE), F32)],
        scratch_shapes=[pltpu.VMEM((1, LANE), F32)],
        compiler_params=_cparams(("arbitrary",)),
    )(dfq, dfk, rest, bf_pad)


def _head_rows(a, t):
    return a[:, :A_HEADS].T.reshape(A_HEADS, a.shape[0] // t, 1, t)


def _pair_rows(a, t):
    return a[:, :, :2].transpose(0, 2, 1).reshape(A_HEADS, a.shape[1] // t, 1, t)


def _ln_stats(z):
    mu = jnp.mean(z, axis=-1, keepdims=True)
    zc = z - mu
    rstd = lax.rsqrt(jnp.mean(zc * zc, axis=-1, keepdims=True) + LN_EPS)
    return zc * rstd, rstd


def _ln_fwd(x, y, g, b, name):
    def fn(xv, yv, gv, bv):
        xhat, _ = _ln_stats(ALPHA * xv + yv)
        out = xhat * gv + bv
        return out, out
    return _rowwise(fn, name, [(x, 0, D_MODEL), (y, 0, D_MODEL)], [g, b], [(D_MODEL, F32), (D_MODEL, BF16)])


def _ln_bwd_core(xv, yv, gv, dy):
    xhat, rstd = _ln_stats(ALPHA * xv + yv)
    dxh = dy * gv
    dz = rstd * (dxh - jnp.mean(dxh, axis=-1, keepdims=True) - xhat * jnp.mean(dxh * xhat, axis=-1, keepdims=True))
    return dz, _colsum8(dy * xhat), _colsum8(dy)


def _ln_bwd(x, y, g, ua, ub, name):
    def fn(xv, yv, uav, ubv, gv):
        dz, dg, db = _ln_bwd_core(xv, yv, gv, ALPHA * uav + ubv)
        return dz, dz, dg, db
    return _rowwise(fn, name, [(x, 0, D_MODEL), (y, 0, D_MODEL), (ua, 0, D_MODEL), (ub, 0, D_MODEL)], [g],
                    [(D_MODEL, F32), (D_MODEL, BF16)], [D_MODEL, D_MODEL])


def _ln_bwd_loss(x, y, g, b, target, name):
    def fn(xv, yv, tv, gv, bv):
        xhat, _ = _ln_stats(ALPHA * xv + yv)
        err = xhat * gv + bv - tv
        dz, dg, db = _ln_bwd_core(xv, yv, gv, err * (1.0 / D_MODEL))
        return dz, dz, dg, db, _colsum8(err * err)
    return _rowwise(fn, name, [(x, 0, D_MODEL), (y, 0, D_MODEL), (target, 0, D_MODEL)], [g, b],
                    [(D_MODEL, F32), (D_MODEL, BF16)], [D_MODEL, D_MODEL, D_MODEL])


def _silu_parts(g):
    s = jax.nn.sigmoid(g)
    return g * s, s * (1.0 + g * (1.0 - s))


def _layer_fwd(x, xb, w, bf_pad, lb, norm_g, ln1_g, ln1_b, ln2_g, ln2_b, l, ride=None, on_ride=None):
    T = x.shape[0]
    tq, tk = min(T, FOX_TQ), min(T, FOX_TK)
    sv = {}
    qkv = _matmul(xb, w["in"], "nn", BF16, f"l{l}_qkv", n_off=0, n_size=QKV_W, tn=768)
    rest = _matmul(xb, w["in"], "nn", F32, f"l{l}_rest", n_off=QKV_W, n_size=REST_W, tn=384)
    fcum = _fgate_fwd(rest, bf_pad)
    frow = _head_rows(fcum, tk)
    o_a, ya, lse, *rode = _fox_fwd(qkv, frow, tq, tk, ride)
    if rode:
        on_ride(rode[0])
    o_b, states = _hgrn_fwd(rest, lb, C_QB, C_FB, C_IB)

    def yb_fn(ov, gbv, ngv):
        outs = []
        for h in range(B_HEADS):
            oh = ov[:, h * B_DK:(h + 1) * B_DK]
            r = lax.rsqrt(jnp.mean(oh * oh, axis=-1, keepdims=True) + RMS_EPS)
            outs.append(oh * r * ngv)
        return jnp.concatenate(outs, axis=1) * jax.nn.sigmoid(gbv)
    (yb,) = _rowwise(yb_fn, f"l{l}_yb", [(o_b, 0, B_WIDTH), (rest, C_GB, B_WIDTH)], [norm_g], [(B_WIDTH, BF16)])
    pa = _matmul(ya, w["pa"], "nn", F32, f"l{l}_pa")
    pb = _matmul(yb, w["pb"], "nn", F32, f"l{l}_pb")

    def merge_fn(gav, gbv, pav, pbv):
        return jax.nn.sigmoid(gav) * pav + jax.nn.sigmoid(gbv) * pbv
    (merged,) = _rowwise(merge_fn, f"l{l}_merge", [(rest, C_GA, D_MODEL), (rest, C_GB2, D_MODEL), (pa, 0, D_MODEL), (pb, 0, D_MODEL)],
                         [], [(D_MODEL, BF16)])
    mix = _matmul(merged, w["o"], "nn", F32, f"l{l}_mix")
    x1, x1b = _ln_fwd(x, mix, ln1_g, ln1_b, f"l{l}_ln1")
    hid = _matmul(x1b, w["ffin"], "nn", F32, f"l{l}_ffin", tn=512)

    def act_fn(uv, gv):
        return _silu_parts(gv)[0] * uv
    (act,) = _rowwise(act_fn, f"l{l}_act", [(hid, 0, FFN_HIDDEN), (hid, FFN_HIDDEN, FFN_HIDDEN)], [], [(FFN_HIDDEN, BF16)], tr=256)
    ffn = _matmul(act, w["ffout"], "nn", F32, f"l{l}_ffout", tm=1024)
    x2, x2b = _ln_fwd(x1, ffn, ln2_g, ln2_b, f"l{l}_ln2")
    sv.update(x=x, xb=xb, rest=rest, qkv=qkv, fcum=fcum, frow=frow, o_a=o_a, lse=lse, ya=ya, o_b=o_b, states=states,
              yb=yb, pa=pa, pb=pb, merged=merged, mix=mix, x1=x1, x1b=x1b, hid=hid, act=act, ffn=ffn)
    return x2, x2b, sv


def _layer_bwd(sv, w, dz2, dz2b, dln2, bf_pad, lb, norm_g, ln1_g, l, make_ride=None):
    T = sv["x"].shape[0]
    tq, tk = min(T, FOX_BWD_TQ), min(T, FOX_TK)
    gr = {}
    dact = _matmul(dz2b, w["ffout"], "nt", F32, f"l{l}_dact", tm=1024, tn=1408)
    gr["ffout"] = _matmul(sv["act"], dz2b, "tn", F32, f"l{l}_dwffout", tm=1408, tn=512, tk=2048)

    def dhid_fn(dav, uv, gv):
        sg, dsg = _silu_parts(gv)
        return jnp.concatenate([dav * sg, dav * uv * dsg], axis=1)
    (dhid,) = _rowwise(dhid_fn, f"l{l}_dhid", [(dact, 0, FFN_HIDDEN), (sv["hid"], 0, FFN_HIDDEN), (sv["hid"], FFN_HIDDEN, FFN_HIDDEN)],
                       [], [(2 * FFN_HIDDEN, BF16)], tr=256)
    dx1f = _matmul(dhid, w["ffin"], "nt", F32, f"l{l}_dx1f", tm=1024)
    gr["ffin"] = _matmul(sv["x1b"], dhid, "tn", F32, f"l{l}_dwffin", tm=1024, tn=512, tk=4096)
    dz1, dz1b, dg1, db1 = _ln_bwd(sv["x"], sv["mix"], ln1_g, dz2, dx1f, f"l{l}_dln1")
    dmerged = _matmul(dz1b, w["o"], "nt", F32, f"l{l}_dmerged")
    gr["o"] = _matmul(sv["merged"], dz1b, "tn", F32, f"l{l}_dwo", tm=1024, tn=512, tk=4096)

    def dmerge_fn(dmv, gav, gbv, pav, pbv):
        sa, sb = jax.nn.sigmoid(gav), jax.nn.sigmoid(gbv)
        return dmv * sa, dmv * sb, dmv * pav * sa * (1.0 - sa), dmv * pbv * sb * (1.0 - sb)
    dpa, dpb, dga, dgb2 = _rowwise(
        dmerge_fn, f"l{l}_dmerge",
        [(dmerged, 0, D_MODEL), (sv["rest"], C_GA, D_MODEL), (sv["rest"], C_GB2, D_MODEL), (sv["pa"], 0, D_MODEL), (sv["pb"], 0, D_MODEL)],
        [], [(D_MODEL, BF16)] * 4)
    dya = _matmul(dpa, w["pa"], "nt", BF16, f"l{l}_dya", tn=512)
    gr["pa"] = _matmul(sv["ya"], dpa, "tn", F32, f"l{l}_dwpa", tm=512, tn=512, tk=4096)
    dyb = _matmul(dpb, w["pb"], "nt", F32, f"l{l}_dyb", tn=512)
    gr["pb"] = _matmul(sv["yb"], dpb, "tn", F32, f"l{l}_dwpb", tm=512, tn=512, tk=4096)

    def dyb_fn(dyv, ov, gbv, ngv):
        sg = jax.nn.sigmoid(gbv)
        dos, dgs = [], []
        dng = jnp.zeros((8, B_DK), F32)
        for h in range(B_HEADS):
            sl = slice(h * B_DK, (h + 1) * B_DK)
            oh, dyh, sgh = ov[:, sl], dyv[:, sl], sg[:, sl]
            r = lax.rsqrt(jnp.mean(oh * oh, axis=-1, keepdims=True) + RMS_EPS)
            n = oh * r
            dyn = dyh * sgh
            dgs.append(dyh * n * ngv * sgh * (1.0 - sgh))
            dng = dng + _colsum8(dyn * n)
            dn = dyn * ngv
            dos.append(r * (dn - n * jnp.mean(dn * n, axis=-1, keepdims=True)))
        return jnp.concatenate(dos, axis=1), jnp.concatenate(dgs, axis=1), dng
    do_b, dgb, dng = _rowwise(dyb_fn, f"l{l}_dyb2", [(dyb, 0, B_WIDTH), (sv["o_b"], 0, B_WIDTH), (sv["rest"], C_GB, B_WIDTH)], [norm_g],
                              [(B_WIDTH, F32), (B_WIDTH, BF16)], [B_DK])
    dqb, dfb, dib, dlb = _hgrn_bwd(sv["rest"], lb, sv["states"], do_b, C_QB, C_FB, C_IB)
    delta = _fox_delta(dya, sv["o_a"])
    ride = (make_ride(gr), False) if make_ride else None
    dq, dk, dv, dfk, dfq, *rode = _fox_bwd(sv["qkv"], dya, _pair_rows(sv["lse"], tq), _pair_rows(delta, tq), sv["fcum"], tq, tk, ride)
    dfq_pad = jnp.pad(dfq.reshape(A_HEADS, T).T, ((0, 0), (0, LANE - A_HEADS)))
    dfa, dbf = _fgate_bwd(dfq_pad, dfk, sv["rest"], bf_pad)
    dproj = jnp.concatenate([dq.astype(BF16), dk, dv, dga, dgb2, dqb, dfb, dib, dgb, dfa], axis=1)
    dxm = _matmul(dproj, w["in"], "nt", F32, f"l{l}_dxm", tm=1024, tk=1920)
    gr["in"] = _matmul(sv["xb"], dproj, "tn", F32, f"l{l}_dwin", tm=1024, tn=640, tk=4096)
    small = dict(dln2=dln2, dln1=(dg1, db1), dng=dng, dlb=dlb, dbf=dbf)
    return dz1, dxm, gr, small, rode


MESH = pl.DeviceIdType.MESH
ANY = pl.BlockSpec(memory_space=pl.ANY)


def _me_and_peers():
    x, y, c = lax.axis_index("x"), lax.axis_index("y"), lax.axis_index("c")
    me = 4 * x + 2 * y + c
    peers = []
    for p in range(1, N_DEV):
        px, py, pc = x ^ ((p >> 2) & 1), y ^ ((p >> 1) & 1), c ^ (p & 1)
        peers.append(((px, py, pc), 4 * px + 2 * py + pc))
    return me, peers


PUSH_SCRATCH = [pltpu.SemaphoreType.DMA((N_DEV - 1,)), pltpu.SemaphoreType.DMA((N_DEV - 1,)), pltpu.SemaphoreType.DMA]


def _push_copies(gather, src_ref, out_ref, send_sems, recv_sems, local_sem):
    me, peers = _me_and_peers()
    part = (lambda idx: src_ref) if gather else (lambda idx: src_ref.at[idx])
    mine = lambda: pltpu.make_async_copy(part(me), out_ref.at[me], local_sem)
    mk = lambda p, dev, idx, slot: pltpu.make_async_remote_copy(
        src_ref=part(idx), dst_ref=out_ref.at[slot], send_sem=send_sems.at[p], recv_sem=recv_sems.at[p], device_id=dev, device_id_type=MESH)
    sends = lambda: [mk(p, dev, idx, me) for p, (dev, idx) in enumerate(peers)]
    recvs = lambda: [mk(p, dev, idx, idx) for p, (dev, idx) in enumerate(peers)]
    return mine, sends, recvs


def _push_start(gather, *refs):
    mine, sends, _ = _push_copies(gather, *refs)
    mine().start()
    for cp in sends():
        cp.start()


def _push_wait(gather, *refs):
    mine, sends, recvs = _push_copies(gather, *refs)
    for cp in recvs():
        cp.wait_recv()
    for cp in sends():
        cp.wait_send()
    mine().wait()


def _push_shape(src, gather):
    return jax.ShapeDtypeStruct(((N_DEV,) + src.shape) if gather else src.shape, src.dtype)


def _push(src, gather, name, vmem=False):
    def body(*refs):
        _push_start(gather, *refs)
        _push_wait(gather, *refs)

    spec = pl.BlockSpec(memory_space=pltpu.VMEM) if vmem else ANY
    return pl.pallas_call(body, name=name, in_specs=[spec], out_specs=spec, out_shape=_push_shape(src, gather),
                          scratch_shapes=PUSH_SCRATCH)(src)


def _sum_adam(parts, w, m, v, name, tr=ADAM_ROWS):
    _, R, W = parts.shape
    tr = _pick(R, tr) if R % LANE == 0 else R
    c1 = 1.0 / (1.0 - ADAM_B1 ** ADAM_STEP)
    c2 = 1.0 / (1.0 - ADAM_B2 ** ADAM_STEP)

    def body(p_ref, w_ref, m_ref, v_ref, g_ref, d_ref, nm_ref, nv_ref):
        g = p_ref[0]
        for s in range(1, N_DEV):
            g = g + p_ref[s]
        nm = ADAM_B1 * m_ref[...] + (1.0 - ADAM_B1) * g
        nv = ADAM_B2 * v_ref[...] + (1.0 - ADAM_B2) * (g * g)
        g_ref[...] = g
        nm_ref[...] = nm
        nv_ref[...] = nv
        d_ref[...] = -ADAM_LR * ((nm * c1) / (jnp.sqrt(nv * c2) + ADAM_EPS) + ADAM_WD * w_ref[...])

    blk = pl.BlockSpec((tr, W), lambda i: (i, 0))
    return pl.pallas_call(
        body,
        name=name,
        grid=(R // tr,),
        in_specs=[pl.BlockSpec((N_DEV, tr, W), lambda i: (0, i, 0)), blk, blk, blk],
        out_specs=[blk] * 4,
        out_shape=[jax.ShapeDtypeStruct((R, W), F32)] * 4,
        compiler_params=_cparams(("parallel",)),
    )(parts, w, m, v)


BIG = ("in", "pa", "pb", "o", "ffin", "ffout")
COL_SHARDED = {"in": True, "pa": True, "pb": True, "o": False, "ffin": True, "ffout": False}
SHARD_ROWS = {"in": 5640, "pa": 512, "pb": 512, "o": 1024, "ffin": 5632, "ffout": 2816}
_round_up = lambda n, m: -(-n // m) * m
_group_rows = lambda group: _round_up(sum(SHARD_ROWS[n] for n, _ in group), ADAM_ROWS)
GATHER_FIRST = (("in", 0),)
GATHER_RIDE = tuple((n, 0) for n in BIG[1:]) + tuple((n, 1) for n in BIG)
FIRST_ROWS, RIDE_ROWS = _group_rows(GATHER_FIRST), _group_rows(GATHER_RIDE)
EARLY = tuple((n, 1) for n in BIG) + tuple((n, 0) for n in BIG[1:])
LATE = (("in", 0),)
EARLY_ROWS, LATE_ROWS = _group_rows(EARLY), _group_rows(LATE)
SMALL_ROWS = 80


class _LayerWeights:
    def __init__(self, full, layer):
        self.full, self.layer = full, layer

    def __getitem__(self, name):
        return self.full[(name, self.layer)]


def _pack(parts, rows, lead=()):
    n = len(lead)
    cat = jnp.concatenate([p.reshape(lead + (-1, LANE)) for p in parts], axis=n)
    return jnp.pad(cat, ((0, 0),) * n + ((0, rows - cat.shape[n]), (0, 0)))


def _unpack(packed, shapes, lead=()):
    out, off = [], 0
    for s in shapes:
        r = math.prod(s) // LANE
        out.append(packed[..., off:off + r, :].reshape(lead + tuple(s)))
        off += r
    return out


def _pack_flat(parts, rows):
    flat = jnp.concatenate([p.reshape(-1) for p in parts])
    return jnp.pad(flat, (0, rows * LANE - flat.shape[0])).reshape(rows, LANE)


def _unpack_flat(packed, shapes):
    flat = packed.reshape(-1)
    out, off = [], 0
    for s in shapes:
        n = math.prod(s)
        out.append(flat[off:off + n].reshape(s))
        off += n
    return out


def _join_shards(g, col_sharded):
    if col_sharded:
        return g.transpose(1, 0, 2).reshape(g.shape[1], N_DEV * g.shape[2])
    return g.reshape(N_DEV * g.shape[1], g.shape[2])


def _split_shards(full, col_sharded):
    a, b = full.shape
    if col_sharded:
        return full.reshape(a, N_DEV, b // N_DEV).transpose(1, 0, 2)
    return full.reshape(N_DEV, a // N_DEV, b)


def _permute_in(w):
    a, b = QKV_W + A_HEADS, QKV_W + A_HEADS + 4 * B_WIDTH
    return jnp.concatenate([w[:, :QKV_W], w[:, b:], w[:, a:b], w[:, QKV_W:a],
                            jnp.zeros((w.shape[0], LANE - A_HEADS), w.dtype)], axis=1)


def _unpermute_in(d):
    g0, h0, f0 = QKV_W, QKV_W + 2 * D_MODEL, QKV_W + C_FA
    return jnp.concatenate([d[:, :QKV_W], d[:, f0:f0 + A_HEADS], d[:, h0:f0], d[:, g0:h0]], axis=1)


def _lower_bounds(logits):
    sm = jax.nn.softmax(logits.astype(F32), axis=0)
    return jnp.cumsum(sm, axis=0) - sm[0:1]


def kernel(x, w_in, b_fgate, hgrn_lb_logits, hgrn_norm_g, w_branch_a, w_branch_b, w_out, ln1_g, ln1_b, w_ff_in, w_ff_out, ln2_g, ln2_b, loss_target, m_w_in, m_b_fgate, m_hgrn_lb_logits, m_hgrn_norm_g, m_w_branch_a, m_w_branch_b, m_w_out, m_ln1_g, m_ln1_b, m_w_ff_in, m_w_ff_out, m_ln2_g, m_ln2_b, v_w_in, v_b_fgate, v_hgrn_lb_logits, v_hgrn_norm_g, v_w_branch_a, v_w_branch_b, v_w_out, v_ln1_g, v_ln1_b, v_w_ff_in, v_w_ff_out, v_ln2_g, v_ln2_b):
    big = {"in": (w_in, m_w_in, v_w_in), "pa": (w_branch_a, m_w_branch_a, v_w_branch_a), "pb": (w_branch_b, m_w_branch_b, v_w_branch_b),
           "o": (w_out, m_w_out, v_w_out), "ffin": (w_ff_in, m_w_ff_in, v_w_ff_in), "ffout": (w_ff_out, m_w_ff_out, v_w_ff_out)}
    small = [(b_fgate, m_b_fgate, v_b_fgate), (hgrn_lb_logits, m_hgrn_lb_logits, v_hgrn_lb_logits), (hgrn_norm_g, m_hgrn_norm_g, v_hgrn_norm_g),
             (ln1_g, m_ln1_g, v_ln1_g), (ln1_b, m_ln1_b, v_ln1_b), (ln2_g, m_ln2_g, v_ln2_g), (ln2_b, m_ln2_b, v_ln2_b)]
    small_shapes = [s[0].shape for s in small]
    shard_shape = lambda n: big[n][0].shape[1:]

    full = {}

    def weight_pack(group, rows):
        return _pack([big[n][0][l].astype(BF16) for n, l in group], rows)

    def take_weights(gathered, group):
        for (n, l), g in zip(group, _unpack(gathered, [shard_shape(n) for n, _ in group], lead=(N_DEV,))):
            w = _join_shards(g, COL_SHARDED[n])
            full[(n, l)] = _permute_in(w) if n == "in" else w

    take_weights(_push(weight_pack(GATHER_FIRST, FIRST_ROWS), True, "gather_weights"), GATHER_FIRST)
    weights = [_LayerWeights(full, l) for l in range(DEPTH)]
    lbounds = _lower_bounds(hgrn_lb_logits)

    def layer_params(l):
        bf_pad = jnp.pad(b_fgate[l].astype(F32), (0, LANE - A_HEADS)).reshape(1, LANE)
        row = lambda a: a[l].astype(F32).reshape(1, -1)
        return dict(bf_pad=bf_pad, lb=lbounds[l].reshape(1, -1), norm_g=row(hgrn_norm_g), ln1_g=row(ln1_g), ln1_b=row(ln1_b),
                    ln2_g=row(ln2_g), ln2_b=row(ln2_b))

    xl = x[0]
    xlb = xl.astype(BF16)
    saved = []
    for l in range(DEPTH):
        p = layer_params(l)
        ride = (weight_pack(GATHER_RIDE, RIDE_ROWS), True) if l == 0 else None
        xl, xlb, sv = _layer_fwd(xl, xlb, weights[l], p["bf_pad"], p["lb"], p["norm_g"], p["ln1_g"], p["ln1_b"],
                                 p["ln2_g"], p["ln2_b"], l, ride, lambda g: take_weights(g, GATHER_RIDE))
        saved.append(sv)

    grads = [None] * DEPTH
    smalls = [None] * DEPTH

    def contributions(group, rows):
        parts = []
        for n, l in group:
            g = _unpermute_in(grads[l][n]) if n == "in" else grads[l][n]
            parts.append(_split_shards(g, COL_SHARDED[n]))
        return _pack(parts, rows, lead=(N_DEV,))

    def early_ride(gr0):
        grads[0] = gr0
        return contributions(EARLY, EARLY_ROWS)

    loss_part = None
    ua = ub = None
    received = {}
    for l in reversed(range(DEPTH)):
        p, sv = layer_params(l), saved[l]
        if l == DEPTH - 1:
            dz2, dz2b, dg2, db2, loss_part = _ln_bwd_loss(sv["x1"], sv["ffn"], p["ln2_g"], p["ln2_b"], loss_target[0], f"l{l}_dln2")
        else:
            dz2, dz2b, dg2, db2 = _ln_bwd(sv["x1"], sv["ffn"], p["ln2_g"], ua, ub, f"l{l}_dln2")
        ua, ub, grads[l], smalls[l], rode = _layer_bwd(sv, weights[l], dz2, dz2b, (dg2, db2), p["bf_pad"], p["lb"], p["norm_g"],
                                                       p["ln1_g"], l, early_ride if l == 0 else None)
        if rode:
            received["early"] = rode[0]
    (grad_x,) = _rowwise(lambda a, b: ALPHA * a + b, "grad_x", [(ua, 0, D_MODEL), (ub, 0, D_MODEL)], [], [(D_MODEL, F32)])
    received["late"] = _push(contributions(LATE, LATE_ROWS), False, "exchange_grads")

    by_kind = [{} for _ in range(4)]
    for key, group, rows in (("early", EARLY, EARLY_ROWS), ("late", LATE, LATE_ROWS)):
        wmv = [_pack([big[n][i][l].astype(F32) for n, l in group], rows) for i in range(3)]
        shapes = [shard_shape(n) for n, _ in group]
        for kind, packed in enumerate(_sum_adam(received[key], *wmv, f"adam_{key}")):
            by_kind[kind].update(zip(group, _unpack(packed, shapes)))
    outs_big = [[jnp.stack([d[(n, l)] for l in range(DEPTH)]) for n in BIG] for d in by_kind]

    fold = lambda a: jnp.sum(a, axis=0)
    dlb = jnp.stack([fold(smalls[l]["dlb"]) for l in range(DEPTH)])
    _, lb_vjp = jax.vjp(_lower_bounds, hgrn_lb_logits)
    small_grads = [jnp.stack([fold(smalls[l]["dbf"])[:A_HEADS] for l in range(DEPTH)]), lb_vjp(dlb)[0],
                   jnp.stack([fold(smalls[l]["dng"]) for l in range(DEPTH)]),
                   jnp.stack([fold(smalls[l]["dln1"][0]) for l in range(DEPTH)]), jnp.stack([fold(smalls[l]["dln1"][1]) for l in range(DEPTH)]),
                   jnp.stack([fold(smalls[l]["dln2"][0]) for l in range(DEPTH)]), jnp.stack([fold(smalls[l]["dln2"][1]) for l in range(DEPTH)])]
    loss_local = (0.5 / D_MODEL) * jnp.sum(loss_part)
    small_all = _push(_pack_flat(small_grads + [loss_local.reshape(1)], SMALL_ROWS), True, "gather_small", vmem=True)
    one = jnp.zeros((1,), F32)
    swmv = [_pack_flat([s[i].astype(F32) for s in small] + [one], SMALL_ROWS) for i in range(3)]
    outs_small = [_unpack_flat(o, small_shapes + [(1,)]) for o in _sum_adam(small_all, *swmv, "adam_small")]
    loss = outs_small[0][-1].reshape(())

    order = {"in": 0, "pa": 4, "pb": 5, "o": 6, "ffin": 9, "ffout": 10}
    small_pos = [1, 2, 3, 7, 8, 11, 12]
    result = [loss, grad_x.reshape(x.shape)]
    for kind in range(4):
        slot = [None] * 13
        for n, arr in zip(BIG, outs_big[kind]):
            slot[order[n]] = arr
        for pos, arr in zip(small_pos, outs_small[kind][:-1]):
            slot[pos] = arr
        result.extend(slot)
    return tuple(result)
```

```python
import functools
import math

import jax
import jax.numpy as jnp
from jax import lax
from jax.experimental import pallas as pl
from jax.experimental.pallas import tpu as pltpu

F32 = jnp.float32
BF16 = jnp.bfloat16

D_MODEL = 1024
DEPTH = 2
CHUNK = 64
A_HEADS = 8
A_HEAD_DIM = 64
A_WIDTH = 512
B_WIDTH = 512
B_HEADS = 4
B_DK = 128
FFN_HIDDEN = 2816
IN_TOTAL = 5640
ALPHA = (2 * DEPTH) ** 0.25
LN_EPS = 1e-5
RMS_EPS = 1e-6
ADAM_LR = 0.001
ADAM_B1 = 0.9
ADAM_B2 = 0.999
ADAM_EPS = 1e-08
ADAM_WD = 0.01
ADAM_STEP = 10

N_DEV = 8
LANE = 128
VMEM_LIMIT = 48 * 1024 * 1024
ADAM_ROWS = 512

QKV_W = 3 * A_WIDTH
C_GA, C_GB2 = 0, D_MODEL
C_QB = 2 * D_MODEL
C_FB, C_IB, C_GB = C_QB + B_WIDTH, C_QB + 2 * B_WIDTH, C_QB + 3 * B_WIDTH
C_FA = C_QB + 4 * B_WIDTH
REST_W = C_FA + LANE
IN_PERM = QKV_W + REST_W


def _cparams(sem):
    return pltpu.CompilerParams(dimension_semantics=sem, vmem_limit_bytes=VMEM_LIMIT)


def _pick(n, pref):
    if n <= pref:
        return n
    t = pref
    while n % t:
        t -= LANE
    return t


def _matmul(a, b, mode, out_dtype, name, *, n_off=0, n_size=None, tm=2048, tn=1024, tk=2816, ride=None):
    if mode == "nn":
        M, K = a.shape
        N = b.shape[1] if n_size is None else n_size
    elif mode == "nt":
        M, K = a.shape
        N = b.shape[0]
    else:
        K, M = a.shape
        N = b.shape[1]
    tm, tn, tk = _pick(M, tm), _pick(N, tn), _pick(K, tk)
    assert n_off % tn == 0
    noff = n_off // tn
    nk = K // tk

    dn = {"nn": (((1,), (0,)), ((), ())), "nt": (((1,), (1,)), ((), ())), "tn": (((0,), (0,)), ((), ()))}[mode]

    ni, nj = M // tm, N // tn
    r_in, r_out, r_shape, r_scratch, r_args = _ride(ride)
    assert ride is None or ni * nj * nk > 1

    def body(*refs):
        i, j, kk = pl.program_id(0), pl.program_id(1), pl.program_id(2)
        refs = _ride_steps(ride, refs, 2, 1, (i == 0) & (j == 0) & (kk == 0), (i == ni - 1) & (j == nj - 1) & (kk == nk - 1))
        a_ref, b_ref, o_ref, *scratch = refs
        prod = lax.dot_general(a_ref[...], b_ref[...], dn, preferred_element_type=F32)
        if nk == 1:
            o_ref[...] = prod.astype(o_ref.dtype)
            return
        (acc_ref,) = scratch
        k = pl.program_id(2)

        @pl.when(k == 0)
        def _():
            acc_ref[...] = prod

        @pl.when(k > 0)
        def _():
            acc_ref[...] += prod

        @pl.when(k == nk - 1)
        def _():
            o_ref[...] = acc_ref[...].astype(o_ref.dtype)

    if mode == "nn":
        a_spec = pl.BlockSpec((tm, tk), lambda i, j, k: (i, k))
        b_spec = pl.BlockSpec((tk, tn), lambda i, j, k: (k, j + noff))
    elif mode == "nt":
        a_spec = pl.BlockSpec((tm, tk), lambda i, j, k: (i, k))
        b_spec = pl.BlockSpec((tn, tk), lambda i, j, k: (j, k))
    else:
        a_spec = pl.BlockSpec((tk, tm), lambda i, j, k: (k, i))
        b_spec = pl.BlockSpec((tk, tn), lambda i, j, k: (k, j))
    res = pl.pallas_call(
        body,
        name=name,
        grid=(ni, nj, nk),
        in_specs=[a_spec, b_spec] + r_in,
        out_specs=[pl.BlockSpec((tm, tn), lambda i, j, k: (i, j))] + r_out,
        out_shape=[jax.ShapeDtypeStruct((M, N), out_dtype)] + r_shape,
        scratch_shapes=([pltpu.VMEM((tm, tn), F32)] if nk > 1 else []) + r_scratch,
        compiler_params=_cparams(("parallel", "parallel", "arbitrary") if ride is None else ("arbitrary",) * 3),
    )(a, b, *r_args)
    return res[0] if ride is None else tuple(res)


def _rowwise(fn, name, tiled, params, outs, sums=(), *, tr=512):
    T = tiled[0][0].shape[0]
    tr = _pick(T, tr)
    nt, npar, no, ns = len(tiled), len(params), len(outs), len(sums)

    def body(*refs):
        ins = [r[...] for r in refs[:nt + npar]]
        res = fn(*ins)
        if not isinstance(res, (tuple, list)):
            res = (res,)
        o_refs = refs[nt + npar:nt + npar + no]
        s_refs = refs[nt + npar + no:]
        for r, v in zip(o_refs, res[:no]):
            r[...] = v.astype(r.dtype)
        if ns:
            @pl.when(pl.program_id(0) == 0)
            def _():
                for r in s_refs:
                    r[...] = jnp.zeros_like(r)
            for r, v in zip(s_refs, res[no:]):
                r[...] += v

    in_specs = []
    args = []
    for arr, off, w in tiled:
        assert off % w == 0 or w == arr.shape[1]
        cb = off // w
        in_specs.append(pl.BlockSpec((tr, w), lambda i, cb=cb: (i, cb)))
        args.append(arr)
    for p in params:
        in_specs.append(pl.BlockSpec(p.shape, lambda i, nd=p.ndim: (0,) * nd))
        args.append(p)
    out_specs = [pl.BlockSpec((tr, w), lambda i: (i, 0)) for w, _ in outs]
    out_shape = [jax.ShapeDtypeStruct((T, w), dt) for w, dt in outs]
    out_specs += [pl.BlockSpec((8, w), lambda i: (0, 0)) for w in sums]
    out_shape += [jax.ShapeDtypeStruct((8, w), F32) for w in sums]
    res = pl.pallas_call(
        body,
        name=name,
        grid=(T // tr,),
        in_specs=in_specs,
        out_specs=out_specs,
        out_shape=out_shape,
        compiler_params=_cparams(("arbitrary",)),
    )(*args)
    return res


def _colsum8(v):
    r, w = v.shape
    return jnp.sum(v.reshape(r // 8, 8, w), axis=0)


NEG_BIG = -1e30
QK_SCALE = A_HEAD_DIM ** -0.5
PAIRS = A_HEADS // 2
QKV_BLOCKS = A_WIDTH // LANE
FOX_TQ = 1024
FOX_TK = 1024
FOX_BWD_TQ = 512
_NT = (((1,), (1,)), ((), ()))
_NN = (((1,), (0,)), ((), ()))
_TN = (((0,), (0,)), ((), ()))


def _first_head(shape):
    return lax.broadcasted_iota(jnp.int32, shape, 1) < A_HEAD_DIM


def _split_heads(a, first):
    return jnp.where(first, a, 0), jnp.where(first, 0, a)


def _visible(i, j, tq, tk, keys_on_rows=False):
    shape = (tk, tq) if keys_on_rows else (tq, tk)
    r = lax.broadcasted_iota(jnp.int32, shape, 0)
    c = lax.broadcasted_iota(jnp.int32, shape, 1)
    return ((c - r) if keys_on_rows else (r - c)) >= j * tk - i * tq


def _lanes01(a0, a1, rows):
    lane = lax.broadcasted_iota(jnp.int32, (rows, LANE), 1)
    return jnp.where(lane == 0, a0, jnp.where(lane == 1, a1, 0.0))


def _ride(ride):
    if ride is None:
        return [], [], [], [], []
    src, gather = ride
    return [ANY], [ANY], [_push_shape(src, gather)], PUSH_SCRATCH, [src]


def _ride_steps(ride, refs, n_in, n_out, first, last):
    if ride is None:
        return refs
    src_ref, dst_ref, sems = refs[n_in], refs[n_in + 1 + n_out], refs[-3:]
    pl.when(first)(lambda: _push_start(ride[1], src_ref, dst_ref, *sems))
    pl.when(last)(lambda: _push_wait(ride[1], src_ref, dst_ref, *sems))
    return refs[:n_in] + refs[n_in + 1:n_in + 1 + n_out] + refs[n_in + 2 + n_out:-3]


def _fox_fwd(qkv, frow, tq, tk, ride=None):
    T = qkv.shape[0]
    assert tq == tk, "k blocks before q block i are fully visible, block i is the diagonal one"
    nbk = T // tk
    nq = T // tq
    r_in, r_out, r_shape, r_scratch, r_args = _ride(ride)

    def body(*refs):
        hp, i = pl.program_id(0), pl.program_id(1)
        refs = _ride_steps(ride, refs, 4, 3, (hp == 0) & (i == 0), (hp == PAIRS - 1) & (i == nq - 1))
        q_ref, k_ref, v_ref, fk_ref, o_ref, ob_ref, lse_ref, m_ref, l_ref, acc_ref, s_buf = refs
        first = _first_head((tq, LANE))
        qs = _split_heads(q_ref[...] * QK_SCALE, first)
        m_ref[...] = jnp.full_like(m_ref, NEG_BIG)
        l_ref[...] = jnp.zeros_like(l_ref)
        acc_ref[...] = jnp.zeros_like(acc_ref)

        def logits(j):
            k2 = k_ref[pl.ds(pl.multiple_of(j * tk, tk), tk), :]
            return [lax.dot_general(qs[hh], k2, _NT, preferred_element_type=F32) - fk_ref[hh, j] for hh in range(2)]

        def softmax_pv(j, ss, diag):
            v2 = v_ref[pl.ds(pl.multiple_of(j * tk, tk), tk), :]
            for hh in range(2):
                s = jnp.where(_visible(i, j, tq, tk), ss[hh], NEG_BIG) if diag else ss[hh]
                m_old = m_ref[hh]
                m_new = jnp.maximum(m_old, jnp.max(s, axis=1, keepdims=True))
                p = jnp.exp(s - m_new[:, :1])
                a = jnp.exp(m_old - m_new)
                l_ref[hh] = a * l_ref[hh] + jnp.sum(p, axis=1, keepdims=True)
                acc_ref[hh] = a * acc_ref[hh] + lax.dot_general(p.astype(BF16), v2, _NN, preferred_element_type=F32)
                m_ref[hh] = m_new

        s_buf[0], s_buf[1] = logits(0)

        def off_diag(j, c):
            cur = [s_buf[0], s_buf[1]]
            nxt = logits(j + 1)
            softmax_pv(j, cur, False)
            s_buf[0], s_buf[1] = nxt
            return c

        lax.fori_loop(0, i, off_diag, 0)
        softmax_pv(i, [s_buf[0], s_buf[1]], True)
        o = jnp.where(first, acc_ref[0] / l_ref[0], acc_ref[1] / l_ref[1])
        o_ref[...] = o
        ob_ref[...] = o.astype(BF16)
        lse_ref[0] = _lanes01((m_ref[0] + jnp.log(l_ref[0]))[:, :1], (m_ref[1] + jnp.log(l_ref[1]))[:, :1], tq)

    col_blk = pl.BlockSpec((tq, LANE), lambda hp, i: (i, hp))
    return pl.pallas_call(
        body,
        name="fox_fwd",
        grid=(PAIRS, nq),
        in_specs=[col_blk,
                  pl.BlockSpec((T, LANE), lambda hp, i: (0, QKV_BLOCKS + hp)),
                  pl.BlockSpec((T, LANE), lambda hp, i: (0, 2 * QKV_BLOCKS + hp)),
                  pl.BlockSpec((2, nbk, 1, tk), lambda hp, i: (hp, 0, 0, 0))] + r_in,
        out_specs=[col_blk, col_blk, pl.BlockSpec((1, tq, LANE), lambda hp, i: (hp, i, 0))] + r_out,
        out_shape=[jax.ShapeDtypeStruct((T, A_WIDTH), F32), jax.ShapeDtypeStruct((T, A_WIDTH), BF16),
                   jax.ShapeDtypeStruct((PAIRS, T, LANE), F32)] + r_shape,
        scratch_shapes=[pltpu.VMEM((2, tq, LANE), F32)] * 3 + [pltpu.VMEM((2, tq, tk), F32)] + r_scratch,
        compiler_params=_cparams(("arbitrary", "arbitrary")),
    )(qkv, qkv, qkv, frow, *r_args)


def _fox_delta(do, o):
    T = do.shape[0]
    tr = min(T, FOX_TQ)

    def body(do_ref, o_ref, dl_ref):
        first = _first_head((tr, LANE))
        prod = do_ref[...].astype(F32) * o_ref[...]
        dl_ref[0] = _lanes01(jnp.sum(jnp.where(first, prod, 0.0), axis=1, keepdims=True),
                             jnp.sum(jnp.where(first, 0.0, prod), axis=1, keepdims=True), tr)

    blk = pl.BlockSpec((tr, LANE), lambda hp, i: (i, hp))
    return pl.pallas_call(
        body,
        name="fox_delta",
        grid=(PAIRS, T // tr),
        in_specs=[blk, blk],
        out_specs=pl.BlockSpec((1, tr, LANE), lambda hp, i: (hp, i, 0)),
        out_shape=jax.ShapeDtypeStruct((PAIRS, T, LANE), F32),
        compiler_params=_cparams(("parallel", "parallel")),
    )(do, o)


def _fox_bwd(qkv, do, lse_row, delta_row, fpad, tq, tk, ride=None):
    T = qkv.shape[0]
    nbq = T // tq
    nbk = T // tk
    r_in, r_out, r_shape, r_scratch, r_args = _ride(ride)

    def body(*refs):
        hp = pl.program_id(0)
        j = pl.program_id(1)
        refs = _ride_steps(ride, refs, 7, 5, (hp == 0) & (j == 0), (hp == PAIRS - 1) & (j == nbk - 1))
        q_ref, k_ref, v_ref, do_ref, lse_ref, dl_ref, f_ref, dq_ref, dk_ref, dv_ref, df_ref, dfq_ref, dk_acc, dv_acc, df_acc = refs
        first = _first_head((tk, LANE))
        k2 = k_ref[...]
        ks = _split_heads(k2, first)
        ksq = _split_heads(k2 * QK_SCALE, first)
        vs = _split_heads(v_ref[...], first)
        lane = lax.broadcasted_iota(jnp.int32, (tk, LANE), 1)
        fks = tuple(jnp.sum(jnp.where(lane == 2 * hp + hh, f_ref[...], 0.0), axis=1, keepdims=True) for hh in range(2))
        dk_acc[...] = jnp.zeros_like(dk_acc)
        dv_acc[...] = jnp.zeros_like(dv_acc)
        df_acc[...] = jnp.zeros_like(df_acc)

        @pl.when(j == 0)
        def _():
            dq_ref[...] = jnp.zeros_like(dq_ref)
            dfq_ref[...] = jnp.zeros_like(dfq_ref)

        def step(i, diag):
            rows = pl.ds(pl.multiple_of(i * tq, tq), tq)
            q2 = q_ref[rows, :] * QK_SCALE
            do2 = do_ref[rows, :]
            dq_add = None
            for hh in range(2):
                st = lax.dot_general(ks[hh], q2, _NT, preferred_element_type=F32) - fks[hh]
                pt = jnp.exp(st - lse_ref[hh, i])
                if diag:
                    pt = jnp.where(_visible(i, j, tq, tk, keys_on_rows=True), pt, 0.0)
                dpt = lax.dot_general(vs[hh], do2, _NT, preferred_element_type=F32)
                dst = pt * (dpt - dl_ref[hh, i])
                dsb = dst.astype(BF16)
                dv_acc[hh] += lax.dot_general(pt.astype(BF16), do2, _NN, preferred_element_type=F32)
                dk_acc[hh] += lax.dot_general(dsb, q2, _NN, preferred_element_type=F32)
                df_acc[hh] += jnp.sum(dst, axis=1, keepdims=True)
                dfq_ref[hh, i] += jnp.sum(dst, axis=0, keepdims=True)
                part = lax.dot_general(dsb, ksq[hh], _TN, preferred_element_type=F32)
                dq_add = part if dq_add is None else dq_add + part
            dq_ref[rows, :] += dq_add

        i0 = lax.shift_right_logical(j * tk, tq.bit_length() - 1)
        nd = max(1, tk // tq)
        for d in range(nd):
            step(i0 + d, True)

        def off_diag(i, c):
            step(i, False)
            return c

        lax.fori_loop(i0 + nd, nbq, off_diag, 0)
        dk_ref[...] = jnp.where(first, dk_acc[0], dk_acc[1]).astype(BF16)
        dv_ref[...] = jnp.where(first, dv_acc[0], dv_acc[1]).astype(BF16)
        df_ref[0] = _lanes01(-df_acc[0][:, 0:1], -df_acc[1][:, 0:1], tk)

    whole = lambda cb: pl.BlockSpec((T, LANE), lambda hp, j, cb=cb: (0, cb + hp), pipeline_mode=pl.Buffered(1))
    blk = lambda cb: pl.BlockSpec((tk, LANE), lambda hp, j, cb=cb: (j, cb + hp))
    rows = pl.BlockSpec((2, nbq, 1, tq), lambda hp, j: (hp, 0, 0, 0))
    return pl.pallas_call(
        body,
        name="fox_bwd",
        grid=(PAIRS, nbk),
        in_specs=[whole(0), blk(QKV_BLOCKS), blk(2 * QKV_BLOCKS), whole(0), rows, rows, pl.BlockSpec((tk, LANE), lambda hp, j: (j, 0))] + r_in,
        out_specs=[whole(0), blk(0), blk(0), pl.BlockSpec((1, tk, LANE), lambda hp, j: (hp, j, 0)), rows] + r_out,
        out_shape=[jax.ShapeDtypeStruct((T, A_WIDTH), F32), jax.ShapeDtypeStruct((T, A_WIDTH), BF16), jax.ShapeDtypeStruct((T, A_WIDTH), BF16),
                   jax.ShapeDtypeStruct((PAIRS, T, LANE), F32), jax.ShapeDtypeStruct((A_HEADS, nbq, 1, tq), F32)] + r_shape,
        scratch_shapes=[pltpu.VMEM((2, tk, LANE), F32)] * 3 + r_scratch,
        compiler_params=_cparams(("arbitrary", "arbitrary")),
    )(qkv, qkv, qkv, do, lse_row, delta_row, fpad, *r_args)


HG_ROWS = 512


def _tri(n, upper):
    r = lax.broadcasted_iota(jnp.int32, (n, n), 0)
    c = lax.broadcasted_iota(jnp.int32, (n, n), 1)
    return jnp.where((r <= c) if upper else (r >= c), 1.0, 0.0).astype(F32)


def _tri_dot(tri, x):
    hi = x.astype(BF16)
    r1 = x - hi.astype(F32)
    mid = r1.astype(BF16)
    lo = (r1 - mid.astype(F32)).astype(BF16)
    n = x.shape[1]
    out = jnp.dot(tri.astype(BF16), jnp.concatenate([hi, mid, lo], axis=1), preferred_element_type=F32)
    return out[:, :n] + out[:, n:2 * n] + out[:, 2 * n:]


def _dot(a, b, dn):
    return lax.dot_general(a.astype(BF16), b.astype(BF16), dn, preferred_element_type=F32)


def _hgrn_gates(qb, fb, lb):
    sig = jax.nn.sigmoid(fb)
    f = lb + (1.0 - lb) * sig
    sq = jax.nn.sigmoid(qb)
    return sig, f, sq, qb * sq


def _hgrn_fwd(rest, lb, col_q, col_f, col_i):
    T = rest.shape[0]
    tb = _pick(T, HG_ROWS)
    cpb = tb // CHUNK
    nblk = T // tb

    def body(q_ref, f_ref, i_ref, lb_ref, o_ref, st_ref, s_ref):
        @pl.when(pl.program_id(0) == 0)
        def _():
            s_ref[...] = jnp.zeros_like(s_ref)

        tril = _tri(CHUNK, False)
        mask = tril > 0.5

        def chunk(c, carry):
            rows = pl.ds(pl.multiple_of(c * CHUNK, CHUNK), CHUNK)
            for h in range(B_HEADS):
                cols = slice(h * B_DK, (h + 1) * B_DK)
                _, f, _, q = _hgrn_gates(q_ref[rows, cols], f_ref[rows, cols], lb_ref[:, cols])
                v = i_ref[rows, cols]
                k = 1.0 - f
                b = _tri_dot(tril, jnp.log(f))
                bm = b[CHUNK // 2 - 1:CHUNK // 2, :]
                bl = b[CHUNK - 1:CHUNK, :]
                st = s_ref[h]
                st_ref[h, c] = st
                p = jnp.where(mask, _dot(q * jnp.exp(b - bm), k * jnp.exp(bm - b), _NT), 0.0)
                o_ref[rows, cols] = _dot(p, v, _NN) + _dot(q * jnp.exp(b), st, _NT)
                s_ref[h] = st * jnp.exp(bl) + _dot(v, k * jnp.exp(bl - b), _TN)
            return carry

        lax.fori_loop(0, cpb, chunk, 0)

    col = lambda cb: pl.BlockSpec((tb, B_WIDTH), lambda i, cb=cb: (i, cb))
    return pl.pallas_call(
        body,
        name="hgrn_fwd",
        grid=(nblk,),
        in_specs=[col(col_q // B_WIDTH), col(col_f // B_WIDTH), col(col_i // B_WIDTH), pl.BlockSpec((1, B_WIDTH), lambda i: (0, 0))],
        out_specs=[pl.BlockSpec((tb, B_WIDTH), lambda i: (i, 0)),
                   pl.BlockSpec((B_HEADS, cpb, B_DK, B_DK), lambda i: (0, i, 0, 0))],
        out_shape=[jax.ShapeDtypeStruct((T, B_WIDTH), F32), jax.ShapeDtypeStruct((B_HEADS, T // CHUNK, B_DK, B_DK), F32)],
        scratch_shapes=[pltpu.VMEM((B_HEADS, B_DK, B_DK), F32)],
        compiler_params=_cparams(("arbitrary",)),
    )(rest, rest, rest, lb)


def _hgrn_bwd(rest, lb, states, do, col_q, col_f, col_i):
    T = rest.shape[0]
    tb = _pick(T, HG_ROWS)
    cpb = tb // CHUNK
    nblk = T // tb

    def body(q_ref, f_ref, i_ref, lb_ref, st_ref, do_ref, dq_ref, df_ref, di_ref, dlb_ref, ds_ref):
        @pl.when(pl.program_id(0) == 0)
        def _():
            ds_ref[...] = jnp.zeros_like(ds_ref)
            dlb_ref[...] = jnp.zeros_like(dlb_ref)

        tril = _tri(CHUNK, False)
        triu = _tri(CHUNK, True)
        mask = tril > 0.5

        def chunk(cc, carry):
            c = cpb - 1 - cc
            rows = pl.ds(pl.multiple_of(c * CHUNK, CHUNK), CHUNK)
            for h in range(B_HEADS):
                cols = slice(h * B_DK, (h + 1) * B_DK)
                lbv = lb_ref[:, cols]
                qb = q_ref[rows, cols]
                sig, f, sq, q = _hgrn_gates(qb, f_ref[rows, cols], lbv)
                v = i_ref[rows, cols]
                dov = do_ref[rows, cols]
                k = 1.0 - f
                b = _tri_dot(tril, jnp.log(f))
                bm = b[CHUNK // 2 - 1:CHUNK // 2, :]
                bl = b[CHUNK - 1:CHUNK, :]
                eb, eqm, ekm, ekl, ebl = jnp.exp(b), jnp.exp(b - bm), jnp.exp(bm - b), jnp.exp(bl - b), jnp.exp(bl)
                qt, kt, qa, kd = q * eqm, k * ekm, q * eb, k * ekl
                st0 = st_ref[h, c]
                dst1 = ds_ref[h]
                p = jnp.where(mask, _dot(qt, kt, _NT), 0.0)
                dp = jnp.where(mask, _dot(dov, v, _NT), 0.0)
                dv = _dot(p, dov, _TN) + _dot(kd, dst1, _NT)
                dqt = _dot(dp, kt, _NN)
                dkt = _dot(dp, qt, _TN)
                dqa = _dot(dov, st0, _NN)
                dkd = _dot(v, dst1, _NN)
                ds_ref[h] = _dot(dov, qa, _TN) + dst1 * ebl
                dq = dqt * eqm + dqa * eb
                dk = dkt * ekm + dkd * ekl
                rnd = lambda a: a.astype(BF16).astype(F32)
                db = dqt * rnd(qt) - dkt * rnd(kt) + dqa * qa - dkd * kd
                last = jnp.sum(dkd * kd, axis=0, keepdims=True) + ebl * jnp.sum(dst1 * st0, axis=0, keepdims=True)
                dg = _tri_dot(triu, db) + last
                dfv = dg / f - dk
                dq_ref[rows, cols] = (dq * (sq * (1.0 + qb * (1.0 - sq)))).astype(BF16)
                df_ref[rows, cols] = (dfv * (1.0 - lbv) * sig * (1.0 - sig)).astype(BF16)
                di_ref[rows, cols] = dv.astype(BF16)
                dlb_ref[0:1, cols] += jnp.sum(dfv * (1.0 - sig), axis=0, keepdims=True)
            return carry

        lax.fori_loop(0, cpb, chunk, 0)

    rev = lambda i: nblk - 1 - i
    col = lambda cb: pl.BlockSpec((tb, B_WIDTH), lambda i, cb=cb: (rev(i), cb))
    return pl.pallas_call(
        body,
        name="hgrn_bwd",
        grid=(nblk,),
        in_specs=[col(col_q // B_WIDTH), col(col_f // B_WIDTH), col(col_i // B_WIDTH), pl.BlockSpec((1, B_WIDTH), lambda i: (0, 0)),
                  pl.BlockSpec((B_HEADS, cpb, B_DK, B_DK), lambda i: (0, rev(i), 0, 0)), col(0)],
        out_specs=[col(0), col(0), col(0), pl.BlockSpec((8, B_WIDTH), lambda i: (0, 0))],
        out_shape=[jax.ShapeDtypeStruct((T, B_WIDTH), BF16)] * 3 + [jax.ShapeDtypeStruct((8, B_WIDTH), F32)],
        scratch_shapes=[pltpu.VMEM((B_HEADS, B_DK, B_DK), F32)],
        compiler_params=_cparams(("arbitrary",)),
    )(rest, rest, rest, lb, states, do)


SCAN_ROWS = 256


def _log_sigmoid(x):
    return jnp.minimum(x, 0.0) - jnp.log(1.0 + jnp.exp(-jnp.abs(x)))


def _head_cols(shape):
    return lax.broadcasted_iota(jnp.int32, shape, 1) < A_HEADS


def _fgate_fwd(rest, bf_pad):
    T = rest.shape[0]
    tr = _pick(T, SCAN_ROWS)

    def body(fa_ref, bf_ref, f_ref, carry_ref):
        @pl.when(pl.program_id(0) == 0)
        def _():
            carry_ref[...] = jnp.zeros_like(carry_ref)

        lf = jnp.where(_head_cols((tr, LANE)), _log_sigmoid(fa_ref[...] + bf_ref[...]), 0.0)
        f = _tri_dot(_tri(tr, False), lf) + carry_ref[...]
        f_ref[...] = f
        carry_ref[...] = f[tr - 1:tr, :]

    return pl.pallas_call(
        body,
        name="fgate_fwd",
        grid=(T // tr,),
        in_specs=[pl.BlockSpec((tr, LANE), lambda i: (i, C_FA // LANE)), pl.BlockSpec((1, LANE), lambda i: (0, 0))],
        out_specs=pl.BlockSpec((tr, LANE), lambda i: (i, 0)),
        out_shape=jax.ShapeDtypeStruct((T, LANE), F32),
        scratch_shapes=[pltpu.VMEM((1, LANE), F32)],
        compiler_params=_cparams(("arbitrary",)),
    )(rest, bf_pad)


def _fgate_bwd(dfq, dfk, rest, bf_pad):
    T = rest.shape[0]
    tr = _pick(T, SCAN_ROWS)
    nblk = T // tr

    def body(dq_ref, dk_ref, fa_ref, bf_ref, dfa_ref, db_ref, carry_ref):
        @pl.when(pl.program_id(0) == 0)
        def _():
            carry_ref[...] = jnp.zeros_like(carry_ref)
            db_ref[...] = jnp.zeros_like(db_ref)

        df = dq_ref[...] + dk_ref[0]
        for hp in range(1, PAIRS):
            df = df + pltpu.roll(dk_ref[hp], 2 * hp, 1)
        dlf = _tri_dot(_tri(tr, True), df) + carry_ref[...]
        carry_ref[...] = dlf[0:1, :]
        dfa = jnp.where(_head_cols((tr, LANE)), dlf * jax.nn.sigmoid(-(fa_ref[...] + bf_ref[...])), 0.0)
        dfa_ref[...] = dfa.astype(BF16)
        db_ref[...] += _colsum8(dfa)

    rev = lambda i: nblk - 1 - i
    return pl.pallas_call(
        body,
        name="fgate_bwd",
        grid=(nblk,),
        in_specs=[pl.BlockSpec((tr, LANE), lambda i: (rev(i), 0)), pl.BlockSpec((PAIRS, tr, LANE), lambda i: (0, rev(i), 0)),
                  pl.BlockSpec((tr, LANE), lambda i: (rev(i), C_FA // LANE)), pl.BlockSpec((1, LANE), lambda i: (0, 0))],
        out_specs=[pl.BlockSpec((tr, LANE), lambda i: (rev(i), 0)), pl.BlockSpec((8, LANE), lambda i: (0, 0))],
        out_shape=[jax.ShapeDtypeStruct((T, LANE), BF16), jax.ShapeDtypeStruct((8, LANE), F32)],
        scratch_shapes=[pltpu.VMEM((1, LANE), F32)],
        compiler_params=_cparams(("arbitrary",)),
    )(dfq, dfk, rest, bf_pad)


def _head_rows(a, t):
    return a[:, :A_HEADS].T.reshape(A_HEADS, a.shape[0] // t, 1, t)


def _pair_rows(a, t):
    return a[:, :, :2].transpose(0, 2, 1).reshape(A_HEADS, a.shape[1] // t, 1, t)


def _ln_stats(z):
    mu = jnp.mean(z, axis=-1, keepdims=True)
    zc = z - mu
    rstd = lax.rsqrt(jnp.mean(zc * zc, axis=-1, keepdims=True) + LN_EPS)
    return zc * rstd, rstd


def _ln_fwd(x, y, g, b, name):
    def fn(xv, yv, gv, bv):
        xhat, _ = _ln_stats(ALPHA * xv + yv)
        out = xhat * gv + bv
        return out, out
    return _rowwise(fn, name, [(x, 0, D_MODEL), (y, 0, D_MODEL)], [g, b], [(D_MODEL, F32), (D_MODEL, BF16)])


def _ln_bwd_core(xv, yv, gv, dy):
    xhat, rstd = _ln_stats(ALPHA * xv + yv)
    dxh = dy * gv
    dz = rstd * (dxh - jnp.mean(dxh, axis=-1, keepdims=True) - xhat * jnp.mean(dxh * xhat, axis=-1, keepdims=True))
    return dz, _colsum8(dy * xhat), _colsum8(dy)


def _ln_bwd(x, y, g, ua, ub, name):
    def fn(xv, yv, uav, ubv, gv):
        dz, dg, db = _ln_bwd_core(xv, yv, gv, ALPHA * uav + ubv)
        return dz, dz, dg, db
    return _rowwise(fn, name, [(x, 0, D_MODEL), (y, 0, D_MODEL), (ua, 0, D_MODEL), (ub, 0, D_MODEL)], [g],
                    [(D_MODEL, F32), (D_MODEL, BF16)], [D_MODEL, D_MODEL])


def _ln_bwd_loss(x, y, g, b, target, name):
    def fn(xv, yv, tv, gv, bv):
        xhat, _ = _ln_stats(ALPHA * xv + yv)
        err = xhat * gv + bv - tv
        dz, dg, db = _ln_bwd_core(xv, yv, gv, err * (1.0 / D_MODEL))
        return dz, dz, dg, db, _colsum8(err * err)
    return _rowwise(fn, name, [(x, 0, D_MODEL), (y, 0, D_MODEL), (target, 0, D_MODEL)], [g, b],
                    [(D_MODEL, F32), (D_MODEL, BF16)], [D_MODEL, D_MODEL, D_MODEL])


def _silu_parts(g):
    s = jax.nn.sigmoid(g)
    return g * s, s * (1.0 + g * (1.0 - s))


def _layer_fwd(x, xb, w, bf_pad, lb, norm_g, ln1_g, ln1_b, ln2_g, ln2_b, l, ride=None, on_ride=None):
    T = x.shape[0]
    tq, tk = min(T, FOX_TQ), min(T, FOX_TK)
    sv = {}
    qkv = _matmul(xb, w["in"], "nn", BF16, f"l{l}_qkv", n_off=0, n_size=QKV_W, tn=768)
    rest = _matmul(xb, w["in"], "nn", F32, f"l{l}_rest", n_off=QKV_W, n_size=REST_W, tn=384)
    fcum = _fgate_fwd(rest, bf_pad)
    frow = _head_rows(fcum, tk)
    o_a, ya, lse, *rode = _fox_fwd(qkv, frow, tq, tk, ride)
    if rode:
        on_ride(rode[0])
    o_b, states = _hgrn_fwd(rest, lb, C_QB, C_FB, C_IB)

    def yb_fn(ov, gbv, ngv):
        outs = []
        for h in range(B_HEADS):
            oh = ov[:, h * B_DK:(h + 1) * B_DK]
            r = lax.rsqrt(jnp.mean(oh * oh, axis=-1, keepdims=True) + RMS_EPS)
            outs.append(oh * r * ngv)
        return jnp.concatenate(outs, axis=1) * jax.nn.sigmoid(gbv)
    (yb,) = _rowwise(yb_fn, f"l{l}_yb", [(o_b, 0, B_WIDTH), (rest, C_GB, B_WIDTH)], [norm_g], [(B_WIDTH, BF16)])
    pa = _matmul(ya, w["pa"], "nn", F32, f"l{l}_pa")
    pb = _matmul(yb, w["pb"], "nn", F32, f"l{l}_pb")

    def merge_fn(gav, gbv, pav, pbv):
        return jax.nn.sigmoid(gav) * pav + jax.nn.sigmoid(gbv) * pbv
    (merged,) = _rowwise(merge_fn, f"l{l}_merge", [(rest, C_GA, D_MODEL), (rest, C_GB2, D_MODEL), (pa, 0, D_MODEL), (pb, 0, D_MODEL)],
                         [], [(D_MODEL, BF16)])
    mix = _matmul(merged, w["o"], "nn", F32, f"l{l}_mix")
    x1, x1b = _ln_fwd(x, mix, ln1_g, ln1_b, f"l{l}_ln1")
    hid = _matmul(x1b, w["ffin"], "nn", F32, f"l{l}_ffin", tn=512)

    def act_fn(uv, gv):
        return _silu_parts(gv)[0] * uv
    (act,) = _rowwise(act_fn, f"l{l}_act", [(hid, 0, FFN_HIDDEN), (hid, FFN_HIDDEN, FFN_HIDDEN)], [], [(FFN_HIDDEN, BF16)], tr=256)
    ffn = _matmul(act, w["ffout"], "nn", F32, f"l{l}_ffout", tm=1024)
    x2, x2b = _ln_fwd(x1, ffn, ln2_g, ln2_b, f"l{l}_ln2")
    sv.update(x=x, xb=xb, rest=rest, qkv=qkv, fcum=fcum, frow=frow, o_a=o_a, lse=lse, ya=ya, o_b=o_b, states=states,
              yb=yb, pa=pa, pb=pb, merged=merged, mix=mix, x1=x1, x1b=x1b, hid=hid, act=act, ffn=ffn)
    return x2, x2b, sv


def _layer_bwd(sv, w, dz2, dz2b, dln2, bf_pad, lb, norm_g, ln1_g, l, make_ride=None, make_late_ride=None):
    T = sv["x"].shape[0]
    tq, tk = min(T, FOX_BWD_TQ), min(T, FOX_TK)
    gr = {}
    dact = _matmul(dz2b, w["ffout"], "nt", F32, f"l{l}_dact", tm=1024, tn=1408)
    gr["ffout"] = _matmul(sv["act"], dz2b, "tn", F32, f"l{l}_dwffout", tm=1408, tn=512, tk=2048)

    def dhid_fn(dav, uv, gv):
        sg, dsg = _silu_parts(gv)
        return jnp.concatenate([dav * sg, dav * uv * dsg], axis=1)
    (dhid,) = _rowwise(dhid_fn, f"l{l}_dhid", [(dact, 0, FFN_HIDDEN), (sv["hid"], 0, FFN_HIDDEN), (sv["hid"], FFN_HIDDEN, FFN_HIDDEN)],
                       [], [(2 * FFN_HIDDEN, BF16)], tr=256)
    dx1f = _matmul(dhid, w["ffin"], "nt", F32, f"l{l}_dx1f", tm=1024)
    gr["ffin"] = _matmul(sv["x1b"], dhid, "tn", F32, f"l{l}_dwffin", tm=1024, tn=512, tk=4096)
    dz1, dz1b, dg1, db1 = _ln_bwd(sv["x"], sv["mix"], ln1_g, dz2, dx1f, f"l{l}_dln1")
    dmerged = _matmul(dz1b, w["o"], "nt", F32, f"l{l}_dmerged")
    gr["o"] = _matmul(sv["merged"], dz1b, "tn", F32, f"l{l}_dwo", tm=1024, tn=512, tk=4096)

    def dmerge_fn(dmv, gav, gbv, pav, pbv):
        sa, sb = jax.nn.sigmoid(gav), jax.nn.sigmoid(gbv)
        return dmv * sa, dmv * sb, dmv * pav * sa * (1.0 - sa), dmv * pbv * sb * (1.0 - sb)
    dpa, dpb, dga, dgb2 = _rowwise(
        dmerge_fn, f"l{l}_dmerge",
        [(dmerged, 0, D_MODEL), (sv["rest"], C_GA, D_MODEL), (sv["rest"], C_GB2, D_MODEL), (sv["pa"], 0, D_MODEL), (sv["pb"], 0, D_MODEL)],
        [], [(D_MODEL, BF16)] * 4)
    dya = _matmul(dpa, w["pa"], "nt", BF16, f"l{l}_dya", tn=512)
    gr["pa"] = _matmul(sv["ya"], dpa, "tn", F32, f"l{l}_dwpa", tm=512, tn=512, tk=4096)
    dyb = _matmul(dpb, w["pb"], "nt", F32, f"l{l}_dyb", tn=512)
    gr["pb"] = _matmul(sv["yb"], dpb, "tn", F32, f"l{l}_dwpb", tm=512, tn=512, tk=4096)

    def dyb_fn(dyv, ov, gbv, ngv):
        sg = jax.nn.sigmoid(gbv)
        dos, dgs = [], []
        dng = jnp.zeros((8, B_DK), F32)
        for h in range(B_HEADS):
            sl = slice(h * B_DK, (h + 1) * B_DK)
            oh, dyh, sgh = ov[:, sl], dyv[:, sl], sg[:, sl]
            r = lax.rsqrt(jnp.mean(oh * oh, axis=-1, keepdims=True) + RMS_EPS)
            n = oh * r
            dyn = dyh * sgh
            dgs.append(dyh * n * ngv * sgh * (1.0 - sgh))
            dng = dng + _colsum8(dyn * n)
            dn = dyn * ngv
            dos.append(r * (dn - n * jnp.mean(dn * n, axis=-1, keepdims=True)))
        return jnp.concatenate(dos, axis=1), jnp.concatenate(dgs, axis=1), dng
    do_b, dgb, dng = _rowwise(dyb_fn, f"l{l}_dyb2", [(dyb, 0, B_WIDTH), (sv["o_b"], 0, B_WIDTH), (sv["rest"], C_GB, B_WIDTH)], [norm_g],
                              [(B_WIDTH, F32), (B_WIDTH, BF16)], [B_DK])
    dqb, dfb, dib, dlb = _hgrn_bwd(sv["rest"], lb, sv["states"], do_b, C_QB, C_FB, C_IB)
    delta = _fox_delta(dya, sv["o_a"])
    ride = (make_ride(gr), False) if make_ride else None
    dq, dk, dv, dfk, dfq, *rode = _fox_bwd(sv["qkv"], dya, _pair_rows(sv["lse"], tq), _pair_rows(delta, tq), sv["fcum"], tq, tk, ride)
    dfq_pad = jnp.pad(dfq.reshape(A_HEADS, T).T, ((0, 0), (0, LANE - A_HEADS)))
    dfa, dbf = _fgate_bwd(dfq_pad, dfk, sv["rest"], bf_pad)
    dproj = jnp.concatenate([dq.astype(BF16), dk, dv, dga, dgb2, dqb, dfb, dib, dgb, dfa], axis=1)
    gr["in"] = _matmul(sv["xb"], dproj, "tn", F32, f"l{l}_dwin", tm=1024, tn=640, tk=4096)
    late = (make_late_ride(gr), False) if make_late_ride else None
    dxm, *rode_late = _matmul(dproj, w["in"], "nt", F32, f"l{l}_dxm", tm=1024, tk=1920, ride=late) if late else (
        _matmul(dproj, w["in"], "nt", F32, f"l{l}_dxm", tm=1024, tk=1920),)
    small = dict(dln2=dln2, dln1=(dg1, db1), dng=dng, dlb=dlb, dbf=dbf)
    return dz1, dxm, gr, small, rode, rode_late


MESH = pl.DeviceIdType.MESH
ANY = pl.BlockSpec(memory_space=pl.ANY)


def _me_and_peers():
    x, y, c = lax.axis_index("x"), lax.axis_index("y"), lax.axis_index("c")
    me = 4 * x + 2 * y + c
    peers = []
    for p in range(1, N_DEV):
        px, py, pc = x ^ ((p >> 2) & 1), y ^ ((p >> 1) & 1), c ^ (p & 1)
        peers.append(((px, py, pc), 4 * px + 2 * py + pc))
    return me, peers


PUSH_SCRATCH = [pltpu.SemaphoreType.DMA((N_DEV - 1,)), pltpu.SemaphoreType.DMA((N_DEV - 1,)), pltpu.SemaphoreType.DMA]


def _push_copies(gather, src_ref, out_ref, send_sems, recv_sems, local_sem):
    me, peers = _me_and_peers()
    part = (lambda idx: src_ref) if gather else (lambda idx: src_ref.at[idx])
    mine = lambda: pltpu.make_async_copy(part(me), out_ref.at[me], local_sem)
    mk = lambda p, dev, idx, slot: pltpu.make_async_remote_copy(
        src_ref=part(idx), dst_ref=out_ref.at[slot], send_sem=send_sems.at[p], recv_sem=recv_sems.at[p], device_id=dev, device_id_type=MESH)
    sends = lambda: [mk(p, dev, idx, me) for p, (dev, idx) in enumerate(peers)]
    recvs = lambda: [mk(p, dev, idx, idx) for p, (dev, idx) in enumerate(peers)]
    return mine, sends, recvs


def _push_start(gather, *refs):
    mine, sends, _ = _push_copies(gather, *refs)
    mine().start()
    for cp in sends():
        cp.start()


def _push_wait(gather, *refs):
    mine, sends, recvs = _push_copies(gather, *refs)
    for cp in recvs():
        cp.wait_recv()
    for cp in sends():
        cp.wait_send()
    mine().wait()


def _push_shape(src, gather):
    return jax.ShapeDtypeStruct(((N_DEV,) + src.shape) if gather else src.shape, src.dtype)


def _push(src, gather, name, vmem=False):
    def body(*refs):
        _push_start(gather, *refs)
        _push_wait(gather, *refs)

    spec = pl.BlockSpec(memory_space=pltpu.VMEM) if vmem else ANY
    return pl.pallas_call(body, name=name, in_specs=[spec], out_specs=spec, out_shape=_push_shape(src, gather),
                          scratch_shapes=PUSH_SCRATCH)(src)


def _sum_adam(parts, w, m, v, name, tr=ADAM_ROWS):
    _, R, W = parts.shape
    tr = _pick(R, tr) if R % LANE == 0 else R
    c1 = 1.0 / (1.0 - ADAM_B1 ** ADAM_STEP)
    c2 = 1.0 / (1.0 - ADAM_B2 ** ADAM_STEP)

    def body(p_ref, w_ref, m_ref, v_ref, g_ref, d_ref, nm_ref, nv_ref):
        g = p_ref[0]
        for s in range(1, N_DEV):
            g = g + p_ref[s]
        nm = ADAM_B1 * m_ref[...] + (1.0 - ADAM_B1) * g
        nv = ADAM_B2 * v_ref[...] + (1.0 - ADAM_B2) * (g * g)
        g_ref[...] = g
        nm_ref[...] = nm
        nv_ref[...] = nv
        d_ref[...] = -ADAM_LR * ((nm * c1) / (jnp.sqrt(nv * c2) + ADAM_EPS) + ADAM_WD * w_ref[...])

    blk = pl.BlockSpec((tr, W), lambda i: (i, 0))
    return pl.pallas_call(
        body,
        name=name,
        grid=(R // tr,),
        in_specs=[pl.BlockSpec((N_DEV, tr, W), lambda i: (0, i, 0)), blk, blk, blk],
        out_specs=[blk] * 4,
        out_shape=[jax.ShapeDtypeStruct((R, W), F32)] * 4,
        compiler_params=_cparams(("parallel",)),
    )(parts, w, m, v)


BIG = ("in", "pa", "pb", "o", "ffin", "ffout")
COL_SHARDED = {"in": True, "pa": True, "pb": True, "o": False, "ffin": True, "ffout": False}
SHARD_ROWS = {"in": 5640, "pa": 512, "pb": 512, "o": 1024, "ffin": 5632, "ffout": 2816}
_round_up = lambda n, m: -(-n // m) * m
_group_rows = lambda group: _round_up(sum(SHARD_ROWS[n] for n, _ in group), ADAM_ROWS)
GATHER_FIRST = (("in", 0),)
GATHER_RIDE = tuple((n, 0) for n in BIG[1:]) + tuple((n, 1) for n in BIG)
FIRST_ROWS, RIDE_ROWS = _group_rows(GATHER_FIRST), _group_rows(GATHER_RIDE)
EARLY = tuple((n, 1) for n in BIG) + tuple((n, 0) for n in BIG[1:])
LATE = (("in", 0),)
EARLY_ROWS, LATE_ROWS = _group_rows(EARLY), _group_rows(LATE)
SMALL_ROWS = 80


class _LayerWeights:
    def __init__(self, full, layer):
        self.full, self.layer = full, layer

    def __getitem__(self, name):
        return self.full[(name, self.layer)]


def _pack(parts, rows, lead=()):
    n = len(lead)
    cat = jnp.concatenate([p.reshape(lead + (-1, LANE)) for p in parts], axis=n)
    return jnp.pad(cat, ((0, 0),) * n + ((0, rows - cat.shape[n]), (0, 0)))


def _unpack(packed, shapes, lead=()):
    out, off = [], 0
    for s in shapes:
        r = math.prod(s) // LANE
        out.append(packed[..., off:off + r, :].reshape(lead + tuple(s)))
        off += r
    return out


def _pack_flat(parts, rows):
    flat = jnp.concatenate([p.reshape(-1) for p in parts])
    return jnp.pad(flat, (0, rows * LANE - flat.shape[0])).reshape(rows, LANE)


def _unpack_flat(packed, shapes):
    flat = packed.reshape(-1)
    out, off = [], 0
    for s in shapes:
        n = math.prod(s)
        out.append(flat[off:off + n].reshape(s))
        off += n
    return out


def _join_shards(g, col_sharded):
    if col_sharded:
        return g.transpose(1, 0, 2).reshape(g.shape[1], N_DEV * g.shape[2])
    return g.reshape(N_DEV * g.shape[1], g.shape[2])


def _split_shards(full, col_sharded):
    a, b = full.shape
    if col_sharded:
        return full.reshape(a, N_DEV, b // N_DEV).transpose(1, 0, 2)
    return full.reshape(N_DEV, a // N_DEV, b)


def _permute_in(w):
    a, b = QKV_W + A_HEADS, QKV_W + A_HEADS + 4 * B_WIDTH
    return jnp.concatenate([w[:, :QKV_W], w[:, b:], w[:, a:b], w[:, QKV_W:a],
                            jnp.zeros((w.shape[0], LANE - A_HEADS), w.dtype)], axis=1)


def _unpermute_in(d):
    g0, h0, f0 = QKV_W, QKV_W + 2 * D_MODEL, QKV_W + C_FA
    return jnp.concatenate([d[:, :QKV_W], d[:, f0:f0 + A_HEADS], d[:, h0:f0], d[:, g0:h0]], axis=1)


def _lower_bounds(logits):
    sm = jax.nn.softmax(logits.astype(F32), axis=0)
    return jnp.cumsum(sm, axis=0) - sm[0:1]


def kernel(x, w_in, b_fgate, hgrn_lb_logits, hgrn_norm_g, w_branch_a, w_branch_b, w_out, ln1_g, ln1_b, w_ff_in, w_ff_out, ln2_g, ln2_b, loss_target, m_w_in, m_b_fgate, m_hgrn_lb_logits, m_hgrn_norm_g, m_w_branch_a, m_w_branch_b, m_w_out, m_ln1_g, m_ln1_b, m_w_ff_in, m_w_ff_out, m_ln2_g, m_ln2_b, v_w_in, v_b_fgate, v_hgrn_lb_logits, v_hgrn_norm_g, v_w_branch_a, v_w_branch_b, v_w_out, v_ln1_g, v_ln1_b, v_w_ff_in, v_w_ff_out, v_ln2_g, v_ln2_b):
    big = {"in": (w_in, m_w_in, v_w_in), "pa": (w_branch_a, m_w_branch_a, v_w_branch_a), "pb": (w_branch_b, m_w_branch_b, v_w_branch_b),
           "o": (w_out, m_w_out, v_w_out), "ffin": (w_ff_in, m_w_ff_in, v_w_ff_in), "ffout": (w_ff_out, m_w_ff_out, v_w_ff_out)}
    small = [(b_fgate, m_b_fgate, v_b_fgate), (hgrn_lb_logits, m_hgrn_lb_logits, v_hgrn_lb_logits), (hgrn_norm_g, m_hgrn_norm_g, v_hgrn_norm_g),
             (ln1_g, m_ln1_g, v_ln1_g), (ln1_b, m_ln1_b, v_ln1_b), (ln2_g, m_ln2_g, v_ln2_g), (ln2_b, m_ln2_b, v_ln2_b)]
    small_shapes = [s[0].shape for s in small]
    shard_shape = lambda n: big[n][0].shape[1:]

    full = {}

    def weight_pack(group, rows):
        return _pack([big[n][0][l].astype(BF16) for n, l in group], rows)

    def take_weights(gathered, group):
        for (n, l), g in zip(group, _unpack(gathered, [shard_shape(n) for n, _ in group], lead=(N_DEV,))):
            w = _join_shards(g, COL_SHARDED[n])
            full[(n, l)] = _permute_in(w) if n == "in" else w

    take_weights(_push(weight_pack(GATHER_FIRST, FIRST_ROWS), True, "gather_weights"), GATHER_FIRST)
    weights = [_LayerWeights(full, l) for l in range(DEPTH)]
    lbounds = _lower_bounds(hgrn_lb_logits)

    def layer_params(l):
        bf_pad = jnp.pad(b_fgate[l].astype(F32), (0, LANE - A_HEADS)).reshape(1, LANE)
        row = lambda a: a[l].astype(F32).reshape(1, -1)
        return dict(bf_pad=bf_pad, lb=lbounds[l].reshape(1, -1), norm_g=row(hgrn_norm_g), ln1_g=row(ln1_g), ln1_b=row(ln1_b),
                    ln2_g=row(ln2_g), ln2_b=row(ln2_b))

    xl = x[0]
    xlb = xl.astype(BF16)
    saved = []
    for l in range(DEPTH):
        p = layer_params(l)
        ride = (weight_pack(GATHER_RIDE, RIDE_ROWS), True) if l == 0 else None
        xl, xlb, sv = _layer_fwd(xl, xlb, weights[l], p["bf_pad"], p["lb"], p["norm_g"], p["ln1_g"], p["ln1_b"],
                                 p["ln2_g"], p["ln2_b"], l, ride, lambda g: take_weights(g, GATHER_RIDE))
        saved.append(sv)

    grads = [None] * DEPTH
    smalls = [None] * DEPTH

    def contributions(group, rows):
        parts = []
        for n, l in group:
            g = _unpermute_in(grads[l][n]) if n == "in" else grads[l][n]
            parts.append(_split_shards(g, COL_SHARDED[n]))
        return _pack(parts, rows, lead=(N_DEV,))

    def early_ride(gr0):
        grads[0] = gr0
        return contributions(EARLY, EARLY_ROWS)

    loss_part = None
    ua = ub = None
    received = {}
    for l in reversed(range(DEPTH)):
        p, sv = layer_params(l), saved[l]
        if l == DEPTH - 1:
            dz2, dz2b, dg2, db2, loss_part = _ln_bwd_loss(sv["x1"], sv["ffn"], p["ln2_g"], p["ln2_b"], loss_target[0], f"l{l}_dln2")
        else:
            dz2, dz2b, dg2, db2 = _ln_bwd(sv["x1"], sv["ffn"], p["ln2_g"], ua, ub, f"l{l}_dln2")
        rides = (early_ride, lambda gr0: contributions(LATE, LATE_ROWS)) if l == 0 else (None, None)
        ua, ub, grads[l], smalls[l], rode, rode_late = _layer_bwd(sv, weights[l], dz2, dz2b, (dg2, db2), p["bf_pad"], p["lb"],
                                                                  p["norm_g"], p["ln1_g"], l, *rides)
        if rode:
            received["early"], received["late"] = rode[0], rode_late[0]
    (grad_x,) = _rowwise(lambda a, b: ALPHA * a + b, "grad_x", [(ua, 0, D_MODEL), (ub, 0, D_MODEL)], [], [(D_MODEL, F32)])

    by_kind = [{} for _ in range(4)]
    for key, group, rows in (("early", EARLY, EARLY_ROWS), ("late", LATE, LATE_ROWS)):
        wmv = [_pack([big[n][i][l].astype(F32) for n, l in group], rows) for i in range(3)]
        shapes = [shard_shape(n) for n, _ in group]
        for kind, packed in enumerate(_sum_adam(received[key], *wmv, f"adam_{key}")):
            by_kind[kind].update(zip(group, _unpack(packed, shapes)))
    outs_big = [[jnp.stack([d[(n, l)] for l in range(DEPTH)]) for n in BIG] for d in by_kind]

    fold = lambda a: jnp.sum(a, axis=0)
    dlb = jnp.stack([fold(smalls[l]["dlb"]) for l in range(DEPTH)])
    _, lb_vjp = jax.vjp(_lower_bounds, hgrn_lb_logits)
    small_grads = [jnp.stack([fold(smalls[l]["dbf"])[:A_HEADS] for l in range(DEPTH)]), lb_vjp(dlb)[0],
                   jnp.stack([fold(smalls[l]["dng"]) for l in range(DEPTH)]),
                   jnp.stack([fold(smalls[l]["dln1"][0]) for l in range(DEPTH)]), jnp.stack([fold(smalls[l]["dln1"][1]) for l in range(DEPTH)]),
                   jnp.stack([fold(smalls[l]["dln2"][0]) for l in range(DEPTH)]), jnp.stack([fold(smalls[l]["dln2"][1]) for l in range(DEPTH)])]
    loss_local = (0.5 / D_MODEL) * jnp.sum(loss_part)
    small_all = _push(_pack_flat(small_grads + [loss_local.reshape(1)], SMALL_ROWS), True, "gather_small", vmem=True)
    one = jnp.zeros((1,), F32)
    swmv = [_pack_flat([s[i].astype(F32) for s in small] + [one], SMALL_ROWS) for i in range(3)]
    outs_small = [_unpack_flat(o, small_shapes + [(1,)]) for o in _sum_adam(small_all, *swmv, "adam_small")]
    loss = outs_small[0][-1].reshape(())

    order = {"in": 0, "pa": 4, "pb": 5, "o": 6, "ffin": 9, "ffout": 10}
    small_pos = [1, 2, 3, 7, 8, 11, 12]
    result = [loss, grad_x.reshape(x.shape)]
    for kind in range(4):
        slot = [None] * 13
        for n, arr in zip(BIG, outs_big[kind]):
            slot[order[n]] = arr
        for pos, arr in zip(small_pos, outs_small[kind][:-1]):
            slot[pos] = arr
        result.extend(slot)
    return tuple(result)
```

```python
import functools
import math

import jax
import jax.numpy as jnp
from jax import lax
from jax.experimental import pallas as pl
from jax.experimental.pallas import tpu as pltpu

F32 = jnp.float32
BF16 = jnp.bfloat16

D_MODEL = 1024
DEPTH = 2
CHUNK = 64
A_HEADS = 8
A_HEAD_DIM = 64
A_WIDTH = 512
B_WIDTH = 512
B_HEADS = 4
B_DK = 128
FFN_HIDDEN = 2816
IN_TOTAL = 5640
ALPHA = (2 * DEPTH) ** 0.25
LN_EPS = 1e-5
RMS_EPS = 1e-6
ADAM_LR = 0.001
ADAM_B1 = 0.9
ADAM_B2 = 0.999
ADAM_EPS = 1e-08
ADAM_WD = 0.01
ADAM_STEP = 10

N_DEV = 8
LANE = 128
VMEM_LIMIT = 48 * 1024 * 1024
ADAM_ROWS = 512

QKV_W = 3 * A_WIDTH
C_GA, C_GB2 = 0, D_MODEL
C_QB = 2 * D_MODEL
C_FB, C_IB, C_GB = C_QB + B_WIDTH, C_QB + 2 * B_WIDTH, C_QB + 3 * B_WIDTH
C_FA = C_QB + 4 * B_WIDTH
REST_W = C_FA + LANE
IN_PERM = QKV_W + REST_W


def _cparams(sem):
    return pltpu.CompilerParams(dimension_semantics=sem, vmem_limit_bytes=VMEM_LIMIT)


def _pick(n, pref):
    if n <= pref:
        return n
    t = pref
    while n % t:
        t -= LANE
    return t


def _matmul(a, b, mode, out_dtype, name, *, n_off=0, n_size=None, tm=2048, tn=1024, tk=2816, ride=None, addend=None):
    if mode == "nn":
        M, K = a.shape
        N = b.shape[1] if n_size is None else n_size
    elif mode == "nt":
        M, K = a.shape
        N = b.shape[0]
    else:
        K, M = a.shape
        N = b.shape[1]
    tm, tn, tk = _pick(M, tm), _pick(N, tn), _pick(K, tk)
    assert n_off % tn == 0
    noff = n_off // tn
    nk = K // tk

    dn = {"nn": (((1,), (0,)), ((), ())), "nt": (((1,), (1,)), ((), ())), "tn": (((0,), (0,)), ((), ()))}[mode]

    ni, nj = M // tm, N // tn
    r_in, r_out, r_shape, r_scratch, r_args = _ride(ride)
    assert ride is None or ni * nj * nk > 1

    n_in = 2 if addend is None else 3

    def body(*refs):
        i, j, kk = pl.program_id(0), pl.program_id(1), pl.program_id(2)
        refs = _ride_steps(ride, refs, n_in, 1, (i == 0) & (j == 0) & (kk == 0), (i == ni - 1) & (j == nj - 1) & (kk == nk - 1))
        a_ref, b_ref = refs[:2]
        o_ref, *scratch = refs[n_in:]
        finish = (lambda v: v) if addend is None else (lambda v: v + addend[1] * refs[2][...])
        prod = lax.dot_general(a_ref[...], b_ref[...], dn, preferred_element_type=F32)
        if nk == 1:
            o_ref[...] = finish(prod).astype(o_ref.dtype)
            return
        (acc_ref,) = scratch
        k = pl.program_id(2)

        @pl.when(k == 0)
        def _():
            acc_ref[...] = prod

        @pl.when(k > 0)
        def _():
            acc_ref[...] += prod

        @pl.when(k == nk - 1)
        def _():
            o_ref[...] = finish(acc_ref[...]).astype(o_ref.dtype)

    if mode == "nn":
        a_spec = pl.BlockSpec((tm, tk), lambda i, j, k: (i, k))
        b_spec = pl.BlockSpec((tk, tn), lambda i, j, k: (k, j + noff))
    elif mode == "nt":
        a_spec = pl.BlockSpec((tm, tk), lambda i, j, k: (i, k))
        b_spec = pl.BlockSpec((tn, tk), lambda i, j, k: (j, k))
    else:
        a_spec = pl.BlockSpec((tk, tm), lambda i, j, k: (k, i))
        b_spec = pl.BlockSpec((tk, tn), lambda i, j, k: (k, j))
    res = pl.pallas_call(
        body,
        name=name,
        grid=(ni, nj, nk),
        in_specs=[a_spec, b_spec] + ([] if addend is None else [pl.BlockSpec((tm, tn), lambda i, j, k: (i, j))]) + r_in,
        out_specs=[pl.BlockSpec((tm, tn), lambda i, j, k: (i, j))] + r_out,
        out_shape=[jax.ShapeDtypeStruct((M, N), out_dtype)] + r_shape,
        scratch_shapes=([pltpu.VMEM((tm, tn), F32)] if nk > 1 else []) + r_scratch,
        compiler_params=_cparams(("parallel", "parallel", "arbitrary") if ride is None else ("arbitrary",) * 3),
    )(a, b, *([] if addend is None else [addend[0]]), *r_args)
    return res[0] if ride is None else tuple(res)


def _rowwise(fn, name, tiled, params, outs, sums=(), *, tr=512):
    T = tiled[0][0].shape[0]
    tr = _pick(T, tr)
    nt, npar, no, ns = len(tiled), len(params), len(outs), len(sums)

    def body(*refs):
        ins = [r[...] for r in refs[:nt + npar]]
        res = fn(*ins)
        if not isinstance(res, (tuple, list)):
            res = (res,)
        o_refs = refs[nt + npar:nt + npar + no]
        s_refs = refs[nt + npar + no:]
        for r, v in zip(o_refs, res[:no]):
            r[...] = v.astype(r.dtype)
        if ns:
            @pl.when(pl.program_id(0) == 0)
            def _():
                for r in s_refs:
                    r[...] = jnp.zeros_like(r)
            for r, v in zip(s_refs, res[no:]):
                r[...] += v

    in_specs = []
    args = []
    for arr, off, w in tiled:
        assert off % w == 0 or w == arr.shape[1]
        cb = off // w
        in_specs.append(pl.BlockSpec((tr, w), lambda i, cb=cb: (i, cb)))
        args.append(arr)
    for p in params:
        in_specs.append(pl.BlockSpec(p.shape, lambda i, nd=p.ndim: (0,) * nd))
        args.append(p)
    out_specs = [pl.BlockSpec((tr, w), lambda i: (i, 0)) for w, _ in outs]
    out_shape = [jax.ShapeDtypeStruct((T, w), dt) for w, dt in outs]
    out_specs += [pl.BlockSpec((8, w), lambda i: (0, 0)) for w in sums]
    out_shape += [jax.ShapeDtypeStruct((8, w), F32) for w in sums]
    res = pl.pallas_call(
        body,
        name=name,
        grid=(T // tr,),
        in_specs=in_specs,
        out_specs=out_specs,
        out_shape=out_shape,
        compiler_params=_cparams(("arbitrary",)),
    )(*args)
    return res


def _colsum8(v):
    r, w = v.shape
    return jnp.sum(v.reshape(r // 8, 8, w), axis=0)


NEG_BIG = -1e30
QK_SCALE = A_HEAD_DIM ** -0.5
PAIRS = A_HEADS // 2
QKV_BLOCKS = A_WIDTH // LANE
FOX_TQ = 1024
FOX_TK = 1024
FOX_BWD_TQ = 512
_NT = (((1,), (1,)), ((), ()))
_NN = (((1,), (0,)), ((), ()))
_TN = (((0,), (0,)), ((), ()))


def _first_head(shape):
    return lax.broadcasted_iota(jnp.int32, shape, 1) < A_HEAD_DIM


def _split_heads(a, first):
    return jnp.where(first, a, 0), jnp.where(first, 0, a)


def _visible(i, j, tq, tk, keys_on_rows=False):
    shape = (tk, tq) if keys_on_rows else (tq, tk)
    r = lax.broadcasted_iota(jnp.int32, shape, 0)
    c = lax.broadcasted_iota(jnp.int32, shape, 1)
    return ((c - r) if keys_on_rows else (r - c)) >= j * tk - i * tq


def _lanes01(a0, a1, rows):
    lane = lax.broadcasted_iota(jnp.int32, (rows, LANE), 1)
    return jnp.where(lane == 0, a0, jnp.where(lane == 1, a1, 0.0))


def _ride(ride):
    if ride is None:
        return [], [], [], [], []
    src, gather = ride
    return [ANY], [ANY], [_push_shape(src, gather)], PUSH_SCRATCH, [src]


def _ride_steps(ride, refs, n_in, n_out, first, last):
    if ride is None:
        return refs
    src_ref, dst_ref, sems = refs[n_in], refs[n_in + 1 + n_out], refs[-3:]
    pl.when(first)(lambda: _push_start(ride[1], src_ref, dst_ref, *sems))
    pl.when(last)(lambda: _push_wait(ride[1], src_ref, dst_ref, *sems))
    return refs[:n_in] + refs[n_in + 1:n_in + 1 + n_out] + refs[n_in + 2 + n_out:-3]


def _fox_fwd(qkv, frow, tq, tk, ride=None):
    T = qkv.shape[0]
    assert tq == tk, "k blocks before q block i are fully visible, block i is the diagonal one"
    nbk = T // tk
    nq = T // tq
    r_in, r_out, r_shape, r_scratch, r_args = _ride(ride)

    def body(*refs):
        hp, i = pl.program_id(0), pl.program_id(1)
        refs = _ride_steps(ride, refs, 4, 3, (hp == 0) & (i == 0), (hp == PAIRS - 1) & (i == nq - 1))
        q_ref, k_ref, v_ref, fk_ref, o_ref, ob_ref, lse_ref, m_ref, l_ref, acc_ref, s_buf = refs
        first = _first_head((tq, LANE))
        qs = _split_heads(q_ref[...] * QK_SCALE, first)
        m_ref[...] = jnp.full_like(m_ref, NEG_BIG)
        l_ref[...] = jnp.zeros_like(l_ref)
        acc_ref[...] = jnp.zeros_like(acc_ref)

        def logits(j):
            k2 = k_ref[pl.ds(pl.multiple_of(j * tk, tk), tk), :]
            return [lax.dot_general(qs[hh], k2, _NT, preferred_element_type=F32) - fk_ref[hh, j] for hh in range(2)]

        def softmax_pv(j, ss, diag):
            v2 = v_ref[pl.ds(pl.multiple_of(j * tk, tk), tk), :]
            for hh in range(2):
                s = jnp.where(_visible(i, j, tq, tk), ss[hh], NEG_BIG) if diag else ss[hh]
                m_old = m_ref[hh]
                m_new = jnp.maximum(m_old, jnp.max(s, axis=1, keepdims=True))
                p = jnp.exp(s - m_new[:, :1])
                a = jnp.exp(m_old - m_new)
                l_ref[hh] = a * l_ref[hh] + jnp.sum(p, axis=1, keepdims=True)
                acc_ref[hh] = a * acc_ref[hh] + lax.dot_general(p.astype(BF16), v2, _NN, preferred_element_type=F32)
                m_ref[hh] = m_new

        s_buf[0], s_buf[1] = logits(0)

        def off_diag(j, c):
            cur = [s_buf[0], s_buf[1]]
            nxt = logits(j + 1)
            softmax_pv(j, cur, False)
            s_buf[0], s_buf[1] = nxt
            return c

        lax.fori_loop(0, i, off_diag, 0)
        softmax_pv(i, [s_buf[0], s_buf[1]], True)
        o = jnp.where(first, acc_ref[0] / l_ref[0], acc_ref[1] / l_ref[1])
        o_ref[...] = o
        ob_ref[...] = o.astype(BF16)
        lse_ref[0] = _lanes01((m_ref[0] + jnp.log(l_ref[0]))[:, :1], (m_ref[1] + jnp.log(l_ref[1]))[:, :1], tq)

    col_blk = pl.BlockSpec((tq, LANE), lambda hp, i: (i, hp))
    return pl.pallas_call(
        body,
        name="fox_fwd",
        grid=(PAIRS, nq),
        in_specs=[col_blk,
                  pl.BlockSpec((T, LANE), lambda hp, i: (0, QKV_BLOCKS + hp)),
                  pl.BlockSpec((T, LANE), lambda hp, i: (0, 2 * QKV_BLOCKS + hp)),
                  pl.BlockSpec((2, nbk, 1, tk), lambda hp, i: (hp, 0, 0, 0))] + r_in,
        out_specs=[col_blk, col_blk, pl.BlockSpec((1, tq, LANE), lambda hp, i: (hp, i, 0))] + r_out,
        out_shape=[jax.ShapeDtypeStruct((T, A_WIDTH), F32), jax.ShapeDtypeStruct((T, A_WIDTH), BF16),
                   jax.ShapeDtypeStruct((PAIRS, T, LANE), F32)] + r_shape,
        scratch_shapes=[pltpu.VMEM((2, tq, LANE), F32)] * 3 + [pltpu.VMEM((2, tq, tk), F32)] + r_scratch,
        compiler_params=_cparams(("arbitrary", "arbitrary")),
    )(qkv, qkv, qkv, frow, *r_args)


def _fox_delta(do, o):
    T = do.shape[0]
    tr = min(T, FOX_TQ)

    def body(do_ref, o_ref, dl_ref):
        first = _first_head((tr, LANE))
        prod = do_ref[...].astype(F32) * o_ref[...]
        dl_ref[0] = _lanes01(jnp.sum(jnp.where(first, prod, 0.0), axis=1, keepdims=True),
                             jnp.sum(jnp.where(first, 0.0, prod), axis=1, keepdims=True), tr)

    blk = pl.BlockSpec((tr, LANE), lambda hp, i: (i, hp))
    return pl.pallas_call(
        body,
        name="fox_delta",
        grid=(PAIRS, T // tr),
        in_specs=[blk, blk],
        out_specs=pl.BlockSpec((1, tr, LANE), lambda hp, i: (hp, i, 0)),
        out_shape=jax.ShapeDtypeStruct((PAIRS, T, LANE), F32),
        compiler_params=_cparams(("parallel", "parallel")),
    )(do, o)


def _fox_bwd(qkv, do, lse_row, delta_row, fpad, tq, tk, ride=None):
    T = qkv.shape[0]
    nbq = T // tq
    nbk = T // tk
    r_in, r_out, r_shape, r_scratch, r_args = _ride(ride)

    def body(*refs):
        hp = pl.program_id(0)
        j = pl.program_id(1)
        refs = _ride_steps(ride, refs, 7, 5, (hp == 0) & (j == 0), (hp == PAIRS - 1) & (j == nbk - 1))
        q_ref, k_ref, v_ref, do_ref, lse_ref, dl_ref, f_ref, dq_ref, dk_ref, dv_ref, df_ref, dfq_ref, dk_acc, dv_acc, df_acc = refs
        first = _first_head((tk, LANE))
        k2 = k_ref[...]
        ks = _split_heads(k2, first)
        ksq = _split_heads(k2 * QK_SCALE, first)
        vs = _split_heads(v_ref[...], first)
        lane = lax.broadcasted_iota(jnp.int32, (tk, LANE), 1)
        fks = tuple(jnp.sum(jnp.where(lane == 2 * hp + hh, f_ref[...], 0.0), axis=1, keepdims=True) for hh in range(2))
        dk_acc[...] = jnp.zeros_like(dk_acc)
        dv_acc[...] = jnp.zeros_like(dv_acc)
        df_acc[...] = jnp.zeros_like(df_acc)

        @pl.when(j == 0)
        def _():
            dq_ref[...] = jnp.zeros_like(dq_ref)
            dfq_ref[...] = jnp.zeros_like(dfq_ref)

        def step(i, diag):
            rows = pl.ds(pl.multiple_of(i * tq, tq), tq)
            q2 = q_ref[rows, :] * QK_SCALE
            do2 = do_ref[rows, :]
            dq_add = None
            for hh in range(2):
                st = lax.dot_general(ks[hh], q2, _NT, preferred_element_type=F32) - fks[hh]
                pt = jnp.exp(st - lse_ref[hh, i])
                if diag:
                    pt = jnp.where(_visible(i, j, tq, tk, keys_on_rows=True), pt, 0.0)
                dpt = lax.dot_general(vs[hh], do2, _NT, preferred_element_type=F32)
                dst = pt * (dpt - dl_ref[hh, i])
                dsb = dst.astype(BF16)
                dv_acc[hh] += lax.dot_general(pt.astype(BF16), do2, _NN, preferred_element_type=F32)
                dk_acc[hh] += lax.dot_general(dsb, q2, _NN, preferred_element_type=F32)
                df_acc[hh] += jnp.sum(dst, axis=1, keepdims=True)
                dfq_ref[hh, i] += jnp.sum(dst, axis=0, keepdims=True)
                part = lax.dot_general(dsb, ksq[hh], _TN, preferred_element_type=F32)
                dq_add = part if dq_add is None else dq_add + part
            dq_ref[rows, :] += dq_add

        i0 = lax.shift_right_logical(j * tk, tq.bit_length() - 1)
        nd = max(1, tk // tq)
        for d in range(nd):
            step(i0 + d, True)

        def off_diag(i, c):
            step(i, False)
            return c

        lax.fori_loop(i0 + nd, nbq, off_diag, 0)
        dk_ref[...] = jnp.where(first, dk_acc[0], dk_acc[1]).astype(BF16)
        dv_ref[...] = jnp.where(first, dv_acc[0], dv_acc[1]).astype(BF16)
        df_ref[0] = _lanes01(-df_acc[0][:, 0:1], -df_acc[1][:, 0:1], tk)

    whole = lambda cb: pl.BlockSpec((T, LANE), lambda hp, j, cb=cb: (0, cb + hp), pipeline_mode=pl.Buffered(1))
    blk = lambda cb: pl.BlockSpec((tk, LANE), lambda hp, j, cb=cb: (j, cb + hp))
    rows = pl.BlockSpec((2, nbq, 1, tq), lambda hp, j: (hp, 0, 0, 0))
    return pl.pallas_call(
        body,
        name="fox_bwd",
        grid=(PAIRS, nbk),
        in_specs=[whole(0), blk(QKV_BLOCKS), blk(2 * QKV_BLOCKS), whole(0), rows, rows, pl.BlockSpec((tk, LANE), lambda hp, j: (j, 0))] + r_in,
        out_specs=[whole(0), blk(0), blk(0), pl.BlockSpec((1, tk, LANE), lambda hp, j: (hp, j, 0)), rows] + r_out,
        out_shape=[jax.ShapeDtypeStruct((T, A_WIDTH), F32), jax.ShapeDtypeStruct((T, A_WIDTH), BF16), jax.ShapeDtypeStruct((T, A_WIDTH), BF16),
                   jax.ShapeDtypeStruct((PAIRS, T, LANE), F32), jax.ShapeDtypeStruct((A_HEADS, nbq, 1, tq), F32)] + r_shape,
        scratch_shapes=[pltpu.VMEM((2, tk, LANE), F32)] * 3 + r_scratch,
        compiler_params=_cparams(("arbitrary", "arbitrary")),
    )(qkv, qkv, qkv, do, lse_row, delta_row, fpad, *r_args)


HG_ROWS = 512


def _tri(n, upper):
    r = lax.broadcasted_iota(jnp.int32, (n, n), 0)
    c = lax.broadcasted_iota(jnp.int32, (n, n), 1)
    return jnp.where((r <= c) if upper else (r >= c), 1.0, 0.0).astype(F32)


def _tri_dot(tri, x):
    hi = x.astype(BF16)
    r1 = x - hi.astype(F32)
    mid = r1.astype(BF16)
    lo = (r1 - mid.astype(F32)).astype(BF16)
    n = x.shape[1]
    out = jnp.dot(tri.astype(BF16), jnp.concatenate([hi, mid, lo], axis=1), preferred_element_type=F32)
    return out[:, :n] + out[:, n:2 * n] + out[:, 2 * n:]


def _dot(a, b, dn):
    return lax.dot_general(a.astype(BF16), b.astype(BF16), dn, preferred_element_type=F32)


def _hgrn_gates(qb, fb, lb):
    sig = jax.nn.sigmoid(fb)
    f = lb + (1.0 - lb) * sig
    sq = jax.nn.sigmoid(qb)
    return sig, f, sq, qb * sq


def _hgrn_fwd(rest, lb, col_q, col_f, col_i):
    T = rest.shape[0]
    tb = _pick(T, HG_ROWS)
    cpb = tb // CHUNK
    nblk = T // tb

    def body(q_ref, f_ref, i_ref, lb_ref, o_ref, st_ref, s_ref):
        @pl.when(pl.program_id(0) == 0)
        def _():
            s_ref[...] = jnp.zeros_like(s_ref)

        tril = _tri(CHUNK, False)
        mask = tril > 0.5

        def chunk(c, carry):
            rows = pl.ds(pl.multiple_of(c * CHUNK, CHUNK), CHUNK)
            for h in range(B_HEADS):
                cols = slice(h * B_DK, (h + 1) * B_DK)
                _, f, _, q = _hgrn_gates(q_ref[rows, cols], f_ref[rows, cols], lb_ref[:, cols])
                v = i_ref[rows, cols]
                k = 1.0 - f
                b = _tri_dot(tril, jnp.log(f))
                bm = b[CHUNK // 2 - 1:CHUNK // 2, :]
                bl = b[CHUNK - 1:CHUNK, :]
                st = s_ref[h]
                st_ref[h, c] = st
                p = jnp.where(mask, _dot(q * jnp.exp(b - bm), k * jnp.exp(bm - b), _NT), 0.0)
                o_ref[rows, cols] = _dot(p, v, _NN) + _dot(q * jnp.exp(b), st, _NT)
                s_ref[h] = st * jnp.exp(bl) + _dot(v, k * jnp.exp(bl - b), _TN)
            return carry

        lax.fori_loop(0, cpb, chunk, 0)

    col = lambda cb: pl.BlockSpec((tb, B_WIDTH), lambda i, cb=cb: (i, cb))
    return pl.pallas_call(
        body,
        name="hgrn_fwd",
        grid=(nblk,),
        in_specs=[col(col_q // B_WIDTH), col(col_f // B_WIDTH), col(col_i // B_WIDTH), pl.BlockSpec((1, B_WIDTH), lambda i: (0, 0))],
        out_specs=[pl.BlockSpec((tb, B_WIDTH), lambda i: (i, 0)),
                   pl.BlockSpec((B_HEADS, cpb, B_DK, B_DK), lambda i: (0, i, 0, 0))],
        out_shape=[jax.ShapeDtypeStruct((T, B_WIDTH), F32), jax.ShapeDtypeStruct((B_HEADS, T // CHUNK, B_DK, B_DK), F32)],
        scratch_shapes=[pltpu.VMEM((B_HEADS, B_DK, B_DK), F32)],
        compiler_params=_cparams(("arbitrary",)),
    )(rest, rest, rest, lb)


def _hgrn_bwd(rest, lb, states, do, col_q, col_f, col_i):
    T = rest.shape[0]
    tb = _pick(T, HG_ROWS)
    cpb = tb // CHUNK
    nblk = T // tb

    def body(q_ref, f_ref, i_ref, lb_ref, st_ref, do_ref, dq_ref, df_ref, di_ref, dlb_ref, ds_ref):
        @pl.when(pl.program_id(0) == 0)
        def _():
            ds_ref[...] = jnp.zeros_like(ds_ref)
            dlb_ref[...] = jnp.zeros_like(dlb_ref)

        tril = _tri(CHUNK, False)
        triu = _tri(CHUNK, True)
        mask = tril > 0.5

        def chunk(cc, carry):
            c = cpb - 1 - cc
            rows = pl.ds(pl.multiple_of(c * CHUNK, CHUNK), CHUNK)
            for h in range(B_HEADS):
                cols = slice(h * B_DK, (h + 1) * B_DK)
                lbv = lb_ref[:, cols]
                qb = q_ref[rows, cols]
                sig, f, sq, q = _hgrn_gates(qb, f_ref[rows, cols], lbv)
                v = i_ref[rows, cols]
                dov = do_ref[rows, cols]
                k = 1.0 - f
                b = _tri_dot(tril, jnp.log(f))
                bm = b[CHUNK // 2 - 1:CHUNK // 2, :]
                bl = b[CHUNK - 1:CHUNK, :]
                eb, eqm, ekm, ekl, ebl = jnp.exp(b), jnp.exp(b - bm), jnp.exp(bm - b), jnp.exp(bl - b), jnp.exp(bl)
                qt, kt, qa, kd = q * eqm, k * ekm, q * eb, k * ekl
                st0 = st_ref[h, c]
                dst1 = ds_ref[h]
                p = jnp.where(mask, _dot(qt, kt, _NT), 0.0)
                dp = jnp.where(mask, _dot(dov, v, _NT), 0.0)
                dv = _dot(p, dov, _TN) + _dot(kd, dst1, _NT)
                dqt = _dot(dp, kt, _NN)
                dkt = _dot(dp, qt, _TN)
                dqa = _dot(dov, st0, _NN)
                dkd = _dot(v, dst1, _NN)
                ds_ref[h] = _dot(dov, qa, _TN) + dst1 * ebl
                dq = dqt * eqm + dqa * eb
                dk = dkt * ekm + dkd * ekl
                rnd = lambda a: a.astype(BF16).astype(F32)
                db = dqt * rnd(qt) - dkt * rnd(kt) + dqa * qa - dkd * kd
                last = jnp.sum(dkd * kd, axis=0, keepdims=True) + ebl * jnp.sum(dst1 * st0, axis=0, keepdims=True)
                dg = _tri_dot(triu, db) + last
                dfv = dg / f - dk
                dq_ref[rows, cols] = (dq * (sq * (1.0 + qb * (1.0 - sq)))).astype(BF16)
                df_ref[rows, cols] = (dfv * (1.0 - lbv) * sig * (1.0 - sig)).astype(BF16)
                di_ref[rows, cols] = dv.astype(BF16)
                dlb_ref[0:1, cols] += jnp.sum(dfv * (1.0 - sig), axis=0, keepdims=True)
            return carry

        lax.fori_loop(0, cpb, chunk, 0)

    rev = lambda i: nblk - 1 - i
    col = lambda cb: pl.BlockSpec((tb, B_WIDTH), lambda i, cb=cb: (rev(i), cb))
    return pl.pallas_call(
        body,
        name="hgrn_bwd",
        grid=(nblk,),
        in_specs=[col(col_q // B_WIDTH), col(col_f // B_WIDTH), col(col_i // B_WIDTH), pl.BlockSpec((1, B_WIDTH), lambda i: (0, 0)),
                  pl.BlockSpec((B_HEADS, cpb, B_DK, B_DK), lambda i: (0, rev(i), 0, 0)), col(0)],
        out_specs=[col(0), col(0), col(0), pl.BlockSpec((8, B_WIDTH), lambda i: (0, 0))],
        out_shape=[jax.ShapeDtypeStruct((T, B_WIDTH), BF16)] * 3 + [jax.ShapeDtypeStruct((8, B_WIDTH), F32)],
        scratch_shapes=[pltpu.VMEM((B_HEADS, B_DK, B_DK), F32)],
        compiler_params=_cparams(("arbitrary",)),
    )(rest, rest, rest, lb, states, do)


SCAN_ROWS = 256


def _log_sigmoid(x):
    return jnp.minimum(x, 0.0) - jnp.log(1.0 + jnp.exp(-jnp.abs(x)))


def _head_cols(shape):
    return lax.broadcasted_iota(jnp.int32, shape, 1) < A_HEADS


def _fgate_fwd(rest, bf_pad):
    T = rest.shape[0]
    tr = _pick(T, SCAN_ROWS)

    def body(fa_ref, bf_ref, f_ref, carry_ref):
        @pl.when(pl.program_id(0) == 0)
        def _():
            carry_ref[...] = jnp.zeros_like(carry_ref)

        lf = jnp.where(_head_cols((tr, LANE)), _log_sigmoid(fa_ref[...] + bf_ref[...]), 0.0)
        f = _tri_dot(_tri(tr, False), lf) + carry_ref[...]
        f_ref[...] = f
        carry_ref[...] = f[tr - 1:tr, :]

    return pl.pallas_call(
        body,
        name="fgate_fwd",
        grid=(T // tr,),
        in_specs=[pl.BlockSpec((tr, LANE), lambda i: (i, C_FA // LANE)), pl.BlockSpec((1, LANE), lambda i: (0, 0))],
        out_specs=pl.BlockSpec((tr, LANE), lambda i: (i, 0)),
        out_shape=jax.ShapeDtypeStruct((T, LANE), F32),
        scratch_shapes=[pltpu.VMEM((1, LANE), F32)],
        compiler_params=_cparams(("arbitrary",)),
    )(rest, bf_pad)


def _fgate_bwd(dfq, dfk, rest, bf_pad):
    T = rest.shape[0]
    tr = _pick(T, SCAN_ROWS)
    nblk = T // tr

    def body(dq_ref, dk_ref, fa_ref, bf_ref, dfa_ref, db_ref, carry_ref):
        @pl.when(pl.program_id(0) == 0)
        def _():
            carry_ref[...] = jnp.zeros_like(carry_ref)
            db_ref[...] = jnp.zeros_like(db_ref)

        df = dq_ref[...] + dk_ref[0]
        for hp in range(1, PAIRS):
            df = df + pltpu.roll(dk_ref[hp], 2 * hp, 1)
        dlf = _tri_dot(_tri(tr, True), df) + carry_ref[...]
        carry_ref[...] = dlf[0:1, :]
        dfa = jnp.where(_head_cols((tr, LANE)), dlf * jax.nn.sigmoid(-(fa_ref[...] + bf_ref[...])), 0.0)
        dfa_ref[...] = dfa.astype(BF16)
        db_ref[...] += _colsum8(dfa)

    rev = lambda i: nblk - 1 - i
    return pl.pallas_call(
        body,
        name="fgate_bwd",
        grid=(nblk,),
        in_specs=[pl.BlockSpec((tr, LANE), lambda i: (rev(i), 0)), pl.BlockSpec((PAIRS, tr, LANE), lambda i: (0, rev(i), 0)),
                  pl.BlockSpec((tr, LANE), lambda i: (rev(i), C_FA // LANE)), pl.BlockSpec((1, LANE), lambda i: (0, 0))],
        out_specs=[pl.BlockSpec((tr, LANE), lambda i: (rev(i), 0)), pl.BlockSpec((8, LANE), lambda i: (0, 0))],
        out_shape=[jax.ShapeDtypeStruct((T, LANE), BF16), jax.ShapeDtypeStruct((8, LANE), F32)],
        scratch_shapes=[pltpu.VMEM((1, LANE), F32)],
        compiler_params=_cparams(("arbitrary",)),
    )(dfq, dfk, rest, bf_pad)


def _head_rows(a, t):
    return a[:, :A_HEADS].T.reshape(A_HEADS, a.shape[0] // t, 1, t)


def _pair_rows(a, t):
    return a[:, :, :2].transpose(0, 2, 1).reshape(A_HEADS, a.shape[1] // t, 1, t)


def _ln_stats(z):
    mu = jnp.mean(z, axis=-1, keepdims=True)
    zc = z - mu
    rstd = lax.rsqrt(jnp.mean(zc * zc, axis=-1, keepdims=True) + LN_EPS)
    return zc * rstd, rstd


def _ln_fwd(x, y, g, b, name):
    def fn(xv, yv, gv, bv):
        xhat, _ = _ln_stats(ALPHA * xv + yv)
        out = xhat * gv + bv
        return out, out
    return _rowwise(fn, name, [(x, 0, D_MODEL), (y, 0, D_MODEL)], [g, b], [(D_MODEL, F32), (D_MODEL, BF16)])


def _ln_bwd_core(xv, yv, gv, dy):
    xhat, rstd = _ln_stats(ALPHA * xv + yv)
    dxh = dy * gv
    dz = rstd * (dxh - jnp.mean(dxh, axis=-1, keepdims=True) - xhat * jnp.mean(dxh * xhat, axis=-1, keepdims=True))
    return dz, _colsum8(dy * xhat), _colsum8(dy)


def _ln_bwd(x, y, g, ua, ub, name):
    def fn(xv, yv, uav, ubv, gv):
        dz, dg, db = _ln_bwd_core(xv, yv, gv, ALPHA * uav + ubv)
        return dz, dz, dg, db
    return _rowwise(fn, name, [(x, 0, D_MODEL), (y, 0, D_MODEL), (ua, 0, D_MODEL), (ub, 0, D_MODEL)], [g],
                    [(D_MODEL, F32), (D_MODEL, BF16)], [D_MODEL, D_MODEL])


def _ln_bwd_loss(x, y, g, b, target, name):
    def fn(xv, yv, tv, gv, bv):
        xhat, _ = _ln_stats(ALPHA * xv + yv)
        err = xhat * gv + bv - tv
        dz, dg, db = _ln_bwd_core(xv, yv, gv, err * (1.0 / D_MODEL))
        return dz, dz, dg, db, _colsum8(err * err)
    return _rowwise(fn, name, [(x, 0, D_MODEL), (y, 0, D_MODEL), (target, 0, D_MODEL)], [g, b],
                    [(D_MODEL, F32), (D_MODEL, BF16)], [D_MODEL, D_MODEL, D_MODEL])


def _silu_parts(g):
    s = jax.nn.sigmoid(g)
    return g * s, s * (1.0 + g * (1.0 - s))


def _layer_fwd(x, xb, w, bf_pad, lb, norm_g, ln1_g, ln1_b, ln2_g, ln2_b, l, ride=None, on_ride=None):
    T = x.shape[0]
    tq, tk = min(T, FOX_TQ), min(T, FOX_TK)
    sv = {}
    qkv = _matmul(xb, w["in"], "nn", BF16, f"l{l}_qkv", n_off=0, n_size=QKV_W, tn=768)
    rest = _matmul(xb, w["in"], "nn", F32, f"l{l}_rest", n_off=QKV_W, n_size=REST_W, tn=384)
    fcum = _fgate_fwd(rest, bf_pad)
    frow = _head_rows(fcum, tk)
    o_a, ya, lse, *rode = _fox_fwd(qkv, frow, tq, tk, ride)
    if rode:
        on_ride(rode[0])
    o_b, states = _hgrn_fwd(rest, lb, C_QB, C_FB, C_IB)

    def yb_fn(ov, gbv, ngv):
        outs = []
        for h in range(B_HEADS):
            oh = ov[:, h * B_DK:(h + 1) * B_DK]
            r = lax.rsqrt(jnp.mean(oh * oh, axis=-1, keepdims=True) + RMS_EPS)
            outs.append(oh * r * ngv)
        return jnp.concatenate(outs, axis=1) * jax.nn.sigmoid(gbv)
    (yb,) = _rowwise(yb_fn, f"l{l}_yb", [(o_b, 0, B_WIDTH), (rest, C_GB, B_WIDTH)], [norm_g], [(B_WIDTH, BF16)])
    pa = _matmul(ya, w["pa"], "nn", F32, f"l{l}_pa")
    pb = _matmul(yb, w["pb"], "nn", F32, f"l{l}_pb")

    def merge_fn(gav, gbv, pav, pbv):
        return jax.nn.sigmoid(gav) * pav + jax.nn.sigmoid(gbv) * pbv
    (merged,) = _rowwise(merge_fn, f"l{l}_merge", [(rest, C_GA, D_MODEL), (rest, C_GB2, D_MODEL), (pa, 0, D_MODEL), (pb, 0, D_MODEL)],
                         [], [(D_MODEL, BF16)])
    mix = _matmul(merged, w["o"], "nn", F32, f"l{l}_mix")
    x1, x1b = _ln_fwd(x, mix, ln1_g, ln1_b, f"l{l}_ln1")
    hid = _matmul(x1b, w["ffin"], "nn", F32, f"l{l}_ffin", tn=512)

    def act_fn(uv, gv):
        return _silu_parts(gv)[0] * uv
    (act,) = _rowwise(act_fn, f"l{l}_act", [(hid, 0, FFN_HIDDEN), (hid, FFN_HIDDEN, FFN_HIDDEN)], [], [(FFN_HIDDEN, BF16)], tr=256)
    ffn = _matmul(act, w["ffout"], "nn", F32, f"l{l}_ffout", tm=1024)
    x2, x2b = _ln_fwd(x1, ffn, ln2_g, ln2_b, f"l{l}_ln2")
    sv.update(x=x, xb=xb, rest=rest, qkv=qkv, fcum=fcum, frow=frow, o_a=o_a, lse=lse, ya=ya, o_b=o_b, states=states,
              yb=yb, pa=pa, pb=pb, merged=merged, mix=mix, x1=x1, x1b=x1b, hid=hid, act=act, ffn=ffn)
    return x2, x2b, sv


def _layer_bwd(sv, w, dz2, dz2b, dln2, bf_pad, lb, norm_g, ln1_g, l, make_ride=None, make_late_ride=None):
    T = sv["x"].shape[0]
    tq, tk = min(T, FOX_BWD_TQ), min(T, FOX_TK)
    gr = {}
    dact = _matmul(dz2b, w["ffout"], "nt", F32, f"l{l}_dact", tm=1024, tn=1408)
    gr["ffout"] = _matmul(sv["act"], dz2b, "tn", F32, f"l{l}_dwffout", tm=1408, tn=512, tk=2048)

    def dhid_fn(dav, uv, gv):
        sg, dsg = _silu_parts(gv)
        return jnp.concatenate([dav * sg, dav * uv * dsg], axis=1)
    (dhid,) = _rowwise(dhid_fn, f"l{l}_dhid", [(dact, 0, FFN_HIDDEN), (sv["hid"], 0, FFN_HIDDEN), (sv["hid"], FFN_HIDDEN, FFN_HIDDEN)],
                       [], [(2 * FFN_HIDDEN, BF16)], tr=256)
    dx1f = _matmul(dhid, w["ffin"], "nt", F32, f"l{l}_dx1f", tm=1024)
    gr["ffin"] = _matmul(sv["x1b"], dhid, "tn", F32, f"l{l}_dwffin", tm=1024, tn=512, tk=4096)
    dz1, dz1b, dg1, db1 = _ln_bwd(sv["x"], sv["mix"], ln1_g, dz2, dx1f, f"l{l}_dln1")
    dmerged = _matmul(dz1b, w["o"], "nt", F32, f"l{l}_dmerged")
    gr["o"] = _matmul(sv["merged"], dz1b, "tn", F32, f"l{l}_dwo", tm=1024, tn=512, tk=4096)

    def dmerge_fn(dmv, gav, gbv, pav, pbv):
        sa, sb = jax.nn.sigmoid(gav), jax.nn.sigmoid(gbv)
        return dmv * sa, dmv * sb, dmv * pav * sa * (1.0 - sa), dmv * pbv * sb * (1.0 - sb)
    dpa, dpb, dga, dgb2 = _rowwise(
        dmerge_fn, f"l{l}_dmerge",
        [(dmerged, 0, D_MODEL), (sv["rest"], C_GA, D_MODEL), (sv["rest"], C_GB2, D_MODEL), (sv["pa"], 0, D_MODEL), (sv["pb"], 0, D_MODEL)],
        [], [(D_MODEL, BF16)] * 4)
    dya = _matmul(dpa, w["pa"], "nt", BF16, f"l{l}_dya", tn=512)
    gr["pa"] = _matmul(sv["ya"], dpa, "tn", F32, f"l{l}_dwpa", tm=512, tn=512, tk=4096)
    dyb = _matmul(dpb, w["pb"], "nt", F32, f"l{l}_dyb", tn=512)
    gr["pb"] = _matmul(sv["yb"], dpb, "tn", F32, f"l{l}_dwpb", tm=512, tn=512, tk=4096)

    def dyb_fn(dyv, ov, gbv, ngv):
        sg = jax.nn.sigmoid(gbv)
        dos, dgs = [], []
        dng = jnp.zeros((8, B_DK), F32)
        for h in range(B_HEADS):
            sl = slice(h * B_DK, (h + 1) * B_DK)
            oh, dyh, sgh = ov[:, sl], dyv[:, sl], sg[:, sl]
            r = lax.rsqrt(jnp.mean(oh * oh, axis=-1, keepdims=True) + RMS_EPS)
            n = oh * r
            dyn = dyh * sgh
            dgs.append(dyh * n * ngv * sgh * (1.0 - sgh))
            dng = dng + _colsum8(dyn * n)
            dn = dyn * ngv
            dos.append(r * (dn - n * jnp.mean(dn * n, axis=-1, keepdims=True)))
        return jnp.concatenate(dos, axis=1), jnp.concatenate(dgs, axis=1), dng
    do_b, dgb, dng = _rowwise(dyb_fn, f"l{l}_dyb2", [(dyb, 0, B_WIDTH), (sv["o_b"], 0, B_WIDTH), (sv["rest"], C_GB, B_WIDTH)], [norm_g],
                              [(B_WIDTH, F32), (B_WIDTH, BF16)], [B_DK])
    dqb, dfb, dib, dlb = _hgrn_bwd(sv["rest"], lb, sv["states"], do_b, C_QB, C_FB, C_IB)
    delta = _fox_delta(dya, sv["o_a"])
    ride = (make_ride(gr), False) if make_ride else None
    dq, dk, dv, dfk, dfq, *rode = _fox_bwd(sv["qkv"], dya, _pair_rows(sv["lse"], tq), _pair_rows(delta, tq), sv["fcum"], tq, tk, ride)
    dfq_pad = jnp.pad(dfq.reshape(A_HEADS, T).T, ((0, 0), (0, LANE - A_HEADS)))
    dfa, dbf = _fgate_bwd(dfq_pad, dfk, sv["rest"], bf_pad)
    dproj = jnp.concatenate([dq.astype(BF16), dk, dv, dga, dgb2, dqb, dfb, dib, dgb, dfa], axis=1)
    gr["in"] = _matmul(sv["xb"], dproj, "tn", F32, f"l{l}_dwin", tm=1024, tn=640, tk=4096)
    late = (make_late_ride(gr), False) if make_late_ride else None
    dxm, *rode_late = _matmul(dproj, w["in"], "nt", F32, f"l{l}_dxm", tm=1024, tk=1920, ride=late, addend=(dz1, ALPHA)) if late else (
        _matmul(dproj, w["in"], "nt", F32, f"l{l}_dxm", tm=1024, tk=1920),)
    small = dict(dln2=dln2, dln1=(dg1, db1), dng=dng, dlb=dlb, dbf=dbf)
    return dz1, dxm, gr, small, rode, rode_late


MESH = pl.DeviceIdType.MESH
ANY = pl.BlockSpec(memory_space=pl.ANY)


def _me_and_peers():
    x, y, c = lax.axis_index("x"), lax.axis_index("y"), lax.axis_index("c")
    me = 4 * x + 2 * y + c
    peers = []
    for p in range(1, N_DEV):
        px, py, pc = x ^ ((p >> 2) & 1), y ^ ((p >> 1) & 1), c ^ (p & 1)
        peers.append(((px, py, pc), 4 * px + 2 * py + pc))
    return me, peers


PUSH_SCRATCH = [pltpu.SemaphoreType.DMA((N_DEV - 1,)), pltpu.SemaphoreType.DMA((N_DEV - 1,)), pltpu.SemaphoreType.DMA]


def _push_copies(gather, src_ref, out_ref, send_sems, recv_sems, local_sem):
    me, peers = _me_and_peers()
    part = (lambda idx: src_ref) if gather else (lambda idx: src_ref.at[idx])
    mine = lambda: pltpu.make_async_copy(part(me), out_ref.at[me], local_sem)
    mk = lambda p, dev, idx, slot: pltpu.make_async_remote_copy(
        src_ref=part(idx), dst_ref=out_ref.at[slot], send_sem=send_sems.at[p], recv_sem=recv_sems.at[p], device_id=dev, device_id_type=MESH)
    sends = lambda: [mk(p, dev, idx, me) for p, (dev, idx) in enumerate(peers)]
    recvs = lambda: [mk(p, dev, idx, idx) for p, (dev, idx) in enumerate(peers)]
    return mine, sends, recvs


def _push_start(gather, *refs):
    mine, sends, _ = _push_copies(gather, *refs)
    mine().start()
    for cp in sends():
        cp.start()


def _push_wait(gather, *refs):
    mine, sends, recvs = _push_copies(gather, *refs)
    for cp in recvs():
        cp.wait_recv()
    for cp in sends():
        cp.wait_send()
    mine().wait()


def _push_shape(src, gather):
    return jax.ShapeDtypeStruct(((N_DEV,) + src.shape) if gather else src.shape, src.dtype)


def _push(src, gather, name, vmem=False):
    def body(*refs):
        _push_start(gather, *refs)
        _push_wait(gather, *refs)

    spec = pl.BlockSpec(memory_space=pltpu.VMEM) if vmem else ANY
    return pl.pallas_call(body, name=name, in_specs=[spec], out_specs=spec, out_shape=_push_shape(src, gather),
                          scratch_shapes=PUSH_SCRATCH)(src)


def _sum_adam(parts, w, m, v, name, tr=ADAM_ROWS):
    _, R, W = parts.shape
    tr = _pick(R, tr) if R % LANE == 0 else R
    c1 = 1.0 / (1.0 - ADAM_B1 ** ADAM_STEP)
    c2 = 1.0 / (1.0 - ADAM_B2 ** ADAM_STEP)

    def body(p_ref, w_ref, m_ref, v_ref, g_ref, d_ref, nm_ref, nv_ref):
        g = p_ref[0]
        for s in range(1, N_DEV):
            g = g + p_ref[s]
        nm = ADAM_B1 * m_ref[...] + (1.0 - ADAM_B1) * g
        nv = ADAM_B2 * v_ref[...] + (1.0 - ADAM_B2) * (g * g)
        g_ref[...] = g
        nm_ref[...] = nm
        nv_ref[...] = nv
        d_ref[...] = -ADAM_LR * ((nm * c1) / (jnp.sqrt(nv * c2) + ADAM_EPS) + ADAM_WD * w_ref[...])

    blk = pl.BlockSpec((tr, W), lambda i: (i, 0))
    return pl.pallas_call(
        body,
        name=name,
        grid=(R // tr,),
        in_specs=[pl.BlockSpec((N_DEV, tr, W), lambda i: (0, i, 0)), blk, blk, blk],
        out_specs=[blk] * 4,
        out_shape=[jax.ShapeDtypeStruct((R, W), F32)] * 4,
        compiler_params=_cparams(("parallel",)),
    )(parts, w, m, v)


BIG = ("in", "pa", "pb", "o", "ffin", "ffout")
COL_SHARDED = {"in": True, "pa": True, "pb": True, "o": False, "ffin": True, "ffout": False}
SHARD_ROWS = {"in": 5640, "pa": 512, "pb": 512, "o": 1024, "ffin": 5632, "ffout": 2816}
_round_up = lambda n, m: -(-n // m) * m
_group_rows = lambda group: _round_up(sum(SHARD_ROWS[n] for n, _ in group), ADAM_ROWS)
GATHER_FIRST = (("in", 0),)
GATHER_RIDE = tuple((n, 0) for n in BIG[1:]) + tuple((n, 1) for n in BIG)
FIRST_ROWS, RIDE_ROWS = _group_rows(GATHER_FIRST), _group_rows(GATHER_RIDE)
EARLY = tuple((n, 1) for n in BIG) + tuple((n, 0) for n in BIG[1:])
LATE = (("in", 0),)
EARLY_ROWS, LATE_ROWS = _group_rows(EARLY), _group_rows(LATE)
SMALL_ROWS = 80


class _LayerWeights:
    def __init__(self, full, layer):
        self.full, self.layer = full, layer

    def __getitem__(self, name):
        return self.full[(name, self.layer)]


def _pack(parts, rows, lead=()):
    n = len(lead)
    cat = jnp.concatenate([p.reshape(lead + (-1, LANE)) for p in parts], axis=n)
    return jnp.pad(cat, ((0, 0),) * n + ((0, rows - cat.shape[n]), (0, 0)))


def _unpack(packed, shapes, lead=()):
    out, off = [], 0
    for s in shapes:
        r = math.prod(s) // LANE
        out.append(packed[..., off:off + r, :].reshape(lead + tuple(s)))
        off += r
    return out


def _pack_flat(parts, rows):
    flat = jnp.concatenate([p.reshape(-1) for p in parts])
    return jnp.pad(flat, (0, rows * LANE - flat.shape[0])).reshape(rows, LANE)


def _unpack_flat(packed, shapes):
    flat = packed.reshape(-1)
    out, off = [], 0
    for s in shapes:
        n = math.prod(s)
        out.append(flat[off:off + n].reshape(s))
        off += n
    return out


def _join_shards(g, col_sharded):
    if col_sharded:
        return g.transpose(1, 0, 2).reshape(g.shape[1], N_DEV * g.shape[2])
    return g.reshape(N_DEV * g.shape[1], g.shape[2])


def _split_shards(full, col_sharded):
    a, b = full.shape
    if col_sharded:
        return full.reshape(a, N_DEV, b // N_DEV).transpose(1, 0, 2)
    return full.reshape(N_DEV, a // N_DEV, b)


def _permute_in(w):
    a, b = QKV_W + A_HEADS, QKV_W + A_HEADS + 4 * B_WIDTH
    return jnp.concatenate([w[:, :QKV_W], w[:, b:], w[:, a:b], w[:, QKV_W:a],
                            jnp.zeros((w.shape[0], LANE - A_HEADS), w.dtype)], axis=1)


def _unpermute_in(d):
    g0, h0, f0 = QKV_W, QKV_W + 2 * D_MODEL, QKV_W + C_FA
    return jnp.concatenate([d[:, :QKV_W], d[:, f0:f0 + A_HEADS], d[:, h0:f0], d[:, g0:h0]], axis=1)


def _lower_bounds(logits):
    sm = jax.nn.softmax(logits.astype(F32), axis=0)
    return jnp.cumsum(sm, axis=0) - sm[0:1]


def kernel(x, w_in, b_fgate, hgrn_lb_logits, hgrn_norm_g, w_branch_a, w_branch_b, w_out, ln1_g, ln1_b, w_ff_in, w_ff_out, ln2_g, ln2_b, loss_target, m_w_in, m_b_fgate, m_hgrn_lb_logits, m_hgrn_norm_g, m_w_branch_a, m_w_branch_b, m_w_out, m_ln1_g, m_ln1_b, m_w_ff_in, m_w_ff_out, m_ln2_g, m_ln2_b, v_w_in, v_b_fgate, v_hgrn_lb_logits, v_hgrn_norm_g, v_w_branch_a, v_w_branch_b, v_w_out, v_ln1_g, v_ln1_b, v_w_ff_in, v_w_ff_out, v_ln2_g, v_ln2_b):
    big = {"in": (w_in, m_w_in, v_w_in), "pa": (w_branch_a, m_w_branch_a, v_w_branch_a), "pb": (w_branch_b, m_w_branch_b, v_w_branch_b),
           "o": (w_out, m_w_out, v_w_out), "ffin": (w_ff_in, m_w_ff_in, v_w_ff_in), "ffout": (w_ff_out, m_w_ff_out, v_w_ff_out)}
    small = [(b_fgate, m_b_fgate, v_b_fgate), (hgrn_lb_logits, m_hgrn_lb_logits, v_hgrn_lb_logits), (hgrn_norm_g, m_hgrn_norm_g, v_hgrn_norm_g),
             (ln1_g, m_ln1_g, v_ln1_g), (ln1_b, m_ln1_b, v_ln1_b), (ln2_g, m_ln2_g, v_ln2_g), (ln2_b, m_ln2_b, v_ln2_b)]
    small_shapes = [s[0].shape for s in small]
    shard_shape = lambda n: big[n][0].shape[1:]

    full = {}

    def weight_pack(group, rows):
        return _pack([big[n][0][l].astype(BF16) for n, l in group], rows)

    def take_weights(gathered, group):
        for (n, l), g in zip(group, _unpack(gathered, [shard_shape(n) for n, _ in group], lead=(N_DEV,))):
            w = _join_shards(g, COL_SHARDED[n])
            full[(n, l)] = _permute_in(w) if n == "in" else w

    take_weights(_push(weight_pack(GATHER_FIRST, FIRST_ROWS), True, "gather_weights"), GATHER_FIRST)
    weights = [_LayerWeights(full, l) for l in range(DEPTH)]
    lbounds = _lower_bounds(hgrn_lb_logits)

    def layer_params(l):
        bf_pad = jnp.pad(b_fgate[l].astype(F32), (0, LANE - A_HEADS)).reshape(1, LANE)
        row = lambda a: a[l].astype(F32).reshape(1, -1)
        return dict(bf_pad=bf_pad, lb=lbounds[l].reshape(1, -1), norm_g=row(hgrn_norm_g), ln1_g=row(ln1_g), ln1_b=row(ln1_b),
                    ln2_g=row(ln2_g), ln2_b=row(ln2_b))

    xl = x[0]
    xlb = xl.astype(BF16)
    saved = []
    for l in range(DEPTH):
        p = layer_params(l)
        ride = (weight_pack(GATHER_RIDE, RIDE_ROWS), True) if l == 0 else None
        xl, xlb, sv = _layer_fwd(xl, xlb, weights[l], p["bf_pad"], p["lb"], p["norm_g"], p["ln1_g"], p["ln1_b"],
                                 p["ln2_g"], p["ln2_b"], l, ride, lambda g: take_weights(g, GATHER_RIDE))
        saved.append(sv)

    grads = [None] * DEPTH
    smalls = [None] * DEPTH

    def contributions(group, rows):
        parts = []
        for n, l in group:
            g = _unpermute_in(grads[l][n]) if n == "in" else grads[l][n]
            parts.append(_split_shards(g, COL_SHARDED[n]))
        return _pack(parts, rows, lead=(N_DEV,))

    def early_ride(gr0):
        grads[0] = gr0
        return contributions(EARLY, EARLY_ROWS)

    loss_part = None
    ua = ub = None
    received = {}
    for l in reversed(range(DEPTH)):
        p, sv = layer_params(l), saved[l]
        if l == DEPTH - 1:
            dz2, dz2b, dg2, db2, loss_part = _ln_bwd_loss(sv["x1"], sv["ffn"], p["ln2_g"], p["ln2_b"], loss_target[0], f"l{l}_dln2")
        else:
            dz2, dz2b, dg2, db2 = _ln_bwd(sv["x1"], sv["ffn"], p["ln2_g"], ua, ub, f"l{l}_dln2")
        rides = (early_ride, lambda gr0: contributions(LATE, LATE_ROWS)) if l == 0 else (None, None)
        ua, ub, grads[l], smalls[l], rode, rode_late = _layer_bwd(sv, weights[l], dz2, dz2b, (dg2, db2), p["bf_pad"], p["lb"],
                                                                  p["norm_g"], p["ln1_g"], l, *rides)
        if rode:
            received["early"], received["late"] = rode[0], rode_late[0]
    grad_x = ub

    by_kind = [{} for _ in range(4)]
    for key, group, rows in (("early", EARLY, EARLY_ROWS), ("late", LATE, LATE_ROWS)):
        wmv = [_pack([big[n][i][l].astype(F32) for n, l in group], rows) for i in range(3)]
        shapes = [shard_shape(n) for n, _ in group]
        for kind, packed in enumerate(_sum_adam(received[key], *wmv, f"adam_{key}")):
            by_kind[kind].update(zip(group, _unpack(packed, shapes)))
    outs_big = [[jnp.stack([d[(n, l)] for l in range(DEPTH)]) for n in BIG] for d in by_kind]

    fold = lambda a: jnp.sum(a, axis=0)
    dlb = jnp.stack([fold(smalls[l]["dlb"]) for l in range(DEPTH)])
    _, lb_vjp = jax.vjp(_lower_bounds, hgrn_lb_logits)
    small_grads = [jnp.stack([fold(smalls[l]["dbf"])[:A_HEADS] for l in range(DEPTH)]), lb_vjp(dlb)[0],
                   jnp.stack([fold(smalls[l]["dng"]) for l in range(DEPTH)]),
                   jnp.stack([fold(smalls[l]["dln1"][0]) for l in range(DEPTH)]), jnp.stack([fold(smalls[l]["dln1"][1]) for l in range(DEPTH)]),
                   jnp.stack([fold(smalls[l]["dln2"][0]) for l in range(DEPTH)]), jnp.stack([fold(smalls[l]["dln2"][1]) for l in range(DEPTH)])]
    loss_local = (0.5 / D_MODEL) * jnp.sum(loss_part)
    small_all = _push(_pack_flat(small_grads + [loss_local.reshape(1)], SMALL_ROWS), True, "gather_small", vmem=True)
    one = jnp.zeros((1,), F32)
    swmv = [_pack_flat([s[i].astype(F32) for s in small] + [one], SMALL_ROWS) for i in range(3)]
    outs_small = [_unpack_flat(o, small_shapes + [(1,)]) for o in _sum_adam(small_all, *swmv, "adam_small")]
    loss = outs_small[0][-1].reshape(())

    order = {"in": 0, "pa": 4, "pb": 5, "o": 6, "ffin": 9, "ffout": 10}
    small_pos = [1, 2, 3, 7, 8, 11, 12]
    result = [loss, grad_x.reshape(x.shape)]
    for kind in range(4):
        slot = [None] * 13
        for n, arr in zip(BIG, outs_big[kind]):
            slot[order[n]] = arr
        for pos, arr in zip(small_pos, outs_small[kind][:-1]):
            slot[pos] = arr
        result.extend(slot)
    return tuple(result)
```
